```python
import jax
import jax.numpy as jnp
from jax import lax
import numpy as np

D_MODEL = 2048
BATCH = 1
SEQ = 8192
DEPTH = 4

ATT_HEADS = 8
ATT_HEAD_DIM = 128
ATT_WIDTH = ATT_HEADS * ATT_HEAD_DIM
ROPE_DIM = ATT_HEAD_DIM // 4
ROPE_THETA = 500000.0
MOBA_BLOCK = 256
MOBA_TOPK = 3
MOBA_Q_CHUNK = 64

RET_HEADS = 8
RET_QK_DIM = 64
RET_V_DIM = 128
RET_QK_WIDTH = RET_HEADS * RET_QK_DIM
RET_V_WIDTH = RET_HEADS * RET_V_DIM
RET_CHUNK = 256
RET_ROPE_THETA = 10000.0

POOL_WINDOWS = (2, 4, 8, 16)
POOL_GROUPS = 4
POOL_GROUP_DIM = 256
POOL_WIDTH = POOL_GROUPS * POOL_GROUP_DIM

N_BRANCH = 3
BRANCH_WIDTH = 1024
IN_SPLITS = (ATT_WIDTH, ATT_WIDTH, ATT_WIDTH, RET_QK_WIDTH, RET_QK_WIDTH,
             RET_V_WIDTH, RET_V_WIDTH, POOL_WIDTH, N_BRANCH * D_MODEL)
N_IN = sum(IN_SPLITS)

N_GROUPS = 4
EXPERTS_PER_GROUP = 8
N_EXPERTS = N_GROUPS * EXPERTS_PER_GROUP
TOP_K = 2
D_EXPERT = 512
MOE_BLOCK = 256

DN_ALPHA = (2 * DEPTH) ** 0.25
DN_BETA = (8 * DEPTH) ** -0.25
LN_EPS = 1e-5
NEG = -1e30

kernel_name = "hybrid_moba_retention_pool_hmoe"


def layer_norm(x, g, b):
    xf = x.astype(jnp.float32)
    mu = jnp.mean(xf, axis=-1, keepdims=True)
    var = jnp.mean(jnp.square(xf - mu), axis=-1, keepdims=True)
    return ((xf - mu) * lax.rsqrt(var + LN_EPS) * g + b).astype(x.dtype)


def split_heads(a, n_heads):
    b, s, w = a.shape
    return a.reshape(b, s, n_heads, w // n_heads).transpose(0, 2, 1, 3)


def merge_heads(a):
    b, h, s, d = a.shape
    return a.transpose(0, 2, 1, 3).reshape(b, s, h * d)


def rotary(x, rot_dim, theta):
    s = x.shape[2]
    half = rot_dim // 2
    inv_freq = 1.0 / (theta ** (jnp.arange(0, rot_dim, 2, dtype=jnp.float32) / rot_dim))
    ang = jnp.arange(s, dtype=jnp.float32)[:, None] * inv_freq[None, :]
    cos, sin = jnp.cos(ang), jnp.sin(ang)
    xr = x[..., :rot_dim].astype(jnp.float32)
    x1, x2 = xr[..., :half], xr[..., half:]
    rot = jnp.concatenate([x1 * cos - x2 * sin, x1 * sin + x2 * cos], axis=-1).astype(x.dtype)
    return jnp.concatenate([rot, x[..., rot_dim:]], axis=-1)


def moba_attention(q, k, v):
    b, h, s, hd = q.shape
    nb = -(-s // MOBA_BLOCK)
    sp = nb * MOBA_BLOCK
    pad = ((0, 0), (0, 0), (0, sp - s), (0, 0))
    q, k, v = jnp.pad(q, pad), jnp.pad(k, pad), jnp.pad(v, pad)
    kb = k.reshape(b, h, nb, MOBA_BLOCK, hd)
    vb = v.reshape(b, h, nb, MOBA_BLOCK, hd)
    kmean = jnp.mean(kb.astype(jnp.float32), axis=3)
    gate = jnp.einsum('bhsd,bhnd->bhsn', q.astype(jnp.float32), kmean)
    q_blk = jnp.arange(sp) // MOBA_BLOCK
    past = jnp.arange(nb)[None, :] < q_blk[:, None]
    gate = jnp.where(past, gate, NEG)
    k_sel = min(MOBA_TOPK, nb)
    _, sel = lax.top_k(gate, k_sel)
    sel_valid = sel < q_blk[:, None]

    nc = sp // MOBA_Q_CHUNK

    def to_chunks(a):
        a = a.reshape(b, h, nc, MOBA_Q_CHUNK, *a.shape[3:])
        return jnp.moveaxis(a, 2, 0)

    scale = hd ** -0.5
    b_idx = jnp.arange(b)[:, None, None, None]
    h_idx = jnp.arange(h)[None, :, None, None]
    key_off = jnp.arange(MOBA_BLOCK)

    def chunk_fn(args):
        qc, selc, validc, c = args
        t = c * MOBA_Q_CHUNK + jnp.arange(MOBA_Q_CHUNK)
        own = (c * MOBA_Q_CHUNK) // MOBA_BLOCK
        k_g = kb[b_idx, h_idx, selc]
        v_g = vb[b_idx, h_idx, selc]
        s_sel = jnp.einsum('bhqd,bhqkpd->bhqkp', qc, k_g,
                           preferred_element_type=jnp.float32) * scale
        s_sel = jnp.where(validc[..., None], s_sel, NEG)
        s_sel = s_sel.reshape(b, h, MOBA_Q_CHUNK, k_sel * MOBA_BLOCK)
        k_own = lax.dynamic_index_in_dim(kb, own, axis=2, keepdims=False)
        v_own = lax.dynamic_index_in_dim(vb, own, axis=2, keepdims=False)
        s_own = jnp.einsum('bhqd,bhpd->bhqp', qc, k_own,
                           preferred_element_type=jnp.float32) * scale
        causal = (own * MOBA_BLOCK + key_off)[None, :] <= t[:, None]
        s_own = jnp.where(causal, s_own, NEG)
        p = jax.nn.softmax(jnp.concatenate([s_sel, s_own], axis=-1), axis=-1)
        p_sel = p[..., :k_sel * MOBA_BLOCK].reshape(b, h, MOBA_Q_CHUNK, k_sel, MOBA_BLOCK).astype(v.dtype)
        p_own = p[..., k_sel * MOBA_BLOCK:].astype(v.dtype)
        return (jnp.einsum('bhqkp,bhqkpd->bhqd', p_sel, v_g)
                + jnp.einsum('bhqp,bhpd->bhqd', p_own, v_own))

    out = lax.map(chunk_fn, (to_chunks(q), to_chunks(sel), to_chunks(sel_valid),
                             jnp.arange(nc)))
    out = jnp.moveaxis(out, 0, 2).reshape(b, h, sp, hd)
    return out[:, :, :s]


def retention(q, k, v):
    b, h, s, dk = q.shape
    dv = v.shape[-1]
    n = -(-s // RET_CHUNK)
    sp = n * RET_CHUNK
    c = RET_CHUNK
    pad = ((0, 0), (0, 0), (0, sp - s), (0, 0))
    q = jnp.pad(q, pad).astype(jnp.float32).reshape(b, h, n, c, dk)
    k = jnp.pad(k, pad).astype(jnp.float32).reshape(b, h, n, c, dk) * (dk ** -0.5)
    v = jnp.pad(v, pad).astype(jnp.float32).reshape(b, h, n, c, dv)
    log_gamma = jnp.log1p(-jnp.exp2(-5.0 - jnp.arange(h, dtype=jnp.float32)))
    pos = jnp.arange(c, dtype=jnp.float32)
    diff = pos[:, None] - pos[None, :]
    decay = jnp.where(diff >= 0, jnp.exp(log_gamma[:, None, None] * jnp.maximum(diff, 0.0)), 0.0)
    inner = jnp.einsum('bhncd,bhnmd->bhncm', q, k) * decay[None, :, None]
    inner = jnp.einsum('bhncm,bhnme->bhnce', inner, v)
    k_decay = jnp.exp(log_gamma[:, None] * (c - 1.0 - pos)[None, :])
    q_decay = jnp.exp(log_gamma[:, None] * (pos + 1.0)[None, :])
    chunk_decay = jnp.exp(log_gamma * c)[None, :, None, None]
    kv = jnp.einsum('bhncd,bhnce->bhnde', k * k_decay[None, :, None, :, None], v)

    def step(state, kv_i):
        return state * chunk_decay + kv_i, state

    _, prev = lax.scan(step, jnp.zeros((b, h, dk, dv), jnp.float32), jnp.moveaxis(kv, 2, 0))
    prev = jnp.moveaxis(prev, 0, 2)
    cross = jnp.einsum('bhncd,bhnde->bhnce', q * q_decay[None, :, None, :, None], prev)
    return (inner + cross).reshape(b, h, sp, dv)[:, :, :s]


def multi_scale_pool(p, w_pool, pool_scale):
    b, s, _ = p.shape
    pf = p.astype(jnp.float32)
    cs = jnp.cumsum(pf, axis=1)
    n_avail = jnp.arange(1, s + 1, dtype=jnp.float32)
    outs = []
    for gi, w in enumerate(POOL_WINDOWS):
        lo, hi = gi * POOL_GROUP_DIM, (gi + 1) * POOL_GROUP_DIM
        cg = cs[..., lo:hi]
        lower = jnp.pad(cg, ((0, 0), (w, 0), (0, 0)))[:, :s]
        mean = (cg - lower) / jnp.minimum(n_avail, float(w))[None, :, None]
        outs.append(mean - pf[..., lo:hi])
    d = jnp.stack(outs, axis=2)
    y = jnp.einsum('bsgc,gce->bsge', d, w_pool).reshape(b, s, POOL_WIDTH)
    return (y * pool_scale).astype(p.dtype)


def token_mixer(x, w_in, ret_gain, w_pool, pool_scale, w_branch, w_out):
    b, s, _ = x.shape
    offsets = np.cumsum(IN_SPLITS)[:-1].tolist()
    q_a, k_a, v_a, q_r, k_r, v_r, g_r, p_in, gate_logits = jnp.split(x @ w_in, offsets, axis=-1)
    q_a = rotary(split_heads(q_a, ATT_HEADS), ROPE_DIM, ROPE_THETA)
    k_a = rotary(split_heads(k_a, ATT_HEADS), ROPE_DIM, ROPE_THETA)
    y_a = merge_heads(moba_attention(q_a, k_a, split_heads(v_a, ATT_HEADS)))
    q_r = rotary(split_heads(q_r, RET_HEADS), RET_QK_DIM, RET_ROPE_THETA)
    k_r = rotary(split_heads(k_r, RET_HEADS), RET_QK_DIM, RET_ROPE_THETA)
    r = retention(q_r, k_r, split_heads(v_r, RET_HEADS))
    mu = jnp.mean(r, axis=-1, keepdims=True)
    var = jnp.mean(jnp.square(r - mu), axis=-1, keepdims=True)
    r = merge_heads((r - mu) * lax.rsqrt(var + LN_EPS)) * ret_gain
    y_r = (jax.nn.silu(g_r.astype(jnp.float32)) * r).astype(x.dtype)
    y_p = multi_scale_pool(p_in, w_pool, pool_scale)
    branches = jnp.stack([y_a, y_r, y_p], axis=2)
    y_br = jnp.einsum('bsnc,ncd->bsnd', branches, w_branch)
    gates = jax.nn.sigmoid(gate_logits.reshape(b, s, N_BRANCH, D_MODEL))
    merged = jnp.sum(gates * y_br, axis=2)
    return merged @ w_out


def routed_experts(h, expert, weight, w_gate, w_up, w_down):
    t, d = h.shape
    a = t * TOP_K
    flat_e = expert.reshape(a)
    order = jnp.argsort(flat_e)
    e_sorted = flat_e[order]
    tok_sorted = order // TOP_K
    w_sorted = weight.reshape(a)[order]
    counts = jnp.bincount(flat_e, length=N_EXPERTS)
    padded = (counts + MOE_BLOCK - 1) // MOE_BLOCK * MOE_BLOCK
    pad_end = jnp.cumsum(padded)
    pad_start = pad_end - padded
    start = jnp.cumsum(counts) - counts
    slot = pad_start[e_sorted] + jnp.arange(a) - start[e_sorted]
    n_blocks = -(-a // MOE_BLOCK) + N_EXPERTS
    n_slots = n_blocks * MOE_BLOCK
    slot_tok = jnp.zeros((n_slots,), jnp.int32).at[slot].set(tok_sorted.astype(jnp.int32))
    block_expert = jnp.minimum(
        jnp.searchsorted(pad_end, jnp.arange(n_blocks) * MOE_BLOCK, side='right'), N_EXPERTS - 1)
    xb = h[slot_tok].reshape(n_blocks, MOE_BLOCK, d)

    def expert_block(args):
        xe, e = args
        return (jax.nn.silu(xe @ w_gate[e]) * (xe @ w_up[e])) @ w_down[e]

    yb = lax.map(expert_block, (xb, block_expert)).reshape(n_slots, d)
    y_assign = yb[slot] * w_sorted[:, None].astype(yb.dtype)
    return jnp.zeros((t, d), yb.dtype).at[tok_sorted].add(y_assign)


def hierarchical_moe(x, w_r1, b_r1, w_r2, b_r2, w_e_gate, w_e_up, w_e_down):
    b, s, d = x.shape
    h = x.reshape(b * s, d)
    t = h.shape[0]
    p1 = jax.nn.softmax((h @ w_r1).astype(jnp.float32) + b_r1, axis=-1)
    g_top, g_idx = lax.top_k(p1, 1)
    logits2 = ((h @ w_r2).astype(jnp.float32) + b_r2).reshape(t, N_GROUPS, EXPERTS_PER_GROUP)
    logits2 = jnp.take_along_axis(logits2, g_idx[:, :, None], axis=1)[:, 0]
    p2 = jax.nn.softmax(logits2, axis=-1)
    e_top, e_idx = lax.top_k(p2, TOP_K)
    weight = g_top * e_top / jnp.sum(e_top, axis=-1, keepdims=True)
    expert = g_idx * EXPERTS_PER_GROUP + e_idx
    y = routed_experts(h, expert, weight, w_e_gate, w_e_up, w_e_down)
    return y.reshape(b, s, d).astype(x.dtype)


def setup_inputs(seed: int = 0) -> dict:
    key = jax.random.key(seed)
    ks = jax.random.split(key, 20)
    f32 = jnp.float32
    L, D = DEPTH, D_MODEL

    def nrm(k, shape, scale):
        return jax.random.normal(k, shape, f32) * scale

    return {
        "x": nrm(ks[0], (BATCH, SEQ, D), 1.0),
        "w_in": nrm(ks[1], (L, D, N_IN), D ** -0.5),
        "ret_gain": 1.0 + nrm(ks[2], (L, RET_V_WIDTH), 0.02),
        "w_pool": nrm(ks[3], (L, POOL_GROUPS, POOL_GROUP_DIM, POOL_GROUP_DIM), POOL_GROUP_DIM ** -0.5),
        "pool_scale": 1.0 + nrm(ks[4], (L, POOL_WIDTH), 0.02),
        "w_branch": nrm(ks[5], (L, N_BRANCH, BRANCH_WIDTH, D), BRANCH_WIDTH ** -0.5),
        "w_out": nrm(ks[6], (L, D, D), DN_BETA * D ** -0.5),
        "ln1_g": 1.0 + nrm(ks[7], (L, D), 0.02),
        "ln1_b": nrm(ks[8], (L, D), 0.02),
        "w_r1": nrm(ks[9], (L, D, N_GROUPS), D ** -0.5),
        "b_r1": nrm(ks[10], (L, N_GROUPS), 0.01),
        "w_r2": nrm(ks[11], (L, D, N_EXPERTS), D ** -0.5),
        "b_r2": nrm(ks[12], (L, N_EXPERTS), 0.01),
        "w_e_gate": nrm(ks[13], (L, N_EXPERTS, D, D_EXPERT), D ** -0.5),
        "w_e_up": nrm(ks[14], (L, N_EXPERTS, D, D_EXPERT), D ** -0.5),
        "w_e_down": nrm(ks[15], (L, N_EXPERTS, D_EXPERT, D), DN_BETA * D_EXPERT ** -0.5),
        "ln2_g": 1.0 + nrm(ks[16], (L, D), 0.02),
        "ln2_b": nrm(ks[17], (L, D), 0.02),
    }


def reference(x, w_in, ret_gain, w_pool, pool_scale, w_branch, w_out, ln1_g, ln1_b,
              w_r1, b_r1, w_r2, b_r2, w_e_gate, w_e_up, w_e_down, ln2_g, ln2_b):
    for l in range(DEPTH):
        mix = token_mixer(x, w_in[l], ret_gain[l], w_pool[l], pool_scale[l], w_branch[l], w_out[l])
        x = layer_norm(DN_ALPHA * x + mix, ln1_g[l], ln1_b[l])
        ffn = hierarchical_moe(x, w_r1[l], b_r1[l], w_r2[l], b_r2[l],
                               w_e_gate[l], w_e_up[l], w_e_down[l])
        x = layer_norm(DN_ALPHA * x + ffn, ln2_g[l], ln2_b[l])
    return x
```

```python
import functools

import numpy as np
import jax
import jax.numpy as jnp
from jax import lax
from jax.experimental import pallas as pl
from jax.experimental.pallas import tpu as pltpu

D_MODEL = 2048

ATT_HEADS = 8
ATT_HEAD_DIM = 128
ATT_WIDTH = ATT_HEADS * ATT_HEAD_DIM
ROPE_DIM = ATT_HEAD_DIM // 4
ROPE_THETA = 500000.0
MOBA_BLOCK = 256
MOBA_TOPK = 3

RET_HEADS = 8
RET_QK_DIM = 64
RET_V_DIM = 128
RET_QK_WIDTH = RET_HEADS * RET_QK_DIM
RET_V_WIDTH = RET_HEADS * RET_V_DIM
RET_CHUNK = 256
RET_ROPE_THETA = 10000.0

POOL_WINDOWS = (2, 4, 8, 16)
POOL_GROUPS = 4
POOL_GROUP_DIM = 256
POOL_WIDTH = POOL_GROUPS * POOL_GROUP_DIM
POOL_HALO = 16

N_BRANCH = 3
BRANCH_WIDTH = 1024

OFF_QA = 0
OFF_KA = OFF_QA + ATT_WIDTH
OFF_VA = OFF_KA + ATT_WIDTH
OFF_QR = OFF_VA + ATT_WIDTH
OFF_KR = OFF_QR + RET_QK_WIDTH
OFF_VR = OFF_KR + RET_QK_WIDTH
OFF_GR = OFF_VR + RET_V_WIDTH
OFF_POOL = OFF_GR + RET_V_WIDTH
OFF_GATE = OFF_POOL + POOL_WIDTH
N_IN = OFF_GATE + N_BRANCH * D_MODEL

N_GROUPS = 4
EXPERTS_PER_GROUP = 8
N_EXPERTS = N_GROUPS * EXPERTS_PER_GROUP
TOP_K = 2
D_EXPERT = 512
MOE_BLOCK = 256
ROUTER_LANES = 128

LN_EPS = 1e-5
NEG = -1e30

LANES = 128
VMEM_LIMIT = 56 * 1024 * 1024

F32 = jnp.float32
BF16 = jnp.bfloat16


def _params(n_axes):
    return pltpu.CompilerParams(dimension_semantics=("arbitrary",) * n_axes,
                                vmem_limit_bytes=VMEM_LIMIT)


def _dot(a, b):
    return jnp.dot(a, b, preferred_element_type=F32)


def _dot_nt(a, b):
    return lax.dot_general(a, b, (((1,), (1,)), ((), ())), preferred_element_type=F32)


def _split_bf16(a):
    hi = a.astype(BF16)
    lo = (a - hi.astype(F32)).astype(BF16)
    return hi, lo


def _dot_f32ish(a, b):
    ah, al = _split_bf16(a)
    bh, bl = _split_bf16(b)
    return _dot(ah, bh) + (_dot(ah, bl) + _dot(al, bh))


def _rotate(x, c, s1, s2, shift):
    return x * c + pltpu.roll(x, shift, 1) * s1 + pltpu.roll(x, LANES - shift, 1) * s2


def _layer_norm(v, g, b):
    mu = jnp.mean(v, axis=-1, keepdims=True)
    d = v - mu
    var = jnp.mean(d * d, axis=-1, keepdims=True)
    return d * lax.rsqrt(var + LN_EPS) * g + b


def _inproj_kernel(x_ref, w_ref, o_ref, xb_ref):
    @pl.when(pl.program_id(1) == 0)
    def _():
        xb_ref[...] = x_ref[...].astype(BF16)

    o_ref[...] = _dot(xb_ref[...], w_ref[...])


def _inproj(x, w, tm=1024, tn=512):
    t, d = x.shape
    n = w.shape[1]
    return pl.pallas_call(
        _inproj_kernel,
        grid=(t // tm, n // tn),
        in_specs=[pl.BlockSpec((tm, d), lambda i, j: (i, 0)),
                  pl.BlockSpec((d, tn), lambda i, j: (0, j))],
        out_specs=pl.BlockSpec((tm, tn), lambda i, j: (i, j)),
        out_shape=jax.ShapeDtypeStruct((t, n), F32),
        scratch_shapes=[pltpu.VMEM((tm, d), BF16)],
        compiler_params=_params(2),
        name="inproj",
    )(x, w)


def _moba_prep_kernel(q_ref, k_ref, v_ref, c_ref, s1_ref, s2_ref,
                      qt_ref, ko_ref, vt_ref, bias_ref, km_ref):
    i = pl.program_id(0)
    nblk = km_ref.shape[0]

    @pl.when(i == 0)
    def _():
        km_ref[...] = jnp.zeros_like(km_ref)

    c, s1, s2 = c_ref[...], s1_ref[...], s2_ref[...]
    blk = lax.broadcasted_iota(jnp.int32, (nblk, MOBA_BLOCK), 0)
    scale = ATT_HEAD_DIM ** -0.5
    for h in range(ATT_HEADS):
        sl = slice(h * ATT_HEAD_DIM, (h + 1) * ATT_HEAD_DIM)
        qr = _rotate(q_ref[:, sl], c, s1, s2, ROPE_DIM // 2)
        kr = _rotate(k_ref[:, sl], c, s1, s2, ROPE_DIM // 2)
        ko_ref[h, 0] = kr.astype(BF16)
        km_blk = lax.broadcasted_iota(jnp.int32, (nblk, ATT_HEAD_DIM), 0)
        km_ref[:, sl] = jnp.where(km_blk == i, jnp.mean(kr, axis=0, keepdims=True), km_ref[:, sl])
        qrt = qr.T
        qt_ref[h, 0] = (qrt * scale).astype(BF16)
        vt_ref[h, 0] = v_ref[:, sl].T.astype(BF16)
        gate = _dot_f32ish(km_ref[:, sl], qrt)
        gate = jnp.where(blk < i, gate, NEG)
        rank = jnp.zeros((nblk, MOBA_BLOCK), F32)
        for r in range(nblk):
            row = gate[r:r + 1, :]
            beats = jnp.where(row > gate, 1.0, jnp.where((row == gate) & (blk > r), 1.0, 0.0))
            rank = rank + beats
        chosen = (rank < MOBA_TOPK) & (blk < i)
        bias_ref[h, 0] = jnp.where(chosen, 0.0, NEG)


def _moba_prep(proj, rope):
    t = proj.shape[0]
    nblk = t // MOBA_BLOCK
    c, s1, s2 = rope
    tbl = pl.BlockSpec((MOBA_BLOCK, LANES), lambda i: (i, 0))
    hshape = (ATT_HEADS, nblk)
    return pl.pallas_call(
        _moba_prep_kernel,
        grid=(nblk,),
        in_specs=[pl.BlockSpec((MOBA_BLOCK, ATT_WIDTH), lambda i: (i, OFF_QA // ATT_WIDTH)),
                  pl.BlockSpec((MOBA_BLOCK, ATT_WIDTH), lambda i: (i, OFF_KA // ATT_WIDTH)),
                  pl.BlockSpec((MOBA_BLOCK, ATT_WIDTH), lambda i: (i, OFF_VA // ATT_WIDTH)),
                  tbl, tbl, tbl],
        out_specs=[pl.BlockSpec((ATT_HEADS, 1, ATT_HEAD_DIM, MOBA_BLOCK), lambda i: (0, i, 0, 0)),
                   pl.BlockSpec((ATT_HEADS, 1, MOBA_BLOCK, ATT_HEAD_DIM), lambda i: (0, i, 0, 0)),
                   pl.BlockSpec((ATT_HEADS, 1, ATT_HEAD_DIM, MOBA_BLOCK), lambda i: (0, i, 0, 0)),
                   pl.BlockSpec((ATT_HEADS, 1, nblk, MOBA_BLOCK), lambda i: (0, i, 0, 0))],
        out_shape=[jax.ShapeDtypeStruct(hshape + (ATT_HEAD_DIM, MOBA_BLOCK), BF16),
                   jax.ShapeDtypeStruct(hshape + (MOBA_BLOCK, ATT_HEAD_DIM), BF16),
                   jax.ShapeDtypeStruct(hshape + (ATT_HEAD_DIM, MOBA_BLOCK), BF16),
                   jax.ShapeDtypeStruct(hshape + (nblk, MOBA_BLOCK), F32)],
        scratch_shapes=[pltpu.VMEM((nblk, ATT_WIDTH), F32)],
        compiler_params=_params(1),
        name="moba_prep",
    )(proj, proj, proj, c, s1, s2)


def _moba_attn_kernel(qt_ref, k_ref, vt_ref, bias_ref, o_ref):
    i = pl.program_id(1)
    qt = qt_ref[0, 0]
    s = _dot(k_ref[0, i], qt)
    key = lax.broadcasted_iota(jnp.int32, s.shape, 0)
    qry = lax.broadcasted_iota(jnp.int32, s.shape, 1)
    s = jnp.where(key <= qry, s, NEG)
    m = jnp.max(s, axis=0, keepdims=True)
    p = jnp.exp(s - m)
    l = jnp.sum(p, axis=0, keepdims=True)
    acc = _dot(vt_ref[0, i], p.astype(BF16))

    def body(j, carry):
        m, l, acc = carry
        s = _dot(k_ref[0, j], qt) + bias_ref[0, 0, pl.ds(j, 1), :]
        m_new = jnp.maximum(m, jnp.max(s, axis=0, keepdims=True))
        alpha = jnp.exp(m - m_new)
        p = jnp.exp(s - m_new)
        l = alpha * l + jnp.sum(p, axis=0, keepdims=True)
        acc = alpha * acc + _dot(vt_ref[0, j], p.astype(BF16))
        return m_new, l, acc

    m, l, acc = lax.fori_loop(0, i, body, (m, l, acc))
    o_ref[...] = (acc / l).T.astype(o_ref.dtype)


def _moba_attn(qt, k, vt, bias):
    nblk = qt.shape[1]
    t = nblk * MOBA_BLOCK
    return pl.pallas_call(
        _moba_attn_kernel,
        grid=(ATT_HEADS, nblk),
        in_specs=[pl.BlockSpec((1, 1, ATT_HEAD_DIM, MOBA_BLOCK), lambda h, i: (h, i, 0, 0)),
                  pl.BlockSpec((1, nblk, MOBA_BLOCK, ATT_HEAD_DIM), lambda h, i: (h, 0, 0, 0)),
                  pl.BlockSpec((1, nblk, ATT_HEAD_DIM, MOBA_BLOCK), lambda h, i: (h, 0, 0, 0)),
                  pl.BlockSpec((1, 1, nblk, MOBA_BLOCK), lambda h, i: (h, i, 0, 0))],
        out_specs=pl.BlockSpec((MOBA_BLOCK, ATT_HEAD_DIM), lambda h, i: (i, h)),
        out_shape=jax.ShapeDtypeStruct((t, ATT_WIDTH), BF16),
        compiler_params=_params(2),
        name="moba_attn",
    )(qt, k, vt, bias)


def _retention_kernel(q_ref, k_ref, v_ref, g_ref, c_ref, s1_ref, s2_ref,
                      dec_ref, kdec_ref, qdec_ref, cdec_ref, gain_ref, o_ref, state_ref):
    @pl.when(pl.program_id(1) == 0)
    def _():
        state_ref[...] = jnp.zeros_like(state_ref)

    c, s1, s2 = c_ref[...], s1_ref[...], s2_ref[...]
    q = _rotate(q_ref[...], c, s1, s2, RET_QK_DIM // 2)
    k = _rotate(k_ref[...], c, s1, s2, RET_QK_DIM // 2) * (RET_QK_DIM ** -0.5)
    kd = k * kdec_ref[0]
    qd = q * qdec_ref[0]
    lane = lax.broadcasted_iota(jnp.int32, q.shape, 1)
    kb = k.astype(BF16)
    for hh in range(2):
        mine = (lane < RET_QK_DIM) if hh == 0 else (lane >= RET_QK_DIM)
        vsl = slice(hh * RET_V_DIM, (hh + 1) * RET_V_DIM)
        vb = v_ref[:, vsl].astype(BF16)
        qm = jnp.where(mine, q, 0.0).astype(BF16)
        inner = _dot_nt(qm, kb) * dec_ref[hh]
        out = _dot(inner.astype(BF16), vb)
        qdm = jnp.where(mine, qd, 0.0).astype(BF16)
        out = out + _dot(qdm, state_ref[hh].astype(BF16))
        kdt = jnp.where(mine, kd, 0.0).T.astype(BF16)
        state_ref[hh] = state_ref[hh] * cdec_ref[hh] + _dot(kdt, vb)
        mu = jnp.mean(out, axis=-1, keepdims=True)
        d = out - mu
        var = jnp.mean(d * d, axis=-1, keepdims=True)
        rn = d * lax.rsqrt(var + LN_EPS) * gain_ref[:, vsl]
        g = g_ref[:, vsl]
        o_ref[:, vsl] = (g * jax.nn.sigmoid(g) * rn).astype(o_ref.dtype)


def _retention(proj, rope, tables, gain):
    t = proj.shape[0]
    n = t // RET_CHUNK
    c, s1, s2 = rope
    dec, kdec, qdec, cdec = tables
    pair_v = 2 * RET_V_DIM
    tbl = pl.BlockSpec((RET_CHUNK, LANES), lambda hp, ci: (ci, 0))
    return pl.pallas_call(
        _retention_kernel,
        grid=(RET_HEADS // 2, n),
        in_specs=[pl.BlockSpec((RET_CHUNK, LANES), lambda hp, ci: (ci, OFF_QR // LANES + hp)),
                  pl.BlockSpec((RET_CHUNK, LANES), lambda hp, ci: (ci, OFF_KR // LANES + hp)),
                  pl.BlockSpec((RET_CHUNK, pair_v), lambda hp, ci: (ci, OFF_VR // pair_v + hp)),
                  pl.BlockSpec((RET_CHUNK, pair_v), lambda hp, ci: (ci, OFF_GR // pair_v + hp)),
                  tbl, tbl, tbl,
                  pl.BlockSpec((2, RET_CHUNK, RET_CHUNK), lambda hp, ci: (hp, 0, 0)),
                  pl.BlockSpec((1, RET_CHUNK, LANES), lambda hp, ci: (hp, 0, 0)),
                  pl.BlockSpec((1, RET_CHUNK, LANES), lambda hp, ci: (hp, 0, 0)),
                  pl.BlockSpec((2, 1, RET_V_DIM), lambda hp, ci: (hp, 0, 0)),
                  pl.BlockSpec((1, pair_v), lambda hp, ci: (0, hp))],
        out_specs=pl.BlockSpec((RET_CHUNK, pair_v), lambda hp, ci: (ci, hp)),
        out_shape=jax.ShapeDtypeStruct((t, RET_V_WIDTH), BF16),
        scratch_shapes=[pltpu.VMEM((2, LANES, RET_V_DIM), F32)],
        compiler_params=_params(2),
        name="retention",
    )(proj, proj, proj, proj, c, s1, s2, dec, kdec, qdec, cdec, gain)


def _pool_kernel(cur_ref, prev_ref, w_ref, scale_ref, o_ref, ext_ref, *, tb):
    i = pl.program_id(0)
    g = pl.program_id(1)
    x = cur_ref[...]
    prev = jnp.where(i > 0, prev_ref[...], 0.0)
    ext_ref[0:POOL_HALO, :] = prev
    ext_ref[POOL_HALO:, :] = x
    pos = (i * tb + 1 + lax.broadcasted_iota(jnp.int32, (tb, 1), 0)).astype(F32)
    for gi, w in enumerate(POOL_WINDOWS):
        @pl.when(g == gi)
        def _(w=w):
            tot = x
            for sft in range(1, w):
                tot = tot + ext_ref[POOL_HALO - sft:POOL_HALO - sft + tb, :]
            d = tot / jnp.minimum(pos, float(w)) - x
            y = _dot(d.astype(BF16), w_ref[0])
            o_ref[...] = (y * scale_ref[...]).astype(o_ref.dtype)


def _pool(proj, w_pool, scale, tb=512):
    t = proj.shape[0]
    gd = POOL_GROUP_DIM
    return pl.pallas_call(
        functools.partial(_pool_kernel, tb=tb),
        grid=(t // tb, POOL_GROUPS),
        in_specs=[pl.BlockSpec((tb, gd), lambda i, g: (i, OFF_POOL // gd + g)),
                  pl.BlockSpec((POOL_HALO, gd),
                               lambda i, g: (jnp.maximum(i * (tb // POOL_HALO) - 1, 0), OFF_POOL // gd + g)),
                  pl.BlockSpec((1, gd, gd), lambda i, g: (g, 0, 0)),
                  pl.BlockSpec((1, gd), lambda i, g: (0, g))],
        out_specs=pl.BlockSpec((tb, gd), lambda i, g: (i, g)),
        out_shape=jax.ShapeDtypeStruct((t, POOL_WIDTH), BF16),
        scratch_shapes=[pltpu.VMEM((tb + POOL_HALO, gd), F32)],
        compiler_params=_params(2),
        name="pool",
    )(proj, proj, w_pool, scale)


def _merge_kernel(ya_ref, yr_ref, yp_ref, w_ref, ga_ref, gr_ref, gp_ref, o_ref):
    acc = jax.nn.sigmoid(ga_ref[...]) * _dot(ya_ref[...], w_ref[0])
    acc = acc + jax.nn.sigmoid(gr_ref[...]) * _dot(yr_ref[...], w_ref[1])
    acc = acc + jax.nn.sigmoid(gp_ref[...]) * _dot(yp_ref[...], w_ref[2])
    o_ref[...] = acc.astype(o_ref.dtype)


def _merge(ya, yr, yp, w_branch, proj, tm=1024, tn=512):
    t = ya.shape[0]
    d = w_branch.shape[2]
    br = pl.BlockSpec((tm, BRANCH_WIDTH), lambda i, j: (i, 0))

    def gate(nb):
        return pl.BlockSpec((tm, tn), lambda i, j: (i, (OFF_GATE + nb * d) // tn + j))

    return pl.pallas_call(
        _merge_kernel,
        grid=(t // tm, d // tn),
        in_specs=[br, br, br,
                  pl.BlockSpec((N_BRANCH, BRANCH_WIDTH, tn), lambda i, j: (0, 0, j)),
                  gate(0), gate(1), gate(2)],
        out_specs=pl.BlockSpec((tm, tn), lambda i, j: (i, j)),
        out_shape=jax.ShapeDtypeStruct((t, d), BF16),
        compiler_params=_params(2),
        name="merge",
    )(ya, yr, yp, w_branch, proj, proj, proj)


def _outproj_ln_kernel(m_ref, w_ref, x_ref, g_ref, b_ref, o_ref, *, alpha):
    y = _dot(m_ref[...], w_ref[...])
    o_ref[...] = _layer_norm(alpha * x_ref[...] + y, g_ref[...], b_ref[...])


def _outproj_ln(merged, w_out, x, g, b, alpha, tm=512):
    t, d = x.shape
    row = pl.BlockSpec((tm, d), lambda i: (i, 0))
    vec = pl.BlockSpec((1, d), lambda i: (0, 0))
    return pl.pallas_call(
        functools.partial(_outproj_ln_kernel, alpha=alpha),
        grid=(t // tm,),
        in_specs=[row, pl.BlockSpec((d, d), lambda i: (0, 0)), row, vec, vec],
        out_specs=row,
        out_shape=jax.ShapeDtypeStruct((t, d), F32),
        compiler_params=_params(1),
        name="outproj_ln",
    )(merged, w_out, x, g, b)


def _router_kernel(x_ref, wh_ref, wl_ref, b_ref, e_ref, p_ref):
    x = x_ref[...]
    xh, xl = _split_bf16(x)
    z = _dot(xh, wh_ref[...]) + (_dot(xh, wl_ref[...]) + _dot(xl, wh_ref[...])) + b_ref[...]
    lane = lax.broadcasted_iota(jnp.int32, z.shape, 1)
    big = jnp.int32(ROUTER_LANES)

    def masked_softmax(mask):
        zm = jnp.where(mask, z, NEG)
        e = jnp.where(mask, jnp.exp(zm - jnp.max(zm, axis=-1, keepdims=True)), 0.0)
        return e / jnp.sum(e, axis=-1, keepdims=True)

    def top1(p, mask):
        pm = jnp.where(mask, p, -1.0)
        top = jnp.max(pm, axis=-1, keepdims=True)
        idx = jnp.min(jnp.where(pm == top, lane, big), axis=-1, keepdims=True)
        return top, idx

    in_groups = lane < N_GROUPS
    p1 = masked_softmax(in_groups)
    g_top, g_idx = top1(p1, in_groups)
    lo = N_GROUPS + g_idx * EXPERTS_PER_GROUP
    in_group = (lane >= lo) & (lane < lo + EXPERTS_PER_GROUP)
    p2 = masked_softmax(in_group)
    e1, i1 = top1(p2, in_group)
    e2, i2 = top1(p2, in_group & (lane != i1))
    denom = e1 + e2
    w1 = g_top * e1 / denom
    w2 = g_top * e2 / denom
    e_ref[...] = jnp.where(lane == 0, i1 - N_GROUPS, jnp.where(lane == 1, i2 - N_GROUPS, 0))
    p_ref[...] = jnp.where(lane == 0, w1, jnp.where(lane == 1, w2, 0.0))


def _router(x, w_hi, w_lo, bias, tm=512):
    t, d = x.shape
    row = pl.BlockSpec((tm, d), lambda i: (i, 0))
    wsp = pl.BlockSpec((d, ROUTER_LANES), lambda i: (0, 0))
    out = pl.BlockSpec((tm, ROUTER_LANES), lambda i: (i, 0))
    return pl.pallas_call(
        _router_kernel,
        grid=(t // tm,),
        in_specs=[row, wsp, wsp, pl.BlockSpec((1, ROUTER_LANES), lambda i: (0, 0))],
        out_specs=[out, out],
        out_shape=[jax.ShapeDtypeStruct((t, ROUTER_LANES), jnp.int32),
                   jax.ShapeDtypeStruct((t, ROUTER_LANES), F32)],
        compiler_params=_params(1),
        name="router",
    )(x, w_hi, w_lo, bias)


def _experts_kernel(be_ref, tok_ref, dst_ref, nused_ref,
                    x_hbm, wg_ref, wu_ref, wd_ref, y_hbm, xbuf, ybuf, sem_in, sem_out):
    del be_ref
    b = pl.program_id(0)

    def row_in(r, tok):
        return pltpu.make_async_copy(x_hbm.at[pl.ds(tok, 1), :], xbuf.at[pl.ds(r, 1), :], sem_in)

    def row_out(r, dst):
        return pltpu.make_async_copy(ybuf.at[pl.ds(r, 1), :], y_hbm.at[pl.ds(dst, 1), :], sem_out)

    @pl.when(b < nused_ref[0])
    def _():
        base = b * MOE_BLOCK

        def start_in(r, carry):
            row_in(r, tok_ref[base + r]).start()
            return carry

        def wait_in(r, carry):
            row_in(r, 0).wait()
            return carry

        lax.fori_loop(0, MOE_BLOCK, start_in, 0)
        lax.fori_loop(0, MOE_BLOCK, wait_in, 0)
        x = xbuf[...].astype(BF16)
        gate = _dot(x, wg_ref[0])
        up = _dot(x, wu_ref[0])
        hid = (gate * jax.nn.sigmoid(gate) * up).astype(BF16)
        ybuf[...] = _dot(hid, wd_ref[0])

        def start_out(r, carry):
            dst = dst_ref[base + r]

            @pl.when(dst >= 0)
            def _():
                row_out(r, dst).start()

            return carry

        def wait_out(r, carry):
            @pl.when(dst_ref[base + r] >= 0)
            def _():
                row_out(r, 0).wait()

            return carry

        lax.fori_loop(0, MOE_BLOCK, start_out, 0)
        lax.fori_loop(0, MOE_BLOCK, wait_out, 0)


def _experts(x, wg, wu, wd, block_expert, slot_tok, slot_dst, n_used):
    t, d = x.shape
    n_blocks = block_expert.shape[0]

    def wspec(shape):
        return pl.BlockSpec((1,) + shape, lambda b, be, tok, dst, nu: (be[b], 0, 0))

    grid_spec = pltpu.PrefetchScalarGridSpec(
        num_scalar_prefetch=4,
        grid=(n_blocks,),
        in_specs=[pl.BlockSpec(memory_space=pl.ANY),
                  wspec((d, D_EXPERT)), wspec((d, D_EXPERT)), wspec((D_EXPERT, d))],
        out_specs=pl.BlockSpec(memory_space=pl.ANY),
        scratch_shapes=[pltpu.VMEM((MOE_BLOCK, d), F32),
                        pltpu.VMEM((MOE_BLOCK, d), F32),
                        pltpu.SemaphoreType.DMA(()),
                        pltpu.SemaphoreType.DMA(())],
    )
    return pl.pallas_call(
        _experts_kernel,
        grid_spec=grid_spec,
        out_shape=jax.ShapeDtypeStruct((t * TOP_K, d), F32),
        compiler_params=_params(1),
        name="experts",
    )(block_expert, slot_tok, slot_dst, n_used, x, wg, wu, wd)


def _combine_ln_kernel(x_ref, y0_ref, y1_ref, p_ref, g_ref, b_ref, o_ref, *, alpha):
    p = p_ref[...]
    y = p[:, 0:1] * y0_ref[...] + p[:, 1:2] * y1_ref[...]
    o_ref[...] = _layer_norm(alpha * x_ref[...] + y, g_ref[...], b_ref[...])


def _combine_ln(x, ycomb, probs, g, b, alpha, tm=512):
    t, d = x.shape
    y2 = ycomb.reshape(t, TOP_K * d)
    row = pl.BlockSpec((tm, d), lambda i: (i, 0))
    vec = pl.BlockSpec((1, d), lambda i: (0, 0))
    return pl.pallas_call(
        functools.partial(_combine_ln_kernel, alpha=alpha),
        grid=(t // tm,),
        in_specs=[row, row, pl.BlockSpec((tm, d), lambda i: (i, 1)),
                  pl.BlockSpec((tm, ROUTER_LANES), lambda i: (i, 0)), vec, vec],
        out_specs=row,
        out_shape=jax.ShapeDtypeStruct((t, d), F32),
        compiler_params=_params(1),
        name="combine_ln",
    )(x, y2, y2, probs, g, b)


def _rope_tables(t, rot_dim, theta, head_dim):
    half = rot_dim // 2
    inv_freq = 1.0 / (theta ** (jnp.arange(0, rot_dim, 2, dtype=F32) / rot_dim))
    ang = jnp.arange(t, dtype=F32)[:, None] * inv_freq[None, :]
    cos, sin = jnp.cos(ang), jnp.sin(ang)
    rest = head_dim - rot_dim
    ones, zeros = jnp.ones((t, rest), F32), jnp.zeros((t, rest), F32)
    zh = jnp.zeros((t, half), F32)
    c = jnp.concatenate([cos, cos, ones], axis=1)
    s1 = jnp.concatenate([zh, sin, zeros], axis=1)
    s2 = jnp.concatenate([-sin, zh, zeros], axis=1)
    rep = LANES // head_dim
    return tuple(jnp.tile(a, (1, rep)) for a in (c, s1, s2))


def _retention_tables():
    h, c = RET_HEADS, RET_CHUNK
    log_gamma = jnp.log1p(-jnp.exp2(-5.0 - jnp.arange(h, dtype=F32)))
    pos = jnp.arange(c, dtype=F32)
    diff = pos[:, None] - pos[None, :]
    decay = jnp.where(diff >= 0, jnp.exp(log_gamma[:, None, None] * jnp.maximum(diff, 0.0)), 0.0)
    k_decay = jnp.exp(log_gamma[:, None] * (c - 1.0 - pos)[None, :])
    q_decay = jnp.exp(log_gamma[:, None] * (pos + 1.0)[None, :])
    chunk_decay = jnp.exp(log_gamma * c)

    def pair_lanes(a):
        a = a.reshape(h // 2, 2, c).transpose(0, 2, 1)
        return jnp.repeat(a, RET_QK_DIM, axis=2)

    cdec = jnp.broadcast_to(chunk_decay[:, None, None], (h, 1, RET_V_DIM))
    return decay, pair_lanes(k_decay), pair_lanes(q_decay), cdec


def _dispatch_plan(expert, t):
    a = t * TOP_K
    flat_e = expert.reshape(a)
    order = jnp.argsort(flat_e)
    e_sorted = flat_e[order]
    counts = jnp.bincount(flat_e, length=N_EXPERTS)
    padded = (counts + MOE_BLOCK - 1) // MOE_BLOCK * MOE_BLOCK
    pad_end = jnp.cumsum(padded)
    pad_start = pad_end - padded
    start = jnp.cumsum(counts) - counts
    slot = pad_start[e_sorted] + jnp.arange(a) - start[e_sorted]
    n_blocks = -(-a // MOE_BLOCK) + N_EXPERTS
    n_slots = n_blocks * MOE_BLOCK
    slot_tok = jnp.zeros((n_slots,), jnp.int32).at[slot].set((order // TOP_K).astype(jnp.int32))
    slot_dst = jnp.full((n_slots,), -1, jnp.int32).at[slot].set(order.astype(jnp.int32))
    block_expert = jnp.minimum(
        jnp.searchsorted(pad_end, jnp.arange(n_blocks) * MOE_BLOCK, side='right'), N_EXPERTS - 1)
    n_used = (pad_end[-1] // MOE_BLOCK).astype(jnp.int32).reshape(1)
    return block_expert.astype(jnp.int32), slot_tok, slot_dst, n_used


def _token_mixer_ln(h, w_in, ret_gain, w_pool, pool_scale, w_branch, w_out, ln_g, ln_b,
                    rope_a, rope_r, ret_tables, alpha):
    proj = _inproj(h, w_in.astype(BF16))
    qt, kk, vt, bias = _moba_prep(proj, rope_a)
    y_a = _moba_attn(qt, kk, vt, bias)
    y_r = _retention(proj, rope_r, ret_tables, ret_gain[None, :])
    y_p = _pool(proj, w_pool.astype(BF16), pool_scale[None, :])
    merged = _merge(y_a, y_r, y_p, w_branch.astype(BF16), proj)
    return _outproj_ln(merged, w_out.astype(BF16), h, ln_g[None, :], ln_b[None, :], alpha)


def _moe_ln(h, wr_hi, wr_lo, b_router, w_gate, w_up, w_down, ln_g, ln_b, alpha):
    e_pad, p_pad = _router(h, wr_hi, wr_lo, b_router)
    plan = _dispatch_plan(e_pad[:, :TOP_K], h.shape[0])
    ycomb = _experts(h, w_gate.astype(BF16), w_up.astype(BF16), w_down.astype(BF16), *plan)
    return _combine_ln(h, ycomb, p_pad, ln_g[None, :], ln_b[None, :], alpha)


def kernel(x, w_in, ret_gain, w_pool, pool_scale, w_branch, w_out, ln1_g, ln1_b, w_r1, b_r1, w_r2, b_r2, w_e_gate, w_e_up, w_e_down, ln2_g, ln2_b):
    bsz, seq, d = x.shape
    depth = w_in.shape[0]
    assert bsz == 1 and d == D_MODEL and seq % 1024 == 0
    t = seq
    alpha = float((2 * depth) ** 0.25)

    rope_a = _rope_tables(t, ROPE_DIM, ROPE_THETA, ATT_HEAD_DIM)
    rope_r = _rope_tables(t, RET_QK_DIM, RET_ROPE_THETA, RET_QK_DIM)
    ret_tables = _retention_tables()

    w_router = jnp.concatenate([w_r1, w_r2], axis=2)
    w_router = jnp.pad(w_router, ((0, 0), (0, 0), (0, ROUTER_LANES - w_router.shape[2])))
    wr_hi = w_router.astype(BF16)
    wr_lo = (w_router - wr_hi.astype(F32)).astype(BF16)
    b_router = jnp.pad(jnp.concatenate([b_r1, b_r2], axis=1),
                       ((0, 0), (0, ROUTER_LANES - N_GROUPS - N_EXPERTS)))[:, None, :]

    h = x.reshape(t, d)
    for l in range(depth):
        h = _token_mixer_ln(h, w_in[l], ret_gain[l], w_pool[l], pool_scale[l], w_branch[l], w_out[l],
                            ln1_g[l], ln1_b[l], rope_a, rope_r, ret_tables, alpha)
        h = _moe_ln(h, wr_hi[l], wr_lo[l], b_router[l], w_e_gate[l], w_e_up[l], w_e_down[l],
                    ln2_g[l], ln2_b[l], alpha)
    return h.reshape(bsz, seq, d)
```

```python
import functools

import numpy as np
import jax
import jax.numpy as jnp
from jax import lax
from jax.experimental import pallas as pl
from jax.experimental.pallas import tpu as pltpu

D_MODEL = 2048

ATT_HEADS = 8
ATT_HEAD_DIM = 128
ATT_WIDTH = ATT_HEADS * ATT_HEAD_DIM
ROPE_DIM = ATT_HEAD_DIM // 4
ROPE_THETA = 500000.0
MOBA_BLOCK = 256
MOBA_TOPK = 3

RET_HEADS = 8
RET_QK_DIM = 64
RET_V_DIM = 128
RET_QK_WIDTH = RET_HEADS * RET_QK_DIM
RET_V_WIDTH = RET_HEADS * RET_V_DIM
RET_CHUNK = 256
RET_ROPE_THETA = 10000.0

POOL_WINDOWS = (2, 4, 8, 16)
POOL_GROUPS = 4
POOL_GROUP_DIM = 256
POOL_WIDTH = POOL_GROUPS * POOL_GROUP_DIM
POOL_HALO = 16

N_BRANCH = 3
BRANCH_WIDTH = 1024

OFF_QA = 0
OFF_KA = OFF_QA + ATT_WIDTH
OFF_VA = OFF_KA + ATT_WIDTH
OFF_QR = OFF_VA + ATT_WIDTH
OFF_KR = OFF_QR + RET_QK_WIDTH
OFF_VR = OFF_KR + RET_QK_WIDTH
OFF_GR = OFF_VR + RET_V_WIDTH
OFF_POOL = OFF_GR + RET_V_WIDTH
OFF_GATE = OFF_POOL + POOL_WIDTH
N_IN = OFF_GATE + N_BRANCH * D_MODEL

N_GROUPS = 4
EXPERTS_PER_GROUP = 8
N_EXPERTS = N_GROUPS * EXPERTS_PER_GROUP
TOP_K = 2
D_EXPERT = 512
MOE_BLOCK = 256
ROUTER_LANES = 128

LN_EPS = 1e-5
NEG = -1e30
LOG2_E = 1.4426950408889634

LANES = 128
VMEM_LIMIT = 56 * 1024 * 1024

F32 = jnp.float32
BF16 = jnp.bfloat16


def _params(n_axes):
    return pltpu.CompilerParams(dimension_semantics=("arbitrary",) * n_axes,
                                vmem_limit_bytes=VMEM_LIMIT)


def _dot(a, b):
    return jnp.dot(a, b, preferred_element_type=F32)


def _dot_nt(a, b):
    return lax.dot_general(a, b, (((1,), (1,)), ((), ())), preferred_element_type=F32)


def _split_bf16(a):
    hi = a.astype(BF16)
    lo = (a - hi.astype(F32)).astype(BF16)
    return hi, lo


def _dot_f32ish(a, b):
    ah, al = _split_bf16(a)
    bh, bl = _split_bf16(b)
    return _dot(ah, bh) + (_dot(ah, bl) + _dot(al, bh))


def _rotate(x, c, s1, s2, shift):
    return x * c + pltpu.roll(x, shift, 1) * s1 + pltpu.roll(x, LANES - shift, 1) * s2


def _layer_norm(v, g, b):
    mu = jnp.mean(v, axis=-1, keepdims=True)
    d = v - mu
    var = jnp.mean(d * d, axis=-1, keepdims=True)
    return d * lax.rsqrt(var + LN_EPS) * g + b


def _inproj_kernel(x_ref, w_ref, o_ref, wb_ref):
    @pl.when(pl.program_id(1) == 0)
    def _():
        wb_ref[...] = w_ref[...].astype(BF16)

    o_ref[...] = _dot(x_ref[...], wb_ref[...])


def _inproj(xb, w, layer, tm=1024, tn=1024):
    t, d = xb.shape
    n = w.shape[2]
    return pl.pallas_call(
        _inproj_kernel,
        grid=(n // tn, t // tm),
        in_specs=[pl.BlockSpec((tm, d), lambda j, i: (i, 0)),
                  pl.BlockSpec((None, d, tn), lambda j, i: (layer, 0, j))],
        out_specs=pl.BlockSpec((tm, tn), lambda j, i: (i, j)),
        out_shape=jax.ShapeDtypeStruct((t, n), F32),
        scratch_shapes=[pltpu.VMEM((d, tn), BF16)],
        compiler_params=_params(2),
        name="inproj",
    )(xb, w)


def _moba_prep_kernel(q_ref, k_ref, v_ref, c_ref, s1_ref, s2_ref,
                      qt_ref, ko_ref, vt_ref, bias_ref, km_ref):
    i = pl.program_id(0)
    nblk = km_ref.shape[0]

    @pl.when(i == 0)
    def _():
        km_ref[...] = jnp.zeros_like(km_ref)

    c, s1, s2 = c_ref[...], s1_ref[...], s2_ref[...]
    blk = lax.broadcasted_iota(jnp.int32, (nblk, MOBA_BLOCK), 0)
    scale = ATT_HEAD_DIM ** -0.5 * LOG2_E
    for h in range(ATT_HEADS):
        sl = slice(h * ATT_HEAD_DIM, (h + 1) * ATT_HEAD_DIM)
        qr = _rotate(q_ref[:, sl], c, s1, s2, ROPE_DIM // 2)
        kr = _rotate(k_ref[:, sl], c, s1, s2, ROPE_DIM // 2)
        ko_ref[h, 0] = kr.astype(BF16)
        km_blk = lax.broadcasted_iota(jnp.int32, (nblk, ATT_HEAD_DIM), 0)
        km_ref[:, sl] = jnp.where(km_blk == i, jnp.mean(kr, axis=0, keepdims=True), km_ref[:, sl])
        qrt = qr.T
        qt_ref[h, 0] = (qrt * scale).astype(BF16)
        vt_ref[h, 0] = v_ref[:, sl].T.astype(BF16)
        gate = _dot_f32ish(km_ref[:, sl], qrt)
        gate = jnp.where(blk < i, gate, NEG)
        rank = jnp.zeros((nblk, MOBA_BLOCK), F32)
        for r in range(nblk):
            row = gate[r:r + 1, :]
            beats = jnp.where(row > gate, 1.0, jnp.where((row == gate) & (blk > r), 1.0, 0.0))
            rank = rank + beats
        chosen = (rank < MOBA_TOPK) & (blk < i)
        bias_ref[h, 0] = jnp.where(chosen, 0.0, NEG)


def _moba_prep(proj, rope):
    t = proj.shape[0]
    nblk = t // MOBA_BLOCK
    c, s1, s2 = rope
    tbl = pl.BlockSpec((MOBA_BLOCK, LANES), lambda i: (i, 0))
    hshape = (ATT_HEADS, nblk)
    return pl.pallas_call(
        _moba_prep_kernel,
        grid=(nblk,),
        in_specs=[pl.BlockSpec((MOBA_BLOCK, ATT_WIDTH), lambda i: (i, OFF_QA // ATT_WIDTH)),
                  pl.BlockSpec((MOBA_BLOCK, ATT_WIDTH), lambda i: (i, OFF_KA // ATT_WIDTH)),
                  pl.BlockSpec((MOBA_BLOCK, ATT_WIDTH), lambda i: (i, OFF_VA // ATT_WIDTH)),
                  tbl, tbl, tbl],
        out_specs=[pl.BlockSpec((ATT_HEADS, 1, ATT_HEAD_DIM, MOBA_BLOCK), lambda i: (0, i, 0, 0)),
                   pl.BlockSpec((ATT_HEADS, 1, MOBA_BLOCK, ATT_HEAD_DIM), lambda i: (0, i, 0, 0)),
                   pl.BlockSpec((ATT_HEADS, 1, ATT_HEAD_DIM, MOBA_BLOCK), lambda i: (0, i, 0, 0)),
                   pl.BlockSpec((ATT_HEADS, 1, nblk, MOBA_BLOCK), lambda i: (0, i, 0, 0))],
        out_shape=[jax.ShapeDtypeStruct(hshape + (ATT_HEAD_DIM, MOBA_BLOCK), BF16),
                   jax.ShapeDtypeStruct(hshape + (MOBA_BLOCK, ATT_HEAD_DIM), BF16),
                   jax.ShapeDtypeStruct(hshape + (ATT_HEAD_DIM, MOBA_BLOCK), BF16),
                   jax.ShapeDtypeStruct(hshape + (nblk, MOBA_BLOCK), F32)],
        scratch_shapes=[pltpu.VMEM((nblk, ATT_WIDTH), F32)],
        compiler_params=_params(1),
        name="moba_prep",
    )(proj, proj, proj, c, s1, s2)


def _moba_attn_kernel(qt_ref, k_ref, vt_ref, bias_ref, o_ref, s_ref, p_ref, a_ref, acc_ref):
    i = pl.program_id(1)
    heads = qt_ref.shape[0]
    nblk = k_ref.shape[1]
    shape = (MOBA_BLOCK, MOBA_BLOCK)
    causal = lax.broadcasted_iota(jnp.int32, shape, 0) <= lax.broadcasted_iota(jnp.int32, shape, 1)

    ml = []
    for h in range(heads):
        s = jnp.where(causal, _dot(k_ref[h, i], qt_ref[h, 0]), NEG)
        m = jnp.max(s, axis=0, keepdims=True)
        p = jnp.exp2(s - m)
        ml += [m, jnp.sum(p, axis=0, keepdims=True)]
        acc_ref[h] = _dot(vt_ref[h, i], p.astype(BF16))
        p_ref[1, h] = jnp.zeros(shape, BF16)
        a_ref[1, h] = jnp.ones((1, MOBA_BLOCK), F32)

    def scores(j, slot):
        jc = jnp.minimum(j, nblk - 1)
        for h in range(heads):
            s_ref[slot, h] = _dot(k_ref[h, jc], qt_ref[h, 0]) + bias_ref[h, 0, pl.ds(jc, 1), :]

    def pv(j, slot):
        jc = jnp.maximum(j, 0)
        for h in range(heads):
            acc_ref[h] = a_ref[slot, h] * acc_ref[h] + _dot(vt_ref[h, jc], p_ref[slot, h])

    def softmax(slot, ml):
        out = []
        for h in range(heads):
            m, l = ml[2 * h], ml[2 * h + 1]
            s = s_ref[slot, h]
            m_new = jnp.maximum(m, jnp.max(s, axis=0, keepdims=True))
            alpha = jnp.exp2(m - m_new)
            p = jnp.exp2(s - m_new)
            a_ref[slot, h] = alpha
            p_ref[slot, h] = p.astype(BF16)
            out += [m_new, alpha * l + jnp.sum(p, axis=0, keepdims=True)]
        return out

    scores(0, 0)

    def body(jj, ml):
        j = 2 * jj
        scores(j + 1, 1)
        pv(j - 1, 1)
        ml = softmax(0, ml)
        scores(j + 2, 0)
        pv(j, 0)
        return softmax(1, ml)

    ml = lax.fori_loop(0, (i + 1) // 2, body, ml)
    pv(i - 1, 1)
    for h in range(heads):
        o_ref[:, h * ATT_HEAD_DIM:(h + 1) * ATT_HEAD_DIM] = (acc_ref[h] / ml[2 * h + 1]).T.astype(o_ref.dtype)


def _moba_attn(qt, k, vt, bias, heads=2):
    nblk = qt.shape[1]
    t = nblk * MOBA_BLOCK
    blk2 = (MOBA_BLOCK, MOBA_BLOCK)
    return pl.pallas_call(
        _moba_attn_kernel,
        grid=(ATT_HEADS // heads, nblk),
        in_specs=[pl.BlockSpec((heads, 1, ATT_HEAD_DIM, MOBA_BLOCK), lambda h, i: (h, i, 0, 0)),
                  pl.BlockSpec((heads, nblk, MOBA_BLOCK, ATT_HEAD_DIM), lambda h, i: (h, 0, 0, 0)),
                  pl.BlockSpec((heads, nblk, ATT_HEAD_DIM, MOBA_BLOCK), lambda h, i: (h, 0, 0, 0)),
                  pl.BlockSpec((heads, 1, nblk, MOBA_BLOCK), lambda h, i: (h, i, 0, 0))],
        out_specs=pl.BlockSpec((MOBA_BLOCK, heads * ATT_HEAD_DIM), lambda h, i: (i, h)),
        out_shape=jax.ShapeDtypeStruct((t, ATT_WIDTH), BF16),
        scratch_shapes=[pltpu.VMEM((2, heads) + blk2, F32),
                        pltpu.VMEM((2, heads) + blk2, BF16),
                        pltpu.VMEM((2, heads, 1, MOBA_BLOCK), F32),
                        pltpu.VMEM((heads, ATT_HEAD_DIM, MOBA_BLOCK), F32)],
        compiler_params=_params(2),
        name="moba_attn",
    )(qt, k, vt, bias)


def _retention_kernel(q_ref, k_ref, v_ref, g_ref, c_ref, s1_ref, s2_ref,
                      dec_ref, kdec_ref, qdec_ref, cdec_ref, gain_ref, o_ref, state_ref):
    @pl.when(pl.program_id(1) == 0)
    def _():
        state_ref[...] = jnp.zeros_like(state_ref)

    c, s1, s2 = c_ref[...], s1_ref[...], s2_ref[...]
    q = _rotate(q_ref[...], c, s1, s2, RET_QK_DIM // 2)
    k = _rotate(k_ref[...], c, s1, s2, RET_QK_DIM // 2) * (RET_QK_DIM ** -0.5)
    kd = k * kdec_ref[0]
    qd = q * qdec_ref[0]
    lane = lax.broadcasted_iota(jnp.int32, q.shape, 1)
    kb = k.astype(BF16)
    for hh in range(2):
        mine = (lane < RET_QK_DIM) if hh == 0 else (lane >= RET_QK_DIM)
        vsl = slice(hh * RET_V_DIM, (hh + 1) * RET_V_DIM)
        vb = v_ref[:, vsl].astype(BF16)
        qm = jnp.where(mine, q, 0.0).astype(BF16)
        inner = _dot_nt(qm, kb) * dec_ref[hh]
        out = _dot(inner.astype(BF16), vb)
        qdm = jnp.where(mine, qd, 0.0).astype(BF16)
        out = out + _dot(qdm, state_ref[hh].astype(BF16))
        kdt = jnp.where(mine, kd, 0.0).T.astype(BF16)
        state_ref[hh] = state_ref[hh] * cdec_ref[hh] + _dot(kdt, vb)
        mu = jnp.mean(out, axis=-1, keepdims=True)
        d = out - mu
        var = jnp.mean(d * d, axis=-1, keepdims=True)
        rn = d * lax.rsqrt(var + LN_EPS) * gain_ref[:, vsl]
        g = g_ref[:, vsl]
        o_ref[:, vsl] = (g * jax.nn.sigmoid(g) * rn).astype(o_ref.dtype)


def _retention(proj, rope, tables, gain):
    t = proj.shape[0]
    n = t // RET_CHUNK
    c, s1, s2 = rope
    dec, kdec, qdec, cdec = tables
    pair_v = 2 * RET_V_DIM
    tbl = pl.BlockSpec((RET_CHUNK, LANES), lambda hp, ci: (ci, 0))
    return pl.pallas_call(
        _retention_kernel,
        grid=(RET_HEADS // 2, n),
        in_specs=[pl.BlockSpec((RET_CHUNK, LANES), lambda hp, ci: (ci, OFF_QR // LANES + hp)),
                  pl.BlockSpec((RET_CHUNK, LANES), lambda hp, ci: (ci, OFF_KR // LANES + hp)),
                  pl.BlockSpec((RET_CHUNK, pair_v), lambda hp, ci: (ci, OFF_VR // pair_v + hp)),
                  pl.BlockSpec((RET_CHUNK, pair_v), lambda hp, ci: (ci, OFF_GR // pair_v + hp)),
                  tbl, tbl, tbl,
                  pl.BlockSpec((2, RET_CHUNK, RET_CHUNK), lambda hp, ci: (hp, 0, 0)),
                  pl.BlockSpec((1, RET_CHUNK, LANES), lambda hp, ci: (hp, 0, 0)),
                  pl.BlockSpec((1, RET_CHUNK, LANES), lambda hp, ci: (hp, 0, 0)),
                  pl.BlockSpec((2, 1, RET_V_DIM), lambda hp, ci: (hp, 0, 0)),
                  pl.BlockSpec((1, pair_v), lambda hp, ci: (0, hp))],
        out_specs=pl.BlockSpec((RET_CHUNK, pair_v), lambda hp, ci: (ci, hp)),
        out_shape=jax.ShapeDtypeStruct((t, RET_V_WIDTH), BF16),
        scratch_shapes=[pltpu.VMEM((2, LANES, RET_V_DIM), F32)],
        compiler_params=_params(2),
        name="retention",
    )(proj, proj, proj, proj, c, s1, s2, dec, kdec, qdec, cdec, gain)


def _pool_kernel(cur_ref, prev_ref, w_ref, scale_ref, o_ref, ext_ref, *, tb):
    i = pl.program_id(0)
    g = pl.program_id(1)
    x = cur_ref[...]
    prev = jnp.where(i > 0, prev_ref[...], 0.0)
    ext_ref[0:POOL_HALO, :] = prev
    ext_ref[POOL_HALO:, :] = x
    pos = (i * tb + 1 + lax.broadcasted_iota(jnp.int32, (tb, 1), 0)).astype(F32)
    for gi, w in enumerate(POOL_WINDOWS):
        @pl.when(g == gi)
        def _(w=w):
            tot = x
            for sft in range(1, w):
                tot = tot + ext_ref[POOL_HALO - sft:POOL_HALO - sft + tb, :]
            d = tot / jnp.minimum(pos, float(w)) - x
            y = _dot(d.astype(BF16), w_ref[0])
            o_ref[...] = (y * scale_ref[...]).astype(o_ref.dtype)


def _pool(proj, w_pool, layer, scale, tb=512):
    t = proj.shape[0]
    gd = POOL_GROUP_DIM
    return pl.pallas_call(
        functools.partial(_pool_kernel, tb=tb),
        grid=(t // tb, POOL_GROUPS),
        in_specs=[pl.BlockSpec((tb, gd), lambda i, g: (i, OFF_POOL // gd + g)),
                  pl.BlockSpec((POOL_HALO, gd),
                               lambda i, g: (jnp.maximum(i * (tb // POOL_HALO) - 1, 0), OFF_POOL // gd + g)),
                  pl.BlockSpec((None, 1, gd, gd), lambda i, g: (layer, g, 0, 0)),
                  pl.BlockSpec((1, gd), lambda i, g: (0, g))],
        out_specs=pl.BlockSpec((tb, gd), lambda i, g: (i, g)),
        out_shape=jax.ShapeDtypeStruct((t, POOL_WIDTH), BF16),
        scratch_shapes=[pltpu.VMEM((tb + POOL_HALO, gd), F32)],
        compiler_params=_params(2),
        name="pool",
    )(proj, proj, w_pool, scale)


def _merge_kernel(ya_ref, yr_ref, yp_ref, w_ref, ga_ref, gr_ref, gp_ref, o_ref):
    acc = jax.nn.sigmoid(ga_ref[...]) * _dot(ya_ref[...], w_ref[0])
    acc = acc + jax.nn.sigmoid(gr_ref[...]) * _dot(yr_ref[...], w_ref[1])
    acc = acc + jax.nn.sigmoid(gp_ref[...]) * _dot(yp_ref[...], w_ref[2])
    o_ref[...] = acc.astype(o_ref.dtype)


def _merge(ya, yr, yp, w_branch, layer, proj, tm=1024, tn=512):
    t = ya.shape[0]
    d = w_branch.shape[3]
    br = pl.BlockSpec((tm, BRANCH_WIDTH), lambda i, j: (i, 0))

    def gate(nb):
        return pl.BlockSpec((tm, tn), lambda i, j: (i, (OFF_GATE + nb * d) // tn + j))

    return pl.pallas_call(
        _merge_kernel,
        grid=(t // tm, d // tn),
        in_specs=[br, br, br,
                  pl.BlockSpec((None, N_BRANCH, BRANCH_WIDTH, tn), lambda i, j: (layer, 0, 0, j)),
                  gate(0), gate(1), gate(2)],
        out_specs=pl.BlockSpec((tm, tn), lambda i, j: (i, j)),
        out_shape=jax.ShapeDtypeStruct((t, d), BF16),
        compiler_params=_params(2),
        name="merge",
    )(ya, yr, yp, w_branch, proj, proj, proj)


def _outproj_ln_kernel(m_ref, w_ref, x_ref, g_ref, b_ref, o_ref, *, alpha):
    y = _dot(m_ref[...], w_ref[...])
    o_ref[...] = _layer_norm(alpha * x_ref[...] + y, g_ref[...], b_ref[...])


def _outproj_ln(merged, w_out, layer, x, g, b, alpha, tm=512):
    t, d = x.shape
    row = pl.BlockSpec((tm, d), lambda i: (i, 0))
    vec = pl.BlockSpec((1, d), lambda i: (0, 0))
    return pl.pallas_call(
        functools.partial(_outproj_ln_kernel, alpha=alpha),
        grid=(t // tm,),
        in_specs=[row, pl.BlockSpec((None, d, d), lambda i: (layer, 0, 0)), row, vec, vec],
        out_specs=row,
        out_shape=jax.ShapeDtypeStruct((t, d), F32),
        compiler_params=_params(1),
        name="outproj_ln",
    )(merged, w_out, x, g, b)


def _router_kernel(x_ref, wh_ref, wl_ref, b_ref, e_ref, p_ref):
    x = x_ref[...]
    xh, xl = _split_bf16(x)
    z = _dot(xh, wh_ref[...]) + (_dot(xh, wl_ref[...]) + _dot(xl, wh_ref[...])) + b_ref[...]
    lane = lax.broadcasted_iota(jnp.int32, z.shape, 1)
    big = jnp.int32(ROUTER_LANES)

    def masked_softmax(mask):
        zm = jnp.where(mask, z, NEG)
        e = jnp.where(mask, jnp.exp(zm - jnp.max(zm, axis=-1, keepdims=True)), 0.0)
        return e / jnp.sum(e, axis=-1, keepdims=True)

    def top1(p, mask):
        pm = jnp.where(mask, p, -1.0)
        top = jnp.max(pm, axis=-1, keepdims=True)
        idx = jnp.min(jnp.where(pm == top, lane, big), axis=-1, keepdims=True)
        return top, idx

    in_groups = lane < N_GROUPS
    p1 = masked_softmax(in_groups)
    g_top, g_idx = top1(p1, in_groups)
    lo = N_GROUPS + g_idx * EXPERTS_PER_GROUP
    in_group = (lane >= lo) & (lane < lo + EXPERTS_PER_GROUP)
    p2 = masked_softmax(in_group)
    e1, i1 = top1(p2, in_group)
    e2, i2 = top1(p2, in_group & (lane != i1))
    denom = e1 + e2
    w1 = g_top * e1 / denom
    w2 = g_top * e2 / denom
    e_ref[...] = jnp.where(lane == 0, i1 - N_GROUPS, jnp.where(lane == 1, i2 - N_GROUPS, 0))
    p_ref[...] = jnp.where(lane == 0, w1, jnp.where(lane == 1, w2, 0.0))


def _router(x, w_hi, w_lo, bias, tm=512):
    t, d = x.shape
    row = pl.BlockSpec((tm, d), lambda i: (i, 0))
    wsp = pl.BlockSpec((d, ROUTER_LANES), lambda i: (0, 0))
    out = pl.BlockSpec((tm, ROUTER_LANES), lambda i: (i, 0))
    return pl.pallas_call(
        _router_kernel,
        grid=(t // tm,),
        in_specs=[row, wsp, wsp, pl.BlockSpec((1, ROUTER_LANES), lambda i: (0, 0))],
        out_specs=[out, out],
        out_shape=[jax.ShapeDtypeStruct((t, ROUTER_LANES), jnp.int32),
                   jax.ShapeDtypeStruct((t, ROUTER_LANES), F32)],
        compiler_params=_params(1),
        name="router",
    )(x, w_hi, w_lo, bias)


DMA_UNROLL = 8


def _for_rows(n, fn):
    def group(c, carry):
        for u in range(DMA_UNROLL):
            fn(c * DMA_UNROLL + u)
        return carry

    def single(r, carry):
        fn(r)
        return carry

    full = lax.shift_right_logical(n, DMA_UNROLL.bit_length() - 1)
    lax.fori_loop(0, full, group, 0)
    lax.fori_loop(full * DMA_UNROLL, n, single, 0)


def _experts_kernel(be_ref, base_ref, nvalid_ref, order_ref, nused_ref,
                    x_hbm, wg_ref, wu_ref, wd_ref, y_hbm,
                    xbuf, ybuf, wgb, wub, wdb, sem_in, sem_out, *, t):
    b = pl.program_id(0)
    n_used = nused_ref[0]
    slot = b & 1

    def row_in(bb, sl, r, src):
        del bb
        return pltpu.make_async_copy(x_hbm.at[pl.ds(src, 1), :], xbuf.at[sl, pl.ds(r, 1), :], sem_in.at[sl])

    def row_out(sl, r, dst):
        return pltpu.make_async_copy(ybuf.at[sl, pl.ds(r, 1), :], y_hbm.at[pl.ds(dst, 1), :], sem_out.at[sl])

    def start_gather(bb, sl):
        base = base_ref[bb]
        _for_rows(nvalid_ref[bb], lambda r: row_in(bb, sl, r, order_ref[base + r] >> 1).start())

    def wait_gather(bb, sl):
        _for_rows(nvalid_ref[bb], lambda r: row_in(bb, sl, r, 0).wait())

    def start_scatter(bb, sl):
        base = base_ref[bb]

        def one(r):
            a = order_ref[base + r]
            row_out(sl, r, (a & 1) * t + (a >> 1)).start()

        _for_rows(nvalid_ref[bb], one)

    def wait_scatter(bb, sl):
        _for_rows(nvalid_ref[bb], lambda r: row_out(sl, r, 0).wait())

    @pl.when(b == 0)
    def _():
        xbuf[...] = jnp.zeros_like(xbuf)
        start_gather(0, 0)

    @pl.when(b < n_used)
    def _():
        @pl.when(b + 1 < n_used)
        def _():
            start_gather(b + 1, 1 - slot)

        new_expert = (b == 0) | (be_ref[b] != be_ref[jnp.maximum(b - 1, 0)])

        @pl.when(new_expert)
        def _():
            wgb[...] = wg_ref[...].astype(BF16)
            wub[...] = wu_ref[...].astype(BF16)
            wdb[...] = wd_ref[...].astype(BF16)

        wait_gather(b, slot)

        @pl.when(b >= 2)
        def _():
            wait_scatter(b - 2, slot)

        x = xbuf[slot].astype(BF16)
        gate = _dot(x, wgb[...])
        up = _dot(x, wub[...])
        hid = (gate * jax.nn.sigmoid(gate) * up).astype(BF16)
        ybuf[slot] = _dot(hid, wdb[...])
        start_scatter(b, slot)

        @pl.when(b == n_used - 1)
        def _():
            @pl.when(b >= 1)
            def _():
                wait_scatter(b - 1, 1 - slot)

            wait_scatter(b, slot)


def _experts(x, wg, wu, wd, layer, plan):
    t, d = x.shape
    block_expert, base, nvalid, order, n_used = plan
    n_blocks = block_expert.shape[0]

    def wspec(shape):
        return pl.BlockSpec((None, None) + shape, lambda b, be, *_: (layer, be[b], 0, 0))

    grid_spec = pltpu.PrefetchScalarGridSpec(
        num_scalar_prefetch=5,
        grid=(n_blocks,),
        in_specs=[pl.BlockSpec(memory_space=pl.ANY),
                  wspec((d, D_EXPERT)), wspec((d, D_EXPERT)), wspec((D_EXPERT, d))],
        out_specs=pl.BlockSpec(memory_space=pl.ANY),
        scratch_shapes=[pltpu.VMEM((2, MOE_BLOCK, d), F32),
                        pltpu.VMEM((2, MOE_BLOCK, d), F32),
                        pltpu.VMEM((d, D_EXPERT), BF16),
                        pltpu.VMEM((d, D_EXPERT), BF16),
                        pltpu.VMEM((D_EXPERT, d), BF16),
                        pltpu.SemaphoreType.DMA((2,)),
                        pltpu.SemaphoreType.DMA((2,))],
    )
    y = pl.pallas_call(
        functools.partial(_experts_kernel, t=t),
        grid_spec=grid_spec,
        out_shape=jax.ShapeDtypeStruct((TOP_K * t, d), F32),
        compiler_params=_params(1),
        name="experts",
    )(block_expert, base, nvalid, order, n_used, x, wg, wu, wd)
    return y.reshape(TOP_K, t, d)


def _combine_ln_kernel(x_ref, y0_ref, y1_ref, p_ref, g_ref, b_ref, o_ref, ob_ref, *, alpha):
    p = p_ref[...]
    y = p[:, 0:1] * y0_ref[...] + p[:, 1:2] * y1_ref[...]
    out = _layer_norm(alpha * x_ref[...] + y, g_ref[...], b_ref[...])
    o_ref[...] = out
    ob_ref[...] = out.astype(BF16)


def _combine_ln(x, ycomb, probs, g, b, alpha, tm=512):
    t, d = x.shape
    row = pl.BlockSpec((tm, d), lambda i: (i, 0))
    vec = pl.BlockSpec((1, d), lambda i: (0, 0))
    return pl.pallas_call(
        functools.partial(_combine_ln_kernel, alpha=alpha),
        grid=(t // tm,),
        in_specs=[row,
                  pl.BlockSpec((None, tm, d), lambda i: (0, i, 0)),
                  pl.BlockSpec((None, tm, d), lambda i: (1, i, 0)),
                  pl.BlockSpec((tm, ROUTER_LANES), lambda i: (i, 0)), vec, vec],
        out_specs=[row, row],
        out_shape=[jax.ShapeDtypeStruct((t, d), F32), jax.ShapeDtypeStruct((t, d), BF16)],
        compiler_params=_params(1),
        name="combine_ln",
    )(x, ycomb, ycomb, probs, g, b)


def _rope_tables(t, rot_dim, theta, head_dim):
    half = rot_dim // 2
    inv_freq = 1.0 / (theta ** (jnp.arange(0, rot_dim, 2, dtype=F32) / rot_dim))
    ang = jnp.arange(t, dtype=F32)[:, None] * inv_freq[None, :]
    cos, sin = jnp.cos(ang), jnp.sin(ang)
    rest = head_dim - rot_dim
    ones, zeros = jnp.ones((t, rest), F32), jnp.zeros((t, rest), F32)
    zh = jnp.zeros((t, half), F32)
    c = jnp.concatenate([cos, cos, ones], axis=1)
    s1 = jnp.concatenate([zh, sin, zeros], axis=1)
    s2 = jnp.concatenate([-sin, zh, zeros], axis=1)
    rep = LANES // head_dim
    return tuple(jnp.tile(a, (1, rep)) for a in (c, s1, s2))


def _retention_tables():
    h, c = RET_HEADS, RET_CHUNK
    log_gamma = jnp.log1p(-jnp.exp2(-5.0 - jnp.arange(h, dtype=F32)))
    pos = jnp.arange(c, dtype=F32)
    diff = pos[:, None] - pos[None, :]
    decay = jnp.where(diff >= 0, jnp.exp(log_gamma[:, None, None] * jnp.maximum(diff, 0.0)), 0.0)
    k_decay = jnp.exp(log_gamma[:, None] * (c - 1.0 - pos)[None, :])
    q_decay = jnp.exp(log_gamma[:, None] * (pos + 1.0)[None, :])
    chunk_decay = jnp.exp(log_gamma * c)

    def pair_lanes(a):
        a = a.reshape(h // 2, 2, c).transpose(0, 2, 1)
        return jnp.repeat(a, RET_QK_DIM, axis=2)

    cdec = jnp.broadcast_to(chunk_decay[:, None, None], (h, 1, RET_V_DIM))
    return decay, pair_lanes(k_decay), pair_lanes(q_decay), cdec


def _dispatch_plan(expert, t):
    a = t * TOP_K
    flat_e = expert.reshape(a)
    order = jnp.argsort(flat_e).astype(jnp.int32)
    counts = jnp.sum(flat_e[None, :] == jnp.arange(N_EXPERTS, dtype=flat_e.dtype)[:, None],
                     axis=1, dtype=jnp.int32)
    start = jnp.cumsum(counts) - counts
    blocks = (counts + MOE_BLOCK - 1) // MOE_BLOCK
    blk_end = jnp.cumsum(blocks)
    n_blocks = -(-a // MOE_BLOCK) + N_EXPERTS
    b = jnp.arange(n_blocks, dtype=jnp.int32)
    block_expert = jnp.minimum(jnp.sum(blk_end[None, :] <= b[:, None], axis=1, dtype=jnp.int32),
                               N_EXPERTS - 1)
    first = (b - (blk_end - blocks)[block_expert]) * MOE_BLOCK
    base = start[block_expert] + first
    nvalid = jnp.clip(counts[block_expert] - first, 0, MOE_BLOCK)
    nvalid = jnp.where(b < blk_end[-1], nvalid, 0)
    n_used = blk_end[-1:].astype(jnp.int32)
    return block_expert, base.astype(jnp.int32), nvalid.astype(jnp.int32), order, n_used


def _token_mixer_ln(h, hb, w_in, ret_gain, w_pool, pool_scale, w_branch, w_out, layer, ln_g, ln_b,
                    rope_a, rope_r, ret_tables, alpha):
    proj = _inproj(hb, w_in, layer)
    qt, kk, vt, bias = _moba_prep(proj, rope_a)
    y_a = _moba_attn(qt, kk, vt, bias)
    y_r = _retention(proj, rope_r, ret_tables, ret_gain[None, :])
    y_p = _pool(proj, w_pool, layer, pool_scale[None, :])
    merged = _merge(y_a, y_r, y_p, w_branch, layer, proj)
    return _outproj_ln(merged, w_out, layer, h, ln_g[None, :], ln_b[None, :], alpha)


def _moe_ln(h, wr_hi, wr_lo, b_router, w_gate, w_up, w_down, layer, ln_g, ln_b, alpha):
    e_pad, p_pad = _router(h, wr_hi, wr_lo, b_router)
    plan = _dispatch_plan(e_pad[:, :TOP_K], h.shape[0])
    ycomb = _experts(h, w_gate, w_up, w_down, layer, plan)
    return _combine_ln(h, ycomb, p_pad, ln_g[None, :], ln_b[None, :], alpha)


def kernel(x, w_in, ret_gain, w_pool, pool_scale, w_branch, w_out, ln1_g, ln1_b, w_r1, b_r1, w_r2, b_r2, w_e_gate, w_e_up, w_e_down, ln2_g, ln2_b):
    bsz, seq, d = x.shape
    depth = w_in.shape[0]
    assert bsz == 1 and d == D_MODEL and seq % 1024 == 0
    t = seq
    alpha = float((2 * depth) ** 0.25)

    rope_a = _rope_tables(t, ROPE_DIM, ROPE_THETA, ATT_HEAD_DIM)
    rope_r = _rope_tables(t, RET_QK_DIM, RET_ROPE_THETA, RET_QK_DIM)
    ret_tables = _retention_tables()

    w_router = jnp.concatenate([w_r1, w_r2], axis=2)
    w_router = jnp.pad(w_router, ((0, 0), (0, 0), (0, ROUTER_LANES - w_router.shape[2])))
    wr_hi = w_router.astype(BF16)
    wr_lo = (w_router - wr_hi.astype(F32)).astype(BF16)
    b_router = jnp.pad(jnp.concatenate([b_r1, b_r2], axis=1),
                       ((0, 0), (0, ROUTER_LANES - N_GROUPS - N_EXPERTS)))[:, None, :]

    w_pool_b, w_branch_b, w_out_b = w_pool.astype(BF16), w_branch.astype(BF16), w_out.astype(BF16)
    h = x.reshape(t, d)
    hb = h.astype(BF16)
    for l in range(depth):
        h = _token_mixer_ln(h, hb, w_in, ret_gain[l], w_pool_b, pool_scale[l], w_branch_b, w_out_b, l,
                            ln1_g[l], ln1_b[l], rope_a, rope_r, ret_tables, alpha)
        h, hb = _moe_ln(h, wr_hi[l], wr_lo[l], b_router[l], w_e_gate, w_e_up, w_e_down, l,
                        ln2_g[l], ln2_b[l], alpha)
    return h.reshape(bsz, seq, d)
```

```python
import functools

import numpy as np
import jax
import jax.numpy as jnp
from jax import lax
from jax.experimental import pallas as pl
from jax.experimental.pallas import tpu as pltpu

D_MODEL = 2048

ATT_HEADS = 8
ATT_HEAD_DIM = 128
ATT_WIDTH = ATT_HEADS * ATT_HEAD_DIM
ROPE_DIM = ATT_HEAD_DIM // 4
ROPE_THETA = 500000.0
MOBA_BLOCK = 256
MOBA_TOPK = 3

RET_HEADS = 8
RET_QK_DIM = 64
RET_V_DIM = 128
RET_QK_WIDTH = RET_HEADS * RET_QK_DIM
RET_V_WIDTH = RET_HEADS * RET_V_DIM
RET_CHUNK = 256
RET_ROPE_THETA = 10000.0

POOL_WINDOWS = (2, 4, 8, 16)
POOL_GROUPS = 4
POOL_GROUP_DIM = 256
POOL_WIDTH = POOL_GROUPS * POOL_GROUP_DIM
POOL_HALO = 16

N_BRANCH = 3
BRANCH_WIDTH = 1024

OFF_QA = 0
OFF_KA = OFF_QA + ATT_WIDTH
OFF_VA = OFF_KA + ATT_WIDTH
OFF_QR = OFF_VA + ATT_WIDTH
OFF_KR = OFF_QR + RET_QK_WIDTH
OFF_VR = OFF_KR + RET_QK_WIDTH
OFF_GR = OFF_VR + RET_V_WIDTH
OFF_POOL = OFF_GR + RET_V_WIDTH
OFF_GATE = OFF_POOL + POOL_WIDTH
N_IN = OFF_GATE + N_BRANCH * D_MODEL

N_GROUPS = 4
EXPERTS_PER_GROUP = 8
N_EXPERTS = N_GROUPS * EXPERTS_PER_GROUP
TOP_K = 2
D_EXPERT = 512
MOE_BLOCK = 256
ROUTER_LANES = 128

LN_EPS = 1e-5
NEG = -1e30
TAKEN = -3e38
LOG2_E = 1.4426950408889634

LANES = 128
VMEM_LIMIT = 56 * 1024 * 1024

F32 = jnp.float32
BF16 = jnp.bfloat16


def _params(n_axes):
    return pltpu.CompilerParams(dimension_semantics=("arbitrary",) * n_axes,
                                vmem_limit_bytes=VMEM_LIMIT)


def _dot(a, b):
    return jnp.dot(a, b, preferred_element_type=F32)


def _dot_nt(a, b):
    return lax.dot_general(a, b, (((1,), (1,)), ((), ())), preferred_element_type=F32)


def _split_bf16(a):
    hi = a.astype(BF16)
    lo = (a - hi.astype(F32)).astype(BF16)
    return hi, lo


def _dot_f32ish(a, b):
    ah, al = _split_bf16(a)
    bh, bl = _split_bf16(b)
    return _dot(ah, bh) + (_dot(ah, bl) + _dot(al, bh))


def _rotate(x, c, s1, s2, shift):
    return x * c + pltpu.roll(x, shift, 1) * s1 + pltpu.roll(x, LANES - shift, 1) * s2


def _layer_norm(v, g, b):
    mu = jnp.mean(v, axis=-1, keepdims=True)
    d = v - mu
    var = jnp.mean(d * d, axis=-1, keepdims=True)
    return d * lax.rsqrt(var + LN_EPS) * g + b


def _inproj_kernel(x_ref, w_ref, o_ref, wb_ref):
    @pl.when(pl.program_id(1) == 0)
    def _():
        wb_ref[...] = w_ref[...].astype(BF16)

    o_ref[...] = _dot(x_ref[...], wb_ref[...])


def _inproj(xb, w, layer, tm=1024, tn=1024):
    t, d = xb.shape
    n = w.shape[2]
    return pl.pallas_call(
        _inproj_kernel,
        grid=(n // tn, t // tm),
        in_specs=[pl.BlockSpec((tm, d), lambda j, i: (i, 0)),
                  pl.BlockSpec((None, d, tn), lambda j, i: (layer, 0, j))],
        out_specs=pl.BlockSpec((tm, tn), lambda j, i: (i, j)),
        out_shape=jax.ShapeDtypeStruct((t, n), F32),
        scratch_shapes=[pltpu.VMEM((d, tn), BF16)],
        compiler_params=_params(2),
        name="inproj",
    )(xb, w)


def _moba_prep_kernel(q_ref, k_ref, v_ref, c_ref, s1_ref, s2_ref,
                      qt_ref, ko_ref, vt_ref, bias_ref, km_ref):
    i = pl.program_id(0)
    nblk = km_ref.shape[0]

    @pl.when(i == 0)
    def _():
        km_ref[...] = jnp.zeros_like(km_ref)

    c, s1, s2 = c_ref[...], s1_ref[...], s2_ref[...]
    blk = lax.broadcasted_iota(jnp.int32, (nblk, MOBA_BLOCK), 0)
    blkf = blk.astype(F32)
    scale = ATT_HEAD_DIM ** -0.5 * LOG2_E
    for h in range(ATT_HEADS):
        sl = slice(h * ATT_HEAD_DIM, (h + 1) * ATT_HEAD_DIM)
        qr = _rotate(q_ref[:, sl], c, s1, s2, ROPE_DIM // 2)
        kr = _rotate(k_ref[:, sl], c, s1, s2, ROPE_DIM // 2)
        ko_ref[h, 0] = kr.astype(BF16)
        km_blk = lax.broadcasted_iota(jnp.int32, (nblk, ATT_HEAD_DIM), 0)
        km_ref[:, sl] = jnp.where(km_blk == i, jnp.mean(kr, axis=0, keepdims=True), km_ref[:, sl])
        qrt = qr.T
        qt_ref[h, 0] = (qrt * scale).astype(BF16)
        vt_ref[h, 0] = v_ref[:, sl].T.astype(BF16)
        gate = _dot_f32ish(km_ref[:, sl], qrt)
        gate = jnp.where(blk < i, gate, NEG)
        for _ in range(MOBA_TOPK):
            top = jnp.max(gate, axis=0, keepdims=True)
            first = jnp.min(jnp.where(gate == top, blkf, float(nblk)), axis=0, keepdims=True)
            gate = jnp.where(blkf == first, TAKEN, gate)
        bias_ref[h, 0] = jnp.where(gate == TAKEN, jnp.where(blk < i, 0.0, NEG), NEG)


def _moba_prep(proj, rope):
    t = proj.shape[0]
    nblk = t // MOBA_BLOCK
    c, s1, s2 = rope
    tbl = pl.BlockSpec((MOBA_BLOCK, LANES), lambda i: (i, 0))
    hshape = (ATT_HEADS, nblk)
    return pl.pallas_call(
        _moba_prep_kernel,
        grid=(nblk,),
        in_specs=[pl.BlockSpec((MOBA_BLOCK, ATT_WIDTH), lambda i: (i, OFF_QA // ATT_WIDTH)),
                  pl.BlockSpec((MOBA_BLOCK, ATT_WIDTH), lambda i: (i, OFF_KA // ATT_WIDTH)),
                  pl.BlockSpec((MOBA_BLOCK, ATT_WIDTH), lambda i: (i, OFF_VA // ATT_WIDTH)),
                  tbl, tbl, tbl],
        out_specs=[pl.BlockSpec((ATT_HEADS, 1, ATT_HEAD_DIM, MOBA_BLOCK), lambda i: (0, i, 0, 0)),
                   pl.BlockSpec((ATT_HEADS, 1, MOBA_BLOCK, ATT_HEAD_DIM), lambda i: (0, i, 0, 0)),
                   pl.BlockSpec((ATT_HEADS, 1, ATT_HEAD_DIM, MOBA_BLOCK), lambda i: (0, i, 0, 0)),
                   pl.BlockSpec((ATT_HEADS, 1, nblk, MOBA_BLOCK), lambda i: (0, i, 0, 0))],
        out_shape=[jax.ShapeDtypeStruct(hshape + (ATT_HEAD_DIM, MOBA_BLOCK), BF16),
                   jax.ShapeDtypeStruct(hshape + (MOBA_BLOCK, ATT_HEAD_DIM), BF16),
                   jax.ShapeDtypeStruct(hshape + (ATT_HEAD_DIM, MOBA_BLOCK), BF16),
                   jax.ShapeDtypeStruct(hshape + (nblk, MOBA_BLOCK), F32)],
        scratch_shapes=[pltpu.VMEM((nblk, ATT_WIDTH), F32)],
        compiler_params=_params(1),
        name="moba_prep",
    )(proj, proj, proj, c, s1, s2)


def _moba_attn_kernel(qt_ref, k_ref, vt_ref, bias_ref, o_ref, s_ref, p_ref, a_ref, acc_ref):
    i = pl.program_id(1)
    heads = qt_ref.shape[0]
    nblk = k_ref.shape[1]
    shape = (MOBA_BLOCK, MOBA_BLOCK)
    causal = lax.broadcasted_iota(jnp.int32, shape, 0) <= lax.broadcasted_iota(jnp.int32, shape, 1)

    ml = []
    for h in range(heads):
        s = jnp.where(causal, _dot(k_ref[h, i], qt_ref[h, 0]), NEG)
        m = jnp.max(s, axis=0, keepdims=True)
        p = jnp.exp2(s - m)
        ml += [m, jnp.sum(p, axis=0, keepdims=True)]
        acc_ref[h] = _dot(vt_ref[h, i], p.astype(BF16))
        p_ref[1, h] = jnp.zeros(shape, BF16)
        a_ref[1, h] = jnp.ones((1, MOBA_BLOCK), F32)

    def scores(j, slot):
        jc = jnp.minimum(j, nblk - 1)
        for h in range(heads):
            s_ref[slot, h] = _dot(k_ref[h, jc], qt_ref[h, 0]) + bias_ref[h, 0, pl.ds(jc, 1), :]

    def pv(j, slot):
        jc = jnp.maximum(j, 0)
        for h in range(heads):
            acc_ref[h] = a_ref[slot, h] * acc_ref[h] + _dot(vt_ref[h, jc], p_ref[slot, h])

    def softmax(slot, ml):
        out = []
        for h in range(heads):
            m, l = ml[2 * h], ml[2 * h + 1]
            s = s_ref[slot, h]
            m_new = jnp.maximum(m, jnp.max(s, axis=0, keepdims=True))
            alpha = jnp.exp2(m - m_new)
            p = jnp.exp2(s - m_new)
            a_ref[slot, h] = alpha
            p_ref[slot, h] = p.astype(BF16)
            out += [m_new, alpha * l + jnp.sum(p, axis=0, keepdims=True)]
        return out

    scores(0, 0)

    def body(jj, ml):
        j = 2 * jj
        scores(j + 1, 1)
        pv(j - 1, 1)
        ml = softmax(0, ml)
        scores(j + 2, 0)
        pv(j, 0)
        return softmax(1, ml)

    ml = lax.fori_loop(0, (i + 1) // 2, body, ml)
    pv(i - 1, 1)
    for h in range(heads):
        o_ref[:, h * ATT_HEAD_DIM:(h + 1) * ATT_HEAD_DIM] = (acc_ref[h] / ml[2 * h + 1]).T.astype(o_ref.dtype)


def _moba_attn(qt, k, vt, bias, heads=2):
    nblk = qt.shape[1]
    t = nblk * MOBA_BLOCK
    blk2 = (MOBA_BLOCK, MOBA_BLOCK)
    return pl.pallas_call(
        _moba_attn_kernel,
        grid=(ATT_HEADS // heads, nblk),
        in_specs=[pl.BlockSpec((heads, 1, ATT_HEAD_DIM, MOBA_BLOCK), lambda h, i: (h, i, 0, 0)),
                  pl.BlockSpec((heads, nblk, MOBA_BLOCK, ATT_HEAD_DIM), lambda h, i: (h, 0, 0, 0)),
                  pl.BlockSpec((heads, nblk, ATT_HEAD_DIM, MOBA_BLOCK), lambda h, i: (h, 0, 0, 0)),
                  pl.BlockSpec((heads, 1, nblk, MOBA_BLOCK), lambda h, i: (h, i, 0, 0))],
        out_specs=pl.BlockSpec((MOBA_BLOCK, heads * ATT_HEAD_DIM), lambda h, i: (i, h)),
        out_shape=jax.ShapeDtypeStruct((t, ATT_WIDTH), BF16),
        scratch_shapes=[pltpu.VMEM((2, heads) + blk2, F32),
                        pltpu.VMEM((2, heads) + blk2, BF16),
                        pltpu.VMEM((2, heads, 1, MOBA_BLOCK), F32),
                        pltpu.VMEM((heads, ATT_HEAD_DIM, MOBA_BLOCK), F32)],
        compiler_params=_params(2),
        name="moba_attn",
    )(qt, k, vt, bias)


def _retention_kernel(q_ref, k_ref, v_ref, g_ref, c_ref, s1_ref, s2_ref,
                      dec_ref, kdec_ref, qdec_ref, cdec_ref, gain_ref, o_ref, state_ref):
    @pl.when(pl.program_id(1) == 0)
    def _():
        state_ref[...] = jnp.zeros_like(state_ref)

    pairs = kdec_ref.shape[0]
    heads = 2 * pairs
    c, s1, s2 = c_ref[...], s1_ref[...], s2_ref[...]
    lane = lax.broadcasted_iota(jnp.int32, (RET_CHUNK, LANES), 1)

    def vslice(h):
        return slice(h * RET_V_DIM, (h + 1) * RET_V_DIM)

    qm, qdm, kb, kdt = [], [], [], []
    for p in range(pairs):
        psl = slice(p * LANES, (p + 1) * LANES)
        q = _rotate(q_ref[:, psl], c, s1, s2, RET_QK_DIM // 2)
        k = _rotate(k_ref[:, psl], c, s1, s2, RET_QK_DIM // 2) * (RET_QK_DIM ** -0.5)
        qd = q * qdec_ref[p]
        kd = k * kdec_ref[p]
        kb.append(k.astype(BF16))
        for hh in range(2):
            mine = (lane < RET_QK_DIM) if hh == 0 else (lane >= RET_QK_DIM)
            qm.append(jnp.where(mine, q, 0.0).astype(BF16))
            qdm.append(jnp.where(mine, qd, 0.0).astype(BF16))
            kdt.append(jnp.where(mine, kd, 0.0).T.astype(BF16))
    vb = [v_ref[:, vslice(h)].astype(BF16) for h in range(heads)]
    inner = [_dot_nt(qm[h], kb[h // 2]) for h in range(heads)]
    cross = [_dot(qdm[h], state_ref[h].astype(BF16)) for h in range(heads)]
    for h in range(heads):
        state_ref[h] = state_ref[h] * cdec_ref[h] + _dot(kdt[h], vb[h])
    out = [_dot((inner[h] * dec_ref[h]).astype(BF16), vb[h]) + cross[h] for h in range(heads)]
    for h in range(heads):
        mu = jnp.mean(out[h], axis=-1, keepdims=True)
        d = out[h] - mu
        var = jnp.mean(d * d, axis=-1, keepdims=True)
        rn = d * lax.rsqrt(var + LN_EPS) * gain_ref[:, vslice(h)]
        g = g_ref[:, vslice(h)]
        o_ref[:, vslice(h)] = (g * jax.nn.sigmoid(g) * rn).astype(o_ref.dtype)


def _retention(proj, rope, tables, gain, pairs=4):
    t = proj.shape[0]
    n = t // RET_CHUNK
    c, s1, s2 = rope
    dec, kdec, qdec, cdec = tables
    qk_w = pairs * LANES
    v_w = 2 * pairs * RET_V_DIM
    tbl = pl.BlockSpec((RET_CHUNK, LANES), lambda hp, ci: (ci, 0))
    return pl.pallas_call(
        _retention_kernel,
        grid=(RET_HEADS // (2 * pairs), n),
        in_specs=[pl.BlockSpec((RET_CHUNK, qk_w), lambda hp, ci: (ci, OFF_QR // qk_w + hp)),
                  pl.BlockSpec((RET_CHUNK, qk_w), lambda hp, ci: (ci, OFF_KR // qk_w + hp)),
                  pl.BlockSpec((RET_CHUNK, v_w), lambda hp, ci: (ci, OFF_VR // v_w + hp)),
                  pl.BlockSpec((RET_CHUNK, v_w), lambda hp, ci: (ci, OFF_GR // v_w + hp)),
                  tbl, tbl, tbl,
                  pl.BlockSpec((2 * pairs, RET_CHUNK, RET_CHUNK), lambda hp, ci: (hp, 0, 0)),
                  pl.BlockSpec((pairs, RET_CHUNK, LANES), lambda hp, ci: (hp, 0, 0)),
                  pl.BlockSpec((pairs, RET_CHUNK, LANES), lambda hp, ci: (hp, 0, 0)),
                  pl.BlockSpec((2 * pairs, 1, RET_V_DIM), lambda hp, ci: (hp, 0, 0)),
                  pl.BlockSpec((1, v_w), lambda hp, ci: (0, hp))],
        out_specs=pl.BlockSpec((RET_CHUNK, v_w), lambda hp, ci: (ci, hp)),
        out_shape=jax.ShapeDtypeStruct((t, RET_V_WIDTH), BF16),
        scratch_shapes=[pltpu.VMEM((2 * pairs, LANES, RET_V_DIM), F32)],
        compiler_params=_params(2),
        name="retention",
    )(proj, proj, proj, proj, c, s1, s2, dec, kdec, qdec, cdec, gain)


def _pool_kernel(cur_ref, prev_ref, w_ref, scale_ref, o_ref, ext_ref, *, tb):
    i = pl.program_id(0)
    g = pl.program_id(1)
    x = cur_ref[...]
    prev = jnp.where(i > 0, prev_ref[...], 0.0)
    ext_ref[0:POOL_HALO, :] = prev
    ext_ref[POOL_HALO:, :] = x
    pos = (i * tb + 1 + lax.broadcasted_iota(jnp.int32, (tb, 1), 0)).astype(F32)
    for gi, w in enumerate(POOL_WINDOWS):
        @pl.when(g == gi)
        def _(w=w):
            tot = x
            for sft in range(1, w):
                tot = tot + ext_ref[POOL_HALO - sft:POOL_HALO - sft + tb, :]
            d = tot / jnp.minimum(pos, float(w)) - x
            y = _dot(d.astype(BF16), w_ref[0])
            o_ref[...] = (y * scale_ref[...]).astype(o_ref.dtype)


def _pool(proj, w_pool, layer, scale, tb=512):
    t = proj.shape[0]
    gd = POOL_GROUP_DIM
    return pl.pallas_call(
        functools.partial(_pool_kernel, tb=tb),
        grid=(t // tb, POOL_GROUPS),
        in_specs=[pl.BlockSpec((tb, gd), lambda i, g: (i, OFF_POOL // gd + g)),
                  pl.BlockSpec((POOL_HALO, gd),
                               lambda i, g: (jnp.maximum(i * (tb // POOL_HALO) - 1, 0), OFF_POOL // gd + g)),
                  pl.BlockSpec((None, 1, gd, gd), lambda i, g: (layer, g, 0, 0)),
                  pl.BlockSpec((1, gd), lambda i, g: (0, g))],
        out_specs=pl.BlockSpec((tb, gd), lambda i, g: (i, g)),
        out_shape=jax.ShapeDtypeStruct((t, POOL_WIDTH), BF16),
        scratch_shapes=[pltpu.VMEM((tb + POOL_HALO, gd), F32)],
        compiler_params=_params(2),
        name="pool",
    )(proj, proj, w_pool, scale)


def _merge_kernel(ya_ref, yr_ref, yp_ref, w_ref, ga_ref, gr_ref, gp_ref, o_ref):
    acc = jax.nn.sigmoid(ga_ref[...]) * _dot(ya_ref[...], w_ref[0])
    acc = acc + jax.nn.sigmoid(gr_ref[...]) * _dot(yr_ref[...], w_ref[1])
    acc = acc + jax.nn.sigmoid(gp_ref[...]) * _dot(yp_ref[...], w_ref[2])
    o_ref[...] = acc.astype(o_ref.dtype)


def _merge(ya, yr, yp, w_branch, layer, proj, tm=1024, tn=512):
    t = ya.shape[0]
    d = w_branch.shape[3]
    br = pl.BlockSpec((tm, BRANCH_WIDTH), lambda i, j: (i, 0))

    def gate(nb):
        return pl.BlockSpec((tm, tn), lambda i, j: (i, (OFF_GATE + nb * d) // tn + j))

    return pl.pallas_call(
        _merge_kernel,
        grid=(t // tm, d // tn),
        in_specs=[br, br, br,
                  pl.BlockSpec((None, N_BRANCH, BRANCH_WIDTH, tn), lambda i, j: (layer, 0, 0, j)),
                  gate(0), gate(1), gate(2)],
        out_specs=pl.BlockSpec((tm, tn), lambda i, j: (i, j)),
        out_shape=jax.ShapeDtypeStruct((t, d), BF16),
        compiler_params=_params(2),
        name="merge",
    )(ya, yr, yp, w_branch, proj, proj, proj)


def _outproj_ln_kernel(m_ref, w_ref, x_ref, g_ref, b_ref, o_ref, *, alpha):
    y = _dot(m_ref[...], w_ref[...])
    o_ref[...] = _layer_norm(alpha * x_ref[...] + y, g_ref[...], b_ref[...])


def _outproj_ln(merged, w_out, layer, x, g, b, alpha, tm=512):
    t, d = x.shape
    row = pl.BlockSpec((tm, d), lambda i: (i, 0))
    vec = pl.BlockSpec((1, d), lambda i: (0, 0))
    return pl.pallas_call(
        functools.partial(_outproj_ln_kernel, alpha=alpha),
        grid=(t // tm,),
        in_specs=[row, pl.BlockSpec((None, d, d), lambda i: (layer, 0, 0)), row, vec, vec],
        out_specs=row,
        out_shape=jax.ShapeDtypeStruct((t, d), F32),
        compiler_params=_params(1),
        name="outproj_ln",
    )(merged, w_out, x, g, b)


def _router_kernel(x_ref, wh_ref, wl_ref, b_ref, e_ref, p_ref):
    x = x_ref[...]
    xh, xl = _split_bf16(x)
    z = _dot(xh, wh_ref[...]) + (_dot(xh, wl_ref[...]) + _dot(xl, wh_ref[...])) + b_ref[...]
    lane = lax.broadcasted_iota(jnp.int32, z.shape, 1)
    big = jnp.int32(ROUTER_LANES)

    def masked_softmax(mask):
        zm = jnp.where(mask, z, NEG)
        e = jnp.where(mask, jnp.exp(zm - jnp.max(zm, axis=-1, keepdims=True)), 0.0)
        return e / jnp.sum(e, axis=-1, keepdims=True)

    def top1(p, mask):
        pm = jnp.where(mask, p, -1.0)
        top = jnp.max(pm, axis=-1, keepdims=True)
        idx = jnp.min(jnp.where(pm == top, lane, big), axis=-1, keepdims=True)
        return top, idx

    in_groups = lane < N_GROUPS
    p1 = masked_softmax(in_groups)
    g_top, g_idx = top1(p1, in_groups)
    lo = N_GROUPS + g_idx * EXPERTS_PER_GROUP
    in_group = (lane >= lo) & (lane < lo + EXPERTS_PER_GROUP)
    p2 = masked_softmax(in_group)
    e1, i1 = top1(p2, in_group)
    e2, i2 = top1(p2, in_group & (lane != i1))
    denom = e1 + e2
    w1 = g_top * e1 / denom
    w2 = g_top * e2 / denom
    e_ref[...] = jnp.where(lane == 0, i1 - N_GROUPS, jnp.where(lane == 1, i2 - N_GROUPS, 0))
    p_ref[...] = jnp.where(lane == 0, w1, jnp.where(lane == 1, w2, 0.0))


def _router(x, w_hi, w_lo, bias, tm=512):
    t, d = x.shape
    row = pl.BlockSpec((tm, d), lambda i: (i, 0))
    wsp = pl.BlockSpec((d, ROUTER_LANES), lambda i: (0, 0))
    out = pl.BlockSpec((tm, ROUTER_LANES), lambda i: (i, 0))
    return pl.pallas_call(
        _router_kernel,
        grid=(t // tm,),
        in_specs=[row, wsp, wsp, pl.BlockSpec((1, ROUTER_LANES), lambda i: (0, 0))],
        out_specs=[out, out],
        out_shape=[jax.ShapeDtypeStruct((t, ROUTER_LANES), jnp.int32),
                   jax.ShapeDtypeStruct((t, ROUTER_LANES), F32)],
        compiler_params=_params(1),
        name="router",
    )(x, w_hi, w_lo, bias)


WAIT_CHUNKS = (64, 8, 1)


def _start_rows(n, copy):
    for r in range(MOE_BLOCK):
        @pl.when(r < n)
        def _(r=r):
            copy(r).start()


def _wait_rows(n, span):
    rem = n
    for c in WAIT_CHUNKS:
        cnt = lax.shift_right_logical(rem, c.bit_length() - 1)

        def one(_, carry, c=c):
            span(c).wait()
            return carry

        lax.fori_loop(0, cnt, one, 0)
        rem = rem & (c - 1)


def _experts_kernel(be_ref, base_ref, nvalid_ref, src_ref, dst_ref, nused_ref,
                    x_hbm, wg_ref, wu_ref, wd_ref, y_hbm,
                    xbuf, ybuf, wgb, wub, wdb, sem_in, sem_out):
    b = pl.program_id(0)
    n_used = nused_ref[0]
    slot = b & 1

    def start_gather(bb, sl):
        base = base_ref[bb]
        _start_rows(nvalid_ref[bb], lambda r: pltpu.make_async_copy(
            x_hbm.at[pl.ds(src_ref[base + r], 1), :], xbuf.at[sl, pl.ds(r, 1), :], sem_in.at[sl]))

    def wait_gather(bb, sl):
        _wait_rows(nvalid_ref[bb], lambda c: pltpu.make_async_copy(
            x_hbm.at[pl.ds(0, c), :], xbuf.at[sl, pl.ds(0, c), :], sem_in.at[sl]))

    def start_scatter(bb, sl):
        base = base_ref[bb]
        _start_rows(nvalid_ref[bb], lambda r: pltpu.make_async_copy(
            ybuf.at[sl, pl.ds(r, 1), :], y_hbm.at[pl.ds(dst_ref[base + r], 1), :], sem_out.at[sl]))

    def wait_scatter(bb, sl):
        _wait_rows(nvalid_ref[bb], lambda c: pltpu.make_async_copy(
            ybuf.at[sl, pl.ds(0, c), :], y_hbm.at[pl.ds(0, c), :], sem_out.at[sl]))

    @pl.when(b == 0)
    def _():
        xbuf[...] = jnp.zeros_like(xbuf)
        start_gather(0, 0)

    @pl.when(b < n_used)
    def _():
        @pl.when(b + 1 < n_used)
        def _():
            start_gather(b + 1, 1 - slot)

        new_expert = (b == 0) | (be_ref[b] != be_ref[jnp.maximum(b - 1, 0)])

        @pl.when(new_expert)
        def _():
            wgb[...] = wg_ref[...].astype(BF16)
            wub[...] = wu_ref[...].astype(BF16)
            wdb[...] = wd_ref[...].astype(BF16)

        wait_gather(b, slot)

        @pl.when(b >= 2)
        def _():
            wait_scatter(b - 2, slot)

        x = xbuf[slot].astype(BF16)
        gate = _dot(x, wgb[...])
        up = _dot(x, wub[...])
        hid = (gate * jax.nn.sigmoid(gate) * up).astype(BF16)
        ybuf[slot] = _dot(hid, wdb[...])
        start_scatter(b, slot)

        @pl.when(b == n_used - 1)
        def _():
            @pl.when(b >= 1)
            def _():
                wait_scatter(b - 1, 1 - slot)

            wait_scatter(b, slot)


def _experts(x, wg, wu, wd, layer, plan):
    t, d = x.shape
    block_expert, base, nvalid, src, dst, n_used = plan
    n_blocks = block_expert.shape[0]

    def wspec(shape):
        return pl.BlockSpec((None, None) + shape, lambda b, be, *_: (layer, be[b], 0, 0))

    grid_spec = pltpu.PrefetchScalarGridSpec(
        num_scalar_prefetch=6,
        grid=(n_blocks,),
        in_specs=[pl.BlockSpec(memory_space=pl.ANY),
                  wspec((d, D_EXPERT)), wspec((d, D_EXPERT)), wspec((D_EXPERT, d))],
        out_specs=pl.BlockSpec(memory_space=pl.ANY),
        scratch_shapes=[pltpu.VMEM((2, MOE_BLOCK, d), F32),
                        pltpu.VMEM((2, MOE_BLOCK, d), F32),
                        pltpu.VMEM((d, D_EXPERT), BF16),
                        pltpu.VMEM((d, D_EXPERT), BF16),
                        pltpu.VMEM((D_EXPERT, d), BF16),
                        pltpu.SemaphoreType.DMA((2,)),
                        pltpu.SemaphoreType.DMA((2,))],
    )
    y = pl.pallas_call(
        _experts_kernel,
        grid_spec=grid_spec,
        out_shape=jax.ShapeDtypeStruct((TOP_K * t, d), F32),
        compiler_params=_params(1),
        name="experts",
    )(block_expert, base, nvalid, src, dst, n_used, x, wg, wu, wd)
    return y.reshape(TOP_K, t, d)


def _combine_ln_kernel(x_ref, y0_ref, y1_ref, p_ref, g_ref, b_ref, o_ref, ob_ref, *, alpha):
    p = p_ref[...]
    y = p[:, 0:1] * y0_ref[...] + p[:, 1:2] * y1_ref[...]
    out = _layer_norm(alpha * x_ref[...] + y, g_ref[...], b_ref[...])
    o_ref[...] = out
    ob_ref[...] = out.astype(BF16)


def _combine_ln(x, ycomb, probs, g, b, alpha, tm=512):
    t, d = x.shape
    row = pl.BlockSpec((tm, d), lambda i: (i, 0))
    vec = pl.BlockSpec((1, d), lambda i: (0, 0))
    return pl.pallas_call(
        functools.partial(_combine_ln_kernel, alpha=alpha),
        grid=(t // tm,),
        in_specs=[row,
                  pl.BlockSpec((None, tm, d), lambda i: (0, i, 0)),
                  pl.BlockSpec((None, tm, d), lambda i: (1, i, 0)),
                  pl.BlockSpec((tm, ROUTER_LANES), lambda i: (i, 0)), vec, vec],
        out_specs=[row, row],
        out_shape=[jax.ShapeDtypeStruct((t, d), F32), jax.ShapeDtypeStruct((t, d), BF16)],
        compiler_params=_params(1),
        name="combine_ln",
    )(x, ycomb, ycomb, probs, g, b)


def _rope_tables(t, rot_dim, theta, head_dim):
    half = rot_dim // 2
    inv_freq = 1.0 / (theta ** (jnp.arange(0, rot_dim, 2, dtype=F32) / rot_dim))
    ang = jnp.arange(t, dtype=F32)[:, None] * inv_freq[None, :]
    cos, sin = jnp.cos(ang), jnp.sin(ang)
    rest = head_dim - rot_dim
    ones, zeros = jnp.ones((t, rest), F32), jnp.zeros((t, rest), F32)
    zh = jnp.zeros((t, half), F32)
    c = jnp.concatenate([cos, cos, ones], axis=1)
    s1 = jnp.concatenate([zh, sin, zeros], axis=1)
    s2 = jnp.concatenate([-sin, zh, zeros], axis=1)
    rep = LANES // head_dim
    return tuple(jnp.tile(a, (1, rep)) for a in (c, s1, s2))


def _retention_tables():
    h, c = RET_HEADS, RET_CHUNK
    log_gamma = jnp.log1p(-jnp.exp2(-5.0 - jnp.arange(h, dtype=F32)))
    pos = jnp.arange(c, dtype=F32)
    diff = pos[:, None] - pos[None, :]
    decay = jnp.where(diff >= 0, jnp.exp(log_gamma[:, None, None] * jnp.maximum(diff, 0.0)), 0.0)
    k_decay = jnp.exp(log_gamma[:, None] * (c - 1.0 - pos)[None, :])
    q_decay = jnp.exp(log_gamma[:, None] * (pos + 1.0)[None, :])
    chunk_decay = jnp.exp(log_gamma * c)

    def pair_lanes(a):
        a = a.reshape(h // 2, 2, c).transpose(0, 2, 1)
        return jnp.repeat(a, RET_QK_DIM, axis=2)

    cdec = jnp.broadcast_to(chunk_decay[:, None, None], (h, 1, RET_V_DIM))
    return decay, pair_lanes(k_decay), pair_lanes(q_decay), cdec


def _dispatch_plan(expert, t):
    a = t * TOP_K
    flat_e = expert.reshape(a)
    order = jnp.argsort(flat_e).astype(jnp.int32)
    counts = jnp.sum(flat_e[None, :] == jnp.arange(N_EXPERTS, dtype=flat_e.dtype)[:, None],
                     axis=1, dtype=jnp.int32)
    start = jnp.cumsum(counts) - counts
    blocks = (counts + MOE_BLOCK - 1) // MOE_BLOCK
    blk_end = jnp.cumsum(blocks)
    n_blocks = -(-a // MOE_BLOCK) + N_EXPERTS
    b = jnp.arange(n_blocks, dtype=jnp.int32)
    block_expert = jnp.minimum(jnp.sum(blk_end[None, :] <= b[:, None], axis=1, dtype=jnp.int32),
                               N_EXPERTS - 1)
    first = (b - (blk_end - blocks)[block_expert]) * MOE_BLOCK
    base = start[block_expert] + first
    nvalid = jnp.clip(counts[block_expert] - first, 0, MOE_BLOCK)
    nvalid = jnp.where(b < blk_end[-1], nvalid, 0)
    n_used = blk_end[-1:].astype(jnp.int32)
    order = jnp.pad(order, (0, MOE_BLOCK))
    src = order // TOP_K
    dst = (order % TOP_K) * t + src
    return block_expert, base.astype(jnp.int32), nvalid.astype(jnp.int32), src, dst, n_used


def _token_mixer_ln(h, hb, w_in, ret_gain, w_pool, pool_scale, w_branch, w_out, layer, ln_g, ln_b,
                    rope_a, rope_r, ret_tables, alpha):
    proj = _inproj(hb, w_in, layer)
    qt, kk, vt, bias = _moba_prep(proj, rope_a)
    y_a = _moba_attn(qt, kk, vt, bias)
    y_r = _retention(proj, rope_r, ret_tables, ret_gain[None, :])
    y_p = _pool(proj, w_pool, layer, pool_scale[None, :])
    merged = _merge(y_a, y_r, y_p, w_branch, layer, proj)
    return _outproj_ln(merged, w_out, layer, h, ln_g[None, :], ln_b[None, :], alpha)


def _moe_ln(h, wr_hi, wr_lo, b_router, w_gate, w_up, w_down, layer, ln_g, ln_b, alpha):
    e_pad, p_pad = _router(h, wr_hi, wr_lo, b_router)
    plan = _dispatch_plan(e_pad[:, :TOP_K], h.shape[0])
    ycomb = _experts(h, w_gate, w_up, w_down, layer, plan)
    return _combine_ln(h, ycomb, p_pad, ln_g[None, :], ln_b[None, :], alpha)


def kernel(x, w_in, ret_gain, w_pool, pool_scale, w_branch, w_out, ln1_g, ln1_b, w_r1, b_r1, w_r2, b_r2, w_e_gate, w_e_up, w_e_down, ln2_g, ln2_b):
    bsz, seq, d = x.shape
    depth = w_in.shape[0]
    assert bsz == 1 and d == D_MODEL and seq % 1024 == 0
    t = seq
    alpha = float((2 * depth) ** 0.25)

    rope_a = _rope_tables(t, ROPE_DIM, ROPE_THETA, ATT_HEAD_DIM)
    rope_r = _rope_tables(t, RET_QK_DIM, RET_ROPE_THETA, RET_QK_DIM)
    ret_tables = _retention_tables()

    w_router = jnp.concatenate([w_r1, w_r2], axis=2)
    w_router = jnp.pad(w_router, ((0, 0), (0, 0), (0, ROUTER_LANES - w_router.shape[2])))
    wr_hi = w_router.astype(BF16)
    wr_lo = (w_router - wr_hi.astype(F32)).astype(BF16)
    b_router = jnp.pad(jnp.concatenate([b_r1, b_r2], axis=1),
                       ((0, 0), (0, ROUTER_LANES - N_GROUPS - N_EXPERTS)))[:, None, :]

    w_pool_b, w_branch_b, w_out_b = w_pool.astype(BF16), w_branch.astype(BF16), w_out.astype(BF16)
    h = x.reshape(t, d)
    hb = h.astype(BF16)
    for l in range(depth):
        h = _token_mixer_ln(h, hb, w_in, ret_gain[l], w_pool_b, pool_scale[l], w_branch_b, w_out_b, l,
                            ln1_g[l], ln1_b[l], rope_a, rope_r, ret_tables, alpha)
        h, hb = _moe_ln(h, wr_hi[l], wr_lo[l], b_router[l], w_e_gate, w_e_up, w_e_down, l,
                        ln2_g[l], ln2_b[l], alpha)
    return h.reshape(bsz, seq, d)
```

```python
import functools

import numpy as np
import jax
import jax.numpy as jnp
from jax import lax
from jax.experimental import pallas as pl
from jax.experimental.pallas import tpu as pltpu

D_MODEL = 2048

ATT_HEADS = 8
ATT_HEAD_DIM = 128
ATT_WIDTH = ATT_HEADS * ATT_HEAD_DIM
ROPE_DIM = ATT_HEAD_DIM // 4
ROPE_THETA = 500000.0
MOBA_BLOCK = 256
MOBA_TOPK = 3

RET_HEADS = 8
RET_QK_DIM = 64
RET_V_DIM = 128
RET_QK_WIDTH = RET_HEADS * RET_QK_DIM
RET_V_WIDTH = RET_HEADS * RET_V_DIM
RET_CHUNK = 256
RET_ROPE_THETA = 10000.0

POOL_WINDOWS = (2, 4, 8, 16)
POOL_GROUPS = 4
POOL_GROUP_DIM = 256
POOL_WIDTH = POOL_GROUPS * POOL_GROUP_DIM
POOL_HALO = 16

N_BRANCH = 3
BRANCH_WIDTH = 1024

OFF_QA = 0
OFF_KA = OFF_QA + ATT_WIDTH
OFF_VA = OFF_KA + ATT_WIDTH
OFF_QR = OFF_VA + ATT_WIDTH
OFF_KR = OFF_QR + RET_QK_WIDTH
OFF_VR = OFF_KR + RET_QK_WIDTH
OFF_GR = OFF_VR + RET_V_WIDTH
OFF_POOL = OFF_GR + RET_V_WIDTH
OFF_GATE = OFF_POOL + POOL_WIDTH
N_IN = OFF_GATE + N_BRANCH * D_MODEL

N_GROUPS = 4
EXPERTS_PER_GROUP = 8
N_EXPERTS = N_GROUPS * EXPERTS_PER_GROUP
TOP_K = 2
D_EXPERT = 512
MOE_BLOCK = 256
ROUTER_LANES = 128

LN_EPS = 1e-5
NEG = -1e30
TAKEN = -3e38
LOG2_E = 1.4426950408889634

LANES = 128
VMEM_LIMIT = 56 * 1024 * 1024

F32 = jnp.float32
BF16 = jnp.bfloat16


def _params(n_axes):
    return pltpu.CompilerParams(dimension_semantics=("arbitrary",) * n_axes,
                                vmem_limit_bytes=VMEM_LIMIT)


def _dot(a, b):
    return jnp.dot(a, b, preferred_element_type=F32)


def _dot_nt(a, b):
    return lax.dot_general(a, b, (((1,), (1,)), ((), ())), preferred_element_type=F32)


def _split_bf16(a):
    hi = a.astype(BF16)
    lo = (a - hi.astype(F32)).astype(BF16)
    return hi, lo


def _dot_f32ish(a, b):
    ah, al = _split_bf16(a)
    bh, bl = _split_bf16(b)
    return _dot(ah, bh) + (_dot(ah, bl) + _dot(al, bh))


def _rotate(x, c, s1, s2, shift):
    return x * c + pltpu.roll(x, shift, 1) * s1 + pltpu.roll(x, LANES - shift, 1) * s2


def _layer_norm(v, g, b):
    mu = jnp.mean(v, axis=-1, keepdims=True)
    d = v - mu
    var = jnp.mean(d * d, axis=-1, keepdims=True)
    return d * lax.rsqrt(var + LN_EPS) * g + b


def _inproj_kernel(x_ref, w_ref, o_ref, wb_ref):
    @pl.when(pl.program_id(1) == 0)
    def _():
        wb_ref[...] = w_ref[...].astype(BF16)

    o_ref[...] = _dot(x_ref[...], wb_ref[...])


def _inproj(xb, w, layer, tm=1024, tn=1024):
    t, d = xb.shape
    n = w.shape[2]
    return pl.pallas_call(
        _inproj_kernel,
        grid=(n // tn, t // tm),
        in_specs=[pl.BlockSpec((tm, d), lambda j, i: (i, 0)),
                  pl.BlockSpec((None, d, tn), lambda j, i: (layer, 0, j))],
        out_specs=pl.BlockSpec((tm, tn), lambda j, i: (i, j)),
        out_shape=jax.ShapeDtypeStruct((t, n), F32),
        scratch_shapes=[pltpu.VMEM((d, tn), BF16)],
        compiler_params=_params(2),
        name="inproj",
    )(xb, w)


ATT_KDIM = 2 * ATT_HEAD_DIM
ATT_VROWS = ATT_HEAD_DIM + 16


def _moba_prep_kernel(q_ref, k_ref, v_ref, c_ref, s1_ref, s2_ref, qt_ref, ko_ref, vt_ref, km_ref):
    i = pl.program_id(0)
    nblk = km_ref.shape[0]

    @pl.when(i == 0)
    def _():
        km_ref[...] = jnp.zeros_like(km_ref)

    c, s1, s2 = c_ref[...], s1_ref[...], s2_ref[...]
    blk = lax.broadcasted_iota(jnp.int32, (nblk, MOBA_BLOCK), 0)
    blkf = blk.astype(F32)
    scale = ATT_HEAD_DIM ** -0.5 * LOG2_E
    lane = lax.broadcasted_iota(jnp.int32, (MOBA_BLOCK, ATT_KDIM - ATT_HEAD_DIM), 1)
    this_block = jnp.where(lane == i, 1.0, 0.0).astype(BF16)
    row = lax.broadcasted_iota(jnp.int32, (ATT_VROWS - ATT_HEAD_DIM, MOBA_BLOCK), 0)
    ones_row = jnp.where(row == 0, 1.0, 0.0).astype(BF16)
    pad_rows = ATT_KDIM - ATT_HEAD_DIM - nblk
    for h in range(ATT_HEADS):
        sl = slice(h * ATT_HEAD_DIM, (h + 1) * ATT_HEAD_DIM)
        qr = _rotate(q_ref[:, sl], c, s1, s2, ROPE_DIM // 2)
        kr = _rotate(k_ref[:, sl], c, s1, s2, ROPE_DIM // 2)
        ko_ref[h, 0, :, :ATT_HEAD_DIM] = kr.astype(BF16)
        ko_ref[h, 0, :, ATT_HEAD_DIM:] = this_block
        km_blk = lax.broadcasted_iota(jnp.int32, (nblk, ATT_HEAD_DIM), 0)
        km_ref[:, sl] = jnp.where(km_blk == i, jnp.mean(kr, axis=0, keepdims=True), km_ref[:, sl])
        qrt = qr.T
        qt_ref[h, 0, :ATT_HEAD_DIM, :] = (qrt * scale).astype(BF16)
        vt_ref[h, 0, :ATT_HEAD_DIM, :] = v_ref[:, sl].T.astype(BF16)
        vt_ref[h, 0, ATT_HEAD_DIM:, :] = ones_row
        gate = _dot_f32ish(km_ref[:, sl], qrt)
        gate = jnp.where(blk < i, gate, NEG)
        for _ in range(MOBA_TOPK):
            top = jnp.max(gate, axis=0, keepdims=True)
            first = jnp.min(jnp.where(gate == top, blkf, float(nblk)), axis=0, keepdims=True)
            gate = jnp.where(blkf == first, TAKEN, gate)
        bias = jnp.where(gate == TAKEN, jnp.where(blk < i, 0.0, NEG), NEG)
        qt_ref[h, 0, ATT_HEAD_DIM:ATT_HEAD_DIM + nblk, :] = bias.astype(BF16)
        if pad_rows:
            qt_ref[h, 0, ATT_HEAD_DIM + nblk:, :] = jnp.zeros((pad_rows, MOBA_BLOCK), BF16)


def _moba_prep(proj, rope):
    t = proj.shape[0]
    nblk = t // MOBA_BLOCK
    assert ATT_HEAD_DIM + nblk <= ATT_KDIM and nblk % 16 == 0
    c, s1, s2 = rope
    tbl = pl.BlockSpec((MOBA_BLOCK, LANES), lambda i: (i, 0))
    hshape = (ATT_HEADS, nblk)
    return pl.pallas_call(
        _moba_prep_kernel,
        grid=(nblk,),
        in_specs=[pl.BlockSpec((MOBA_BLOCK, ATT_WIDTH), lambda i: (i, OFF_QA // ATT_WIDTH)),
                  pl.BlockSpec((MOBA_BLOCK, ATT_WIDTH), lambda i: (i, OFF_KA // ATT_WIDTH)),
                  pl.BlockSpec((MOBA_BLOCK, ATT_WIDTH), lambda i: (i, OFF_VA // ATT_WIDTH)),
                  tbl, tbl, tbl],
        out_specs=[pl.BlockSpec((ATT_HEADS, 1, ATT_KDIM, MOBA_BLOCK), lambda i: (0, i, 0, 0)),
                   pl.BlockSpec((ATT_HEADS, 1, MOBA_BLOCK, ATT_KDIM), lambda i: (0, i, 0, 0)),
                   pl.BlockSpec((ATT_HEADS, 1, ATT_VROWS, MOBA_BLOCK), lambda i: (0, i, 0, 0))],
        out_shape=[jax.ShapeDtypeStruct(hshape + (ATT_KDIM, MOBA_BLOCK), BF16),
                   jax.ShapeDtypeStruct(hshape + (MOBA_BLOCK, ATT_KDIM), BF16),
                   jax.ShapeDtypeStruct(hshape + (ATT_VROWS, MOBA_BLOCK), BF16)],
        scratch_shapes=[pltpu.VMEM((nblk, ATT_WIDTH), F32)],
        compiler_params=_params(1),
        name="moba_prep",
    )(proj, proj, proj, c, s1, s2)


def _moba_attn_kernel(qt_ref, k_ref, vt_ref, o_ref, s_ref, p_ref, a_ref, acc_ref):
    i = pl.program_id(1)
    heads = qt_ref.shape[0]
    nblk = k_ref.shape[1]
    shape = (MOBA_BLOCK, MOBA_BLOCK)
    causal = lax.broadcasted_iota(jnp.int32, shape, 0) <= lax.broadcasted_iota(jnp.int32, shape, 1)

    ms = []
    for h in range(heads):
        s = _dot(k_ref[h, i, :, :ATT_HEAD_DIM], qt_ref[h, 0, :ATT_HEAD_DIM, :])
        s = jnp.where(causal, s, NEG)
        m = jnp.max(s, axis=0, keepdims=True)
        ms.append(m)
        acc_ref[h] = _dot(vt_ref[h, i], jnp.exp2(s - m).astype(BF16))
        p_ref[1, h] = jnp.zeros(shape, BF16)
        a_ref[1, h] = jnp.ones((1, MOBA_BLOCK), F32)

    def scores(j, slot):
        jc = jnp.minimum(j, nblk - 1)
        for h in range(heads):
            s_ref[slot, h] = _dot(k_ref[h, jc], qt_ref[h, 0])

    def pv(j, slot):
        jc = jnp.maximum(j, 0)
        for h in range(heads):
            acc_ref[h] = a_ref[slot, h] * acc_ref[h] + _dot(vt_ref[h, jc], p_ref[slot, h])

    def softmax(slot, ms):
        out = []
        for h in range(heads):
            s = s_ref[slot, h]
            m_new = jnp.maximum(ms[h], jnp.max(s, axis=0, keepdims=True))
            a_ref[slot, h] = jnp.exp2(ms[h] - m_new)
            p_ref[slot, h] = jnp.exp2(s - m_new).astype(BF16)
            out.append(m_new)
        return out

    scores(0, 0)

    def body(jj, ms):
        j = 2 * jj
        scores(j + 1, 1)
        pv(j - 1, 1)
        ms = softmax(0, ms)
        scores(j + 2, 0)
        pv(j, 0)
        return softmax(1, ms)

    lax.fori_loop(0, (i + 1) // 2, body, ms)
    pv(i - 1, 1)
    for h in range(heads):
        acc = acc_ref[h]
        out = acc[:ATT_HEAD_DIM] / acc[ATT_HEAD_DIM:ATT_HEAD_DIM + 1]
        o_ref[:, h * ATT_HEAD_DIM:(h + 1) * ATT_HEAD_DIM] = out.T.astype(o_ref.dtype)


def _moba_attn(qt, k, vt, heads=2):
    nblk = qt.shape[1]
    t = nblk * MOBA_BLOCK
    blk2 = (MOBA_BLOCK, MOBA_BLOCK)
    return pl.pallas_call(
        _moba_attn_kernel,
        grid=(ATT_HEADS // heads, nblk),
        in_specs=[pl.BlockSpec((heads, 1, ATT_KDIM, MOBA_BLOCK), lambda h, i: (h, i, 0, 0)),
                  pl.BlockSpec((heads, nblk, MOBA_BLOCK, ATT_KDIM), lambda h, i: (h, 0, 0, 0)),
                  pl.BlockSpec((heads, nblk, ATT_VROWS, MOBA_BLOCK), lambda h, i: (h, 0, 0, 0))],
        out_specs=pl.BlockSpec((MOBA_BLOCK, heads * ATT_HEAD_DIM), lambda h, i: (i, h)),
        out_shape=jax.ShapeDtypeStruct((t, ATT_WIDTH), BF16),
        scratch_shapes=[pltpu.VMEM((2, heads) + blk2, F32),
                        pltpu.VMEM((2, heads) + blk2, BF16),
                        pltpu.VMEM((2, heads, 1, MOBA_BLOCK), F32),
                        pltpu.VMEM((heads, ATT_VROWS, MOBA_BLOCK), F32)],
        compiler_params=_params(2),
        name="moba_attn",
    )(qt, k, vt)


def _retention_kernel(q_ref, k_ref, v_ref, g_ref, c_ref, s1_ref, s2_ref,
                      dec_ref, kdec_ref, qdec_ref, cdec_ref, gain_ref, o_ref, state_ref):
    @pl.when(pl.program_id(1) == 0)
    def _():
        state_ref[...] = jnp.zeros_like(state_ref)

    pairs = kdec_ref.shape[0]
    heads = 2 * pairs
    c, s1, s2 = c_ref[...], s1_ref[...], s2_ref[...]
    lane = lax.broadcasted_iota(jnp.int32, (RET_CHUNK, LANES), 1)

    def vslice(h):
        return slice(h * RET_V_DIM, (h + 1) * RET_V_DIM)

    qm, qdm, kb, kdt = [], [], [], []
    for p in range(pairs):
        psl = slice(p * LANES, (p + 1) * LANES)
        q = _rotate(q_ref[:, psl], c, s1, s2, RET_QK_DIM // 2)
        k = _rotate(k_ref[:, psl], c, s1, s2, RET_QK_DIM // 2) * (RET_QK_DIM ** -0.5)
        qd = q * qdec_ref[p]
        kd = k * kdec_ref[p]
        kb.append(k.astype(BF16))
        for hh in range(2):
            mine = (lane < RET_QK_DIM) if hh == 0 else (lane >= RET_QK_DIM)
            qm.append(jnp.where(mine, q, 0.0).astype(BF16))
            qdm.append(jnp.where(mine, qd, 0.0).astype(BF16))
            kdt.append(jnp.where(mine, kd, 0.0).T.astype(BF16))
    vb = [v_ref[:, vslice(h)].astype(BF16) for h in range(heads)]
    inner = [_dot_nt(qm[h], kb[h // 2]) for h in range(heads)]
    cross = [_dot(qdm[h], state_ref[h].astype(BF16)) for h in range(heads)]
    for h in range(heads):
        state_ref[h] = state_ref[h] * cdec_ref[h] + _dot(kdt[h], vb[h])
    out = [_dot((inner[h] * dec_ref[h]).astype(BF16), vb[h]) + cross[h] for h in range(heads)]
    for h in range(heads):
        mu = jnp.mean(out[h], axis=-1, keepdims=True)
        d = out[h] - mu
        var = jnp.mean(d * d, axis=-1, keepdims=True)
        rn = d * lax.rsqrt(var + LN_EPS) * gain_ref[:, vslice(h)]
        g = g_ref[:, vslice(h)]
        o_ref[:, vslice(h)] = (g * jax.nn.sigmoid(g) * rn).astype(o_ref.dtype)


def _retention(proj, rope, tables, gain, pairs=4):
    t = proj.shape[0]
    n = t // RET_CHUNK
    c, s1, s2 = rope
    dec, kdec, qdec, cdec = tables
    qk_w = pairs * LANES
    v_w = 2 * pairs * RET_V_DIM
    tbl = pl.BlockSpec((RET_CHUNK, LANES), lambda hp, ci: (ci, 0))
    return pl.pallas_call(
        _retention_kernel,
        grid=(RET_HEADS // (2 * pairs), n),
        in_specs=[pl.BlockSpec((RET_CHUNK, qk_w), lambda hp, ci: (ci, OFF_QR // qk_w + hp)),
                  pl.BlockSpec((RET_CHUNK, qk_w), lambda hp, ci: (ci, OFF_KR // qk_w + hp)),
                  pl.BlockSpec((RET_CHUNK, v_w), lambda hp, ci: (ci, OFF_VR // v_w + hp)),
                  pl.BlockSpec((RET_CHUNK, v_w), lambda hp, ci: (ci, OFF_GR // v_w + hp)),
                  tbl, tbl, tbl,
                  pl.BlockSpec((2 * pairs, RET_CHUNK, RET_CHUNK), lambda hp, ci: (hp, 0, 0)),
                  pl.BlockSpec((pairs, RET_CHUNK, LANES), lambda hp, ci: (hp, 0, 0)),
                  pl.BlockSpec((pairs, RET_CHUNK, LANES), lambda hp, ci: (hp, 0, 0)),
                  pl.BlockSpec((2 * pairs, 1, RET_V_DIM), lambda hp, ci: (hp, 0, 0)),
                  pl.BlockSpec((1, v_w), lambda hp, ci: (0, hp))],
        out_specs=pl.BlockSpec((RET_CHUNK, v_w), lambda hp, ci: (ci, hp)),
        out_shape=jax.ShapeDtypeStruct((t, RET_V_WIDTH), BF16),
        scratch_shapes=[pltpu.VMEM((2 * pairs, LANES, RET_V_DIM), F32)],
        compiler_params=_params(2),
        name="retention",
    )(proj, proj, proj, proj, c, s1, s2, dec, kdec, qdec, cdec, gain)


def _pool_kernel(cur_ref, prev_ref, w_ref, scale_ref, o_ref, ext_ref, *, tb):
    i = pl.program_id(0)
    g = pl.program_id(1)
    x = cur_ref[...]
    prev = jnp.where(i > 0, prev_ref[...], 0.0)
    ext_ref[0:POOL_HALO, :] = prev
    ext_ref[POOL_HALO:, :] = x
    pos = (i * tb + 1 + lax.broadcasted_iota(jnp.int32, (tb, 1), 0)).astype(F32)
    for gi, w in enumerate(POOL_WINDOWS):
        @pl.when(g == gi)
        def _(w=w):
            tot = x
            for sft in range(1, w):
                tot = tot + ext_ref[POOL_HALO - sft:POOL_HALO - sft + tb, :]
            d = tot / jnp.minimum(pos, float(w)) - x
            y = _dot(d.astype(BF16), w_ref[0])
            o_ref[...] = (y * scale_ref[...]).astype(o_ref.dtype)


def _pool(proj, w_pool, layer, scale, tb=512):
    t = proj.shape[0]
    gd = POOL_GROUP_DIM
    return pl.pallas_call(
        functools.partial(_pool_kernel, tb=tb),
        grid=(t // tb, POOL_GROUPS),
        in_specs=[pl.BlockSpec((tb, gd), lambda i, g: (i, OFF_POOL // gd + g)),
                  pl.BlockSpec((POOL_HALO, gd),
                               lambda i, g: (jnp.maximum(i * (tb // POOL_HALO) - 1, 0), OFF_POOL // gd + g)),
                  pl.BlockSpec((None, 1, gd, gd), lambda i, g: (layer, g, 0, 0)),
                  pl.BlockSpec((1, gd), lambda i, g: (0, g))],
        out_specs=pl.BlockSpec((tb, gd), lambda i, g: (i, g)),
        out_shape=jax.ShapeDtypeStruct((t, POOL_WIDTH), BF16),
        scratch_shapes=[pltpu.VMEM((tb + POOL_HALO, gd), F32)],
        compiler_params=_params(2),
        name="pool",
    )(proj, proj, w_pool, scale)


def _merge_kernel(ya_ref, yr_ref, yp_ref, w_ref, ga_ref, gr_ref, gp_ref, o_ref):
    acc = jax.nn.sigmoid(ga_ref[...]) * _dot(ya_ref[...], w_ref[0])
    acc = acc + jax.nn.sigmoid(gr_ref[...]) * _dot(yr_ref[...], w_ref[1])
    acc = acc + jax.nn.sigmoid(gp_ref[...]) * _dot(yp_ref[...], w_ref[2])
    o_ref[...] = acc.astype(o_ref.dtype)


def _merge(ya, yr, yp, w_branch, layer, proj, tm=1024, tn=512):
    t = ya.shape[0]
    d = w_branch.shape[3]
    br = pl.BlockSpec((tm, BRANCH_WIDTH), lambda i, j: (i, 0))

    def gate(nb):
        return pl.BlockSpec((tm, tn), lambda i, j: (i, (OFF_GATE + nb * d) // tn + j))

    return pl.pallas_call(
        _merge_kernel,
        grid=(t // tm, d // tn),
        in_specs=[br, br, br,
                  pl.BlockSpec((None, N_BRANCH, BRANCH_WIDTH, tn), lambda i, j: (layer, 0, 0, j)),
                  gate(0), gate(1), gate(2)],
        out_specs=pl.BlockSpec((tm, tn), lambda i, j: (i, j)),
        out_shape=jax.ShapeDtypeStruct((t, d), BF16),
        compiler_params=_params(2),
        name="merge",
    )(ya, yr, yp, w_branch, proj, proj, proj)


def _outproj_ln_kernel(m_ref, w_ref, x_ref, g_ref, b_ref, o_ref, *, alpha):
    y = _dot(m_ref[...], w_ref[...])
    o_ref[...] = _layer_norm(alpha * x_ref[...] + y, g_ref[...], b_ref[...])


def _outproj_ln(merged, w_out, layer, x, g, b, alpha, tm=512):
    t, d = x.shape
    row = pl.BlockSpec((tm, d), lambda i: (i, 0))
    vec = pl.BlockSpec((1, d), lambda i: (0, 0))
    return pl.pallas_call(
        functools.partial(_outproj_ln_kernel, alpha=alpha),
        grid=(t // tm,),
        in_specs=[row, pl.BlockSpec((None, d, d), lambda i: (layer, 0, 0)), row, vec, vec],
        out_specs=row,
        out_shape=jax.ShapeDtypeStruct((t, d), F32),
        compiler_params=_params(1),
        name="outproj_ln",
    )(merged, w_out, x, g, b)


def _router_kernel(x_ref, wh_ref, wl_ref, b_ref, e_ref, p_ref):
    x = x_ref[...]
    xh, xl = _split_bf16(x)
    z = _dot(xh, wh_ref[...]) + (_dot(xh, wl_ref[...]) + _dot(xl, wh_ref[...])) + b_ref[...]
    lane = lax.broadcasted_iota(jnp.int32, z.shape, 1)
    big = jnp.int32(ROUTER_LANES)

    def masked_softmax(mask):
        zm = jnp.where(mask, z, NEG)
        e = jnp.where(mask, jnp.exp(zm - jnp.max(zm, axis=-1, keepdims=True)), 0.0)
        return e / jnp.sum(e, axis=-1, keepdims=True)

    def top1(p, mask):
        pm = jnp.where(mask, p, -1.0)
        top = jnp.max(pm, axis=-1, keepdims=True)
        idx = jnp.min(jnp.where(pm == top, lane, big), axis=-1, keepdims=True)
        return top, idx

    in_groups = lane < N_GROUPS
    p1 = masked_softmax(in_groups)
    g_top, g_idx = top1(p1, in_groups)
    lo = N_GROUPS + g_idx * EXPERTS_PER_GROUP
    in_group = (lane >= lo) & (lane < lo + EXPERTS_PER_GROUP)
    p2 = masked_softmax(in_group)
    e1, i1 = top1(p2, in_group)
    e2, i2 = top1(p2, in_group & (lane != i1))
    denom = e1 + e2
    w1 = g_top * e1 / denom
    w2 = g_top * e2 / denom
    e_ref[...] = jnp.where(lane == 0, i1 - N_GROUPS, jnp.where(lane == 1, i2 - N_GROUPS, 0))
    p_ref[...] = jnp.where(lane == 0, w1, jnp.where(lane == 1, w2, 0.0))


def _router(x, w_hi, w_lo, bias, tm=512):
    t, d = x.shape
    row = pl.BlockSpec((tm, d), lambda i: (i, 0))
    wsp = pl.BlockSpec((d, ROUTER_LANES), lambda i: (0, 0))
    out = pl.BlockSpec((tm, ROUTER_LANES), lambda i: (i, 0))
    return pl.pallas_call(
        _router_kernel,
        grid=(t // tm,),
        in_specs=[row, wsp, wsp, pl.BlockSpec((1, ROUTER_LANES), lambda i: (0, 0))],
        out_specs=[out, out],
        out_shape=[jax.ShapeDtypeStruct((t, ROUTER_LANES), jnp.int32),
                   jax.ShapeDtypeStruct((t, ROUTER_LANES), F32)],
        compiler_params=_params(1),
        name="router",
    )(x, w_hi, w_lo, bias)


WAIT_CHUNKS = (64, 8, 1)


def _start_rows(n, copy):
    for r in range(MOE_BLOCK):
        @pl.when(r < n)
        def _(r=r):
            copy(r).start()


def _wait_rows(n, span):
    rem = n
    for c in WAIT_CHUNKS:
        cnt = lax.shift_right_logical(rem, c.bit_length() - 1)

        def one(_, carry, c=c):
            span(c).wait()
            return carry

        lax.fori_loop(0, cnt, one, 0)
        rem = rem & (c - 1)


def _experts_kernel(be_ref, base_ref, nvalid_ref, src_ref, dst_ref, nused_ref,
                    x_hbm, wg_ref, wu_ref, wd_ref, y_hbm,
                    xbuf, ybuf, wgb, wub, wdb, sem_in, sem_out):
    b = pl.program_id(0)
    n_used = nused_ref[0]
    slot = b & 1

    def start_gather(bb, sl):
        base = base_ref[bb]
        _start_rows(nvalid_ref[bb], lambda r: pltpu.make_async_copy(
            x_hbm.at[pl.ds(src_ref[base + r], 1), :], xbuf.at[sl, pl.ds(r, 1), :], sem_in.at[sl]))

    def wait_gather(bb, sl):
        _wait_rows(nvalid_ref[bb], lambda c: pltpu.make_async_copy(
            x_hbm.at[pl.ds(0, c), :], xbuf.at[sl, pl.ds(0, c), :], sem_in.at[sl]))

    def start_scatter(bb, sl):
        base = base_ref[bb]
        _start_rows(nvalid_ref[bb], lambda r: pltpu.make_async_copy(
            ybuf.at[sl, pl.ds(r, 1), :], y_hbm.at[pl.ds(dst_ref[base + r], 1), :], sem_out.at[sl]))

    def wait_scatter(bb, sl):
        _wait_rows(nvalid_ref[bb], lambda c: pltpu.make_async_copy(
            ybuf.at[sl, pl.ds(0, c), :], y_hbm.at[pl.ds(0, c), :], sem_out.at[sl]))

    @pl.when(b == 0)
    def _():
        xbuf[...] = jnp.zeros_like(xbuf)
        start_gather(0, 0)

    @pl.when(b < n_used)
    def _():
        @pl.when(b + 1 < n_used)
        def _():
            start_gather(b + 1, 1 - slot)

        new_expert = (b == 0) | (be_ref[b] != be_ref[jnp.maximum(b - 1, 0)])

        @pl.when(new_expert)
        def _():
            wgb[...] = wg_ref[...].astype(BF16)
            wub[...] = wu_ref[...].astype(BF16)
            wdb[...] = wd_ref[...].astype(BF16)

        wait_gather(b, slot)

        @pl.when(b >= 2)
        def _():
            wait_scatter(b - 2, slot)

        x = xbuf[slot].astype(BF16)
        gate = _dot(x, wgb[...])
        up = _dot(x, wub[...])
        hid = (gate * jax.nn.sigmoid(gate) * up).astype(BF16)
        ybuf[slot] = _dot(hid, wdb[...])
        start_scatter(b, slot)

        @pl.when(b == n_used - 1)
        def _():
            @pl.when(b >= 1)
            def _():
                wait_scatter(b - 1, 1 - slot)

            wait_scatter(b, slot)


def _experts(x, wg, wu, wd, layer, plan):
    t, d = x.shape
    block_expert, base, nvalid, src, dst, n_used = plan
    n_blocks = block_expert.shape[0]

    def wspec(shape):
        return pl.BlockSpec((None, None) + shape, lambda b, be, *_: (layer, be[b], 0, 0))

    grid_spec = pltpu.PrefetchScalarGridSpec(
        num_scalar_prefetch=6,
        grid=(n_blocks,),
        in_specs=[pl.BlockSpec(memory_space=pl.ANY),
                  wspec((d, D_EXPERT)), wspec((d, D_EXPERT)), wspec((D_EXPERT, d))],
        out_specs=pl.BlockSpec(memory_space=pl.ANY),
        scratch_shapes=[pltpu.VMEM((2, MOE_BLOCK, d), F32),
                        pltpu.VMEM((2, MOE_BLOCK, d), F32),
                        pltpu.VMEM((d, D_EXPERT), BF16),
                        pltpu.VMEM((d, D_EXPERT), BF16),
                        pltpu.VMEM((D_EXPERT, d), BF16),
                        pltpu.SemaphoreType.DMA((2,)),
                        pltpu.SemaphoreType.DMA((2,))],
    )
    y = pl.pallas_call(
        _experts_kernel,
        grid_spec=grid_spec,
        out_shape=jax.ShapeDtypeStruct((TOP_K * t, d), F32),
        compiler_params=_params(1),
        name="experts",
    )(block_expert, base, nvalid, src, dst, n_used, x, wg, wu, wd)
    return y.reshape(TOP_K, t, d)


def _combine_ln_kernel(x_ref, y0_ref, y1_ref, p_ref, g_ref, b_ref, o_ref, ob_ref, *, alpha):
    p = p_ref[...]
    y = p[:, 0:1] * y0_ref[...] + p[:, 1:2] * y1_ref[...]
    out = _layer_norm(alpha * x_ref[...] + y, g_ref[...], b_ref[...])
    o_ref[...] = out
    ob_ref[...] = out.astype(BF16)


def _combine_ln(x, ycomb, probs, g, b, alpha, tm=512):
    t, d = x.shape
    row = pl.BlockSpec((tm, d), lambda i: (i, 0))
    vec = pl.BlockSpec((1, d), lambda i: (0, 0))
    return pl.pallas_call(
        functools.partial(_combine_ln_kernel, alpha=alpha),
        grid=(t // tm,),
        in_specs=[row,
                  pl.BlockSpec((None, tm, d), lambda i: (0, i, 0)),
                  pl.BlockSpec((None, tm, d), lambda i: (1, i, 0)),
                  pl.BlockSpec((tm, ROUTER_LANES), lambda i: (i, 0)), vec, vec],
        out_specs=[row, row],
        out_shape=[jax.ShapeDtypeStruct((t, d), F32), jax.ShapeDtypeStruct((t, d), BF16)],
        compiler_params=_params(1),
        name="combine_ln",
    )(x, ycomb, ycomb, probs, g, b)


def _rope_tables(t, rot_dim, theta, head_dim):
    half = rot_dim // 2
    inv_freq = 1.0 / (theta ** (jnp.arange(0, rot_dim, 2, dtype=F32) / rot_dim))
    ang = jnp.arange(t, dtype=F32)[:, None] * inv_freq[None, :]
    cos, sin = jnp.cos(ang), jnp.sin(ang)
    rest = head_dim - rot_dim
    ones, zeros = jnp.ones((t, rest), F32), jnp.zeros((t, rest), F32)
    zh = jnp.zeros((t, half), F32)
    c = jnp.concatenate([cos, cos, ones], axis=1)
    s1 = jnp.concatenate([zh, sin, zeros], axis=1)
    s2 = jnp.concatenate([-sin, zh, zeros], axis=1)
    rep = LANES // head_dim
    return tuple(jnp.tile(a, (1, rep)) for a in (c, s1, s2))


def _retention_tables():
    h, c = RET_HEADS, RET_CHUNK
    log_gamma = jnp.log1p(-jnp.exp2(-5.0 - jnp.arange(h, dtype=F32)))
    pos = jnp.arange(c, dtype=F32)
    diff = pos[:, None] - pos[None, :]
    decay = jnp.where(diff >= 0, jnp.exp(log_gamma[:, None, None] * jnp.maximum(diff, 0.0)), 0.0)
    k_decay = jnp.exp(log_gamma[:, None] * (c - 1.0 - pos)[None, :])
    q_decay = jnp.exp(log_gamma[:, None] * (pos + 1.0)[None, :])
    chunk_decay = jnp.exp(log_gamma * c)

    def pair_lanes(a):
        a = a.reshape(h // 2, 2, c).transpose(0, 2, 1)
        return jnp.repeat(a, RET_QK_DIM, axis=2)

    cdec = jnp.broadcast_to(chunk_decay[:, None, None], (h, 1, RET_V_DIM))
    return decay, pair_lanes(k_decay), pair_lanes(q_decay), cdec


def _dispatch_plan(expert, t):
    a = t * TOP_K
    flat_e = expert.reshape(a)
    order = jnp.argsort(flat_e).astype(jnp.int32)
    counts = jnp.sum(flat_e[None, :] == jnp.arange(N_EXPERTS, dtype=flat_e.dtype)[:, None],
                     axis=1, dtype=jnp.int32)
    start = jnp.cumsum(counts) - counts
    blocks = (counts + MOE_BLOCK - 1) // MOE_BLOCK
    blk_end = jnp.cumsum(blocks)
    n_blocks = -(-a // MOE_BLOCK) + N_EXPERTS
    b = jnp.arange(n_blocks, dtype=jnp.int32)
    block_expert = jnp.minimum(jnp.sum(blk_end[None, :] <= b[:, None], axis=1, dtype=jnp.int32),
                               N_EXPERTS - 1)
    first = (b - (blk_end - blocks)[block_expert]) * MOE_BLOCK
    base = start[block_expert] + first
    nvalid = jnp.clip(counts[block_expert] - first, 0, MOE_BLOCK)
    nvalid = jnp.where(b < blk_end[-1], nvalid, 0)
    n_used = blk_end[-1:].astype(jnp.int32)
    order = jnp.pad(order, (0, MOE_BLOCK))
    src = order // TOP_K
    dst = (order % TOP_K) * t + src
    return block_expert, base.astype(jnp.int32), nvalid.astype(jnp.int32), src, dst, n_used


def _token_mixer_ln(h, hb, w_in, ret_gain, w_pool, pool_scale, w_branch, w_out, layer, ln_g, ln_b,
                    rope_a, rope_r, ret_tables, alpha):
    proj = _inproj(hb, w_in, layer)
    qt, kk, vt = _moba_prep(proj, rope_a)
    y_a = _moba_attn(qt, kk, vt)
    y_r = _retention(proj, rope_r, ret_tables, ret_gain[None, :])
    y_p = _pool(proj, w_pool, layer, pool_scale[None, :])
    merged = _merge(y_a, y_r, y_p, w_branch, layer, proj)
    return _outproj_ln(merged, w_out, layer, h, ln_g[None, :], ln_b[None, :], alpha)


def _moe_ln(h, wr_hi, wr_lo, b_router, w_gate, w_up, w_down, layer, ln_g, ln_b, alpha):
    e_pad, p_pad = _router(h, wr_hi, wr_lo, b_router)
    plan = _dispatch_plan(e_pad[:, :TOP_K], h.shape[0])
    ycomb = _experts(h, w_gate, w_up, w_down, layer, plan)
    return _combine_ln(h, ycomb, p_pad, ln_g[None, :], ln_b[None, :], alpha)


def kernel(x, w_in, ret_gain, w_pool, pool_scale, w_branch, w_out, ln1_g, ln1_b, w_r1, b_r1, w_r2, b_r2, w_e_gate, w_e_up, w_e_down, ln2_g, ln2_b):
    bsz, seq, d = x.shape
    depth = w_in.shape[0]
    assert bsz == 1 and d == D_MODEL and seq % 1024 == 0
    t = seq
    alpha = float((2 * depth) ** 0.25)

    rope_a = _rope_tables(t, ROPE_DIM, ROPE_THETA, ATT_HEAD_DIM)
    rope_r = _rope_tables(t, RET_QK_DIM, RET_ROPE_THETA, RET_QK_DIM)
    ret_tables = _retention_tables()

    w_router = jnp.concatenate([w_r1, w_r2], axis=2)
    w_router = jnp.pad(w_router, ((0, 0), (0, 0), (0, ROUTER_LANES - w_router.shape[2])))
    wr_hi = w_router.astype(BF16)
    wr_lo = (w_router - wr_hi.astype(F32)).astype(BF16)
    b_router = jnp.pad(jnp.concatenate([b_r1, b_r2], axis=1),
                       ((0, 0), (0, ROUTER_LANES - N_GROUPS - N_EXPERTS)))[:, None, :]

    w_pool_b, w_branch_b, w_out_b = w_pool.astype(BF16), w_branch.astype(BF16), w_out.astype(BF16)
    h = x.reshape(t, d)
    hb = h.astype(BF16)
    for l in range(depth):
        h = _token_mixer_ln(h, hb, w_in, ret_gain[l], w_pool_b, pool_scale[l], w_branch_b, w_out_b, l,
                            ln1_g[l], ln1_b[l], rope_a, rope_r, ret_tables, alpha)
        h, hb = _moe_ln(h, wr_hi[l], wr_lo[l], b_router[l], w_e_gate, w_e_up, w_e_down, l,
                        ln2_g[l], ln2_b[l], alpha)
    return h.reshape(bsz, seq, d)
```

```python
import functools

import numpy as np
import jax
import jax.numpy as jnp
from jax import lax
from jax.experimental import pallas as pl
from jax.experimental.pallas import tpu as pltpu

D_MODEL = 2048

ATT_HEADS = 8
ATT_HEAD_DIM = 128
ATT_WIDTH = ATT_HEADS * ATT_HEAD_DIM
ROPE_DIM = ATT_HEAD_DIM // 4
ROPE_THETA = 500000.0
MOBA_BLOCK = 256
MOBA_TOPK = 3

RET_HEADS = 8
RET_QK_DIM = 64
RET_V_DIM = 128
RET_QK_WIDTH = RET_HEADS * RET_QK_DIM
RET_V_WIDTH = RET_HEADS * RET_V_DIM
RET_CHUNK = 256
RET_ROPE_THETA = 10000.0

POOL_WINDOWS = (2, 4, 8, 16)
POOL_GROUPS = 4
POOL_GROUP_DIM = 256
POOL_WIDTH = POOL_GROUPS * POOL_GROUP_DIM
POOL_HALO = 32

N_BRANCH = 3
BRANCH_WIDTH = 1024

OFF_QA = 0
OFF_KA = OFF_QA + ATT_WIDTH
OFF_VA = OFF_KA + ATT_WIDTH
OFF_QR = OFF_VA + ATT_WIDTH
OFF_KR = OFF_QR + RET_QK_WIDTH
OFF_VR = OFF_KR + RET_QK_WIDTH
OFF_GR = OFF_VR + RET_V_WIDTH
OFF_POOL = OFF_GR + RET_V_WIDTH
OFF_GATE = OFF_POOL + POOL_WIDTH
N_IN = OFF_GATE + N_BRANCH * D_MODEL

N_GROUPS = 4
EXPERTS_PER_GROUP = 8
N_EXPERTS = N_GROUPS * EXPERTS_PER_GROUP
TOP_K = 2
D_EXPERT = 512
MOE_BLOCK = 256
ROUTER_LANES = 128

LN_EPS = 1e-5
NEG = -1e30
TAKEN = -3e38
LOG2_E = 1.4426950408889634

LANES = 128
SUBLANES = 8
VMEM_LIMIT = 56 * 1024 * 1024

F32 = jnp.float32
BF16 = jnp.bfloat16


def _params(n_axes):
    return pltpu.CompilerParams(dimension_semantics=("arbitrary",) * n_axes,
                                vmem_limit_bytes=VMEM_LIMIT)


def _dot(a, b):
    return jnp.dot(a, b, preferred_element_type=F32)


def _dot_nt(a, b):
    return lax.dot_general(a, b, (((1,), (1,)), ((), ())), preferred_element_type=F32)


def _split_bf16(a):
    hi = a.astype(BF16)
    lo = (a - hi.astype(F32)).astype(BF16)
    return hi, lo


def _dot_f32ish(a, b):
    ah, al = _split_bf16(a)
    bh, bl = _split_bf16(b)
    return _dot(ah, bh) + (_dot(ah, bl) + _dot(al, bh))


def _rotate(x, c, s1, s2, shift):
    return x * c + pltpu.roll(x, shift, 1) * s1 + pltpu.roll(x, LANES - shift, 1) * s2


def _layer_norm(v, g, b):
    mu = jnp.mean(v, axis=-1, keepdims=True)
    d = v - mu
    var = jnp.mean(d * d, axis=-1, keepdims=True)
    return d * lax.rsqrt(var + LN_EPS) * g + b


def _inproj_kernel(x_ref, w_ref, o_ref, wb_ref, *, gate):
    @pl.when(pl.program_id(1) == 0)
    def _():
        wb_ref[...] = w_ref[...].astype(BF16)

    y = _dot(x_ref[...], wb_ref[...])
    o_ref[...] = jax.nn.sigmoid(y).astype(o_ref.dtype) if gate else y


def _inproj(xb, w, layer, col0, ncols, gate=False, tm=1024, tn=1024):
    t, d = xb.shape
    return pl.pallas_call(
        functools.partial(_inproj_kernel, gate=gate),
        grid=(ncols // tn, t // tm),
        in_specs=[pl.BlockSpec((tm, d), lambda j, i: (i, 0)),
                  pl.BlockSpec((None, d, tn), lambda j, i: (layer, 0, col0 // tn + j))],
        out_specs=pl.BlockSpec((tm, tn), lambda j, i: (i, j)),
        out_shape=jax.ShapeDtypeStruct((t, ncols), BF16 if gate else F32),
        scratch_shapes=[pltpu.VMEM((d, tn), BF16)],
        compiler_params=_params(2),
        name="inproj_gates" if gate else "inproj",
    )(xb, w)


ATT_KDIM = 2 * ATT_HEAD_DIM
ATT_VROWS = ATT_HEAD_DIM + 16


def _moba_prep_kernel(q_ref, k_ref, v_ref, c_ref, s1_ref, s2_ref, qt_ref, ko_ref, vt_ref, km_ref):
    i = pl.program_id(0)
    nblk = km_ref.shape[0]

    @pl.when(i == 0)
    def _():
        km_ref[...] = jnp.zeros_like(km_ref)

    c, s1, s2 = c_ref[...], s1_ref[...], s2_ref[...]
    blk = lax.broadcasted_iota(jnp.int32, (nblk, MOBA_BLOCK), 0)
    blkf = blk.astype(F32)
    scale = ATT_HEAD_DIM ** -0.5 * LOG2_E
    lane = lax.broadcasted_iota(jnp.int32, (MOBA_BLOCK, ATT_KDIM - ATT_HEAD_DIM), 1)
    this_block = jnp.where(lane == i, 1.0, 0.0).astype(BF16)
    row = lax.broadcasted_iota(jnp.int32, (ATT_VROWS - ATT_HEAD_DIM, MOBA_BLOCK), 0)
    ones_row = jnp.where(row == 0, 1.0, 0.0).astype(BF16)
    pad_rows = ATT_KDIM - ATT_HEAD_DIM - nblk
    for h in range(ATT_HEADS):
        sl = slice(h * ATT_HEAD_DIM, (h + 1) * ATT_HEAD_DIM)
        qr = _rotate(q_ref[:, sl], c, s1, s2, ROPE_DIM // 2)
        kr = _rotate(k_ref[:, sl], c, s1, s2, ROPE_DIM // 2)
        ko_ref[h, 0, :, :ATT_HEAD_DIM] = kr.astype(BF16)
        ko_ref[h, 0, :, ATT_HEAD_DIM:] = this_block
        km_blk = lax.broadcasted_iota(jnp.int32, (nblk, ATT_HEAD_DIM), 0)
        km_ref[:, sl] = jnp.where(km_blk == i, jnp.mean(kr, axis=0, keepdims=True), km_ref[:, sl])
        qrt = qr.T
        qt_ref[h, 0, :ATT_HEAD_DIM, :] = (qrt * scale).astype(BF16)
        vt_ref[h, 0, :ATT_HEAD_DIM, :] = v_ref[:, sl].T.astype(BF16)
        vt_ref[h, 0, ATT_HEAD_DIM:, :] = ones_row
        gate = _dot_f32ish(km_ref[:, sl], qrt)
        gate = jnp.where(blk < i, gate, NEG)
        for _ in range(MOBA_TOPK):
            top = jnp.max(gate, axis=0, keepdims=True)
            first = jnp.min(jnp.where(gate == top, blkf, float(nblk)), axis=0, keepdims=True)
            gate = jnp.where(blkf == first, TAKEN, gate)
        bias = jnp.where(gate == TAKEN, jnp.where(blk < i, 0.0, NEG), NEG)
        qt_ref[h, 0, ATT_HEAD_DIM:ATT_HEAD_DIM + nblk, :] = bias.astype(BF16)
        if pad_rows:
            qt_ref[h, 0, ATT_HEAD_DIM + nblk:, :] = jnp.zeros((pad_rows, MOBA_BLOCK), BF16)


def _moba_prep(proj, rope):
    t = proj.shape[0]
    nblk = t // MOBA_BLOCK
    assert ATT_HEAD_DIM + nblk <= ATT_KDIM and nblk % 16 == 0
    c, s1, s2 = rope
    tbl = pl.BlockSpec((MOBA_BLOCK, LANES), lambda i: (i, 0))
    hshape = (ATT_HEADS, nblk)
    return pl.pallas_call(
        _moba_prep_kernel,
        grid=(nblk,),
        in_specs=[pl.BlockSpec((MOBA_BLOCK, ATT_WIDTH), lambda i: (i, OFF_QA // ATT_WIDTH)),
                  pl.BlockSpec((MOBA_BLOCK, ATT_WIDTH), lambda i: (i, OFF_KA // ATT_WIDTH)),
                  pl.BlockSpec((MOBA_BLOCK, ATT_WIDTH), lambda i: (i, OFF_VA // ATT_WIDTH)),
                  tbl, tbl, tbl],
        out_specs=[pl.BlockSpec((ATT_HEADS, 1, ATT_KDIM, MOBA_BLOCK), lambda i: (0, i, 0, 0)),
                   pl.BlockSpec((ATT_HEADS, 1, MOBA_BLOCK, ATT_KDIM), lambda i: (0, i, 0, 0)),
                   pl.BlockSpec((ATT_HEADS, 1, ATT_VROWS, MOBA_BLOCK), lambda i: (0, i, 0, 0))],
        out_shape=[jax.ShapeDtypeStruct(hshape + (ATT_KDIM, MOBA_BLOCK), BF16),
                   jax.ShapeDtypeStruct(hshape + (MOBA_BLOCK, ATT_KDIM), BF16),
                   jax.ShapeDtypeStruct(hshape + (ATT_VROWS, MOBA_BLOCK), BF16)],
        scratch_shapes=[pltpu.VMEM((nblk, ATT_WIDTH), F32)],
        compiler_params=_params(1),
        name="moba_prep",
    )(proj, proj, proj, c, s1, s2)


def _moba_attn_kernel(qt_ref, k_ref, vt_ref, o_ref, s_ref, p_ref, a_ref, acc_ref):
    i = pl.program_id(1)
    heads = qt_ref.shape[0]
    nblk = k_ref.shape[1]
    shape = (MOBA_BLOCK, MOBA_BLOCK)
    causal = lax.broadcasted_iota(jnp.int32, shape, 0) <= lax.broadcasted_iota(jnp.int32, shape, 1)

    ms = []
    for h in range(heads):
        s = _dot(k_ref[h, i, :, :ATT_HEAD_DIM], qt_ref[h, 0, :ATT_HEAD_DIM, :])
        s = jnp.where(causal, s, NEG)
        m = jnp.max(s, axis=0, keepdims=True)
        ms.append(m)
        acc_ref[h] = _dot(vt_ref[h, i], jnp.exp2(s - m).astype(BF16))
        p_ref[1, h] = jnp.zeros(shape, BF16)
        a_ref[1, h] = jnp.ones((1, MOBA_BLOCK), F32)

    def scores(j, slot):
        jc = jnp.minimum(j, nblk - 1)
        for h in range(heads):
            s_ref[slot, h] = _dot(k_ref[h, jc], qt_ref[h, 0])

    def pv(j, slot):
        jc = jnp.maximum(j, 0)
        for h in range(heads):
            acc_ref[h] = a_ref[slot, h] * acc_ref[h] + _dot(vt_ref[h, jc], p_ref[slot, h])

    def softmax(slot, ms):
        out = []
        for h in range(heads):
            s = s_ref[slot, h]
            m_new = jnp.maximum(ms[h], jnp.max(s, axis=0, keepdims=True))
            a_ref[slot, h] = jnp.exp2(ms[h] - m_new)
            p_ref[slot, h] = jnp.exp2(s - m_new).astype(BF16)
            out.append(m_new)
        return out

    scores(0, 0)

    def body(jj, ms):
        j = 2 * jj
        scores(j + 1, 1)
        pv(j - 1, 1)
        ms = softmax(0, ms)
        scores(j + 2, 0)
        pv(j, 0)
        return softmax(1, ms)

    lax.fori_loop(0, (i + 1) // 2, body, ms)
    pv(i - 1, 1)
    for h in range(heads):
        acc = acc_ref[h]
        out = acc[:ATT_HEAD_DIM] / acc[ATT_HEAD_DIM:ATT_HEAD_DIM + 1]
        o_ref[:, h * ATT_HEAD_DIM:(h + 1) * ATT_HEAD_DIM] = out.T.astype(o_ref.dtype)


def _moba_attn(qt, k, vt, heads=2):
    nblk = qt.shape[1]
    t = nblk * MOBA_BLOCK
    blk2 = (MOBA_BLOCK, MOBA_BLOCK)
    return pl.pallas_call(
        _moba_attn_kernel,
        grid=(ATT_HEADS // heads, nblk),
        in_specs=[pl.BlockSpec((heads, 1, ATT_KDIM, MOBA_BLOCK), lambda h, i: (h, i, 0, 0)),
                  pl.BlockSpec((heads, nblk, MOBA_BLOCK, ATT_KDIM), lambda h, i: (h, 0, 0, 0)),
                  pl.BlockSpec((heads, nblk, ATT_VROWS, MOBA_BLOCK), lambda h, i: (h, 0, 0, 0))],
        out_specs=pl.BlockSpec((MOBA_BLOCK, heads * ATT_HEAD_DIM), lambda h, i: (i, h)),
        out_shape=jax.ShapeDtypeStruct((t, ATT_WIDTH), BF16),
        scratch_shapes=[pltpu.VMEM((2, heads) + blk2, F32),
                        pltpu.VMEM((2, heads) + blk2, BF16),
                        pltpu.VMEM((2, heads, 1, MOBA_BLOCK), F32),
                        pltpu.VMEM((heads, ATT_VROWS, MOBA_BLOCK), F32)],
        compiler_params=_params(2),
        name="moba_attn",
    )(qt, k, vt)


def _retention_kernel(q_ref, k_ref, v_ref, g_ref, c_ref, s1_ref, s2_ref,
                      dec_ref, kdec_ref, qdec_ref, cdec_ref, gain_ref, o_ref, state_ref):
    @pl.when(pl.program_id(1) == 0)
    def _():
        state_ref[...] = jnp.zeros_like(state_ref)

    pairs = kdec_ref.shape[0]
    heads = 2 * pairs
    c, s1, s2 = c_ref[...], s1_ref[...], s2_ref[...]
    lane = lax.broadcasted_iota(jnp.int32, (RET_CHUNK, LANES), 1)

    def vslice(h):
        return slice(h * RET_V_DIM, (h + 1) * RET_V_DIM)

    qm, qdm, kb, kdt = [], [], [], []
    for p in range(pairs):
        psl = slice(p * LANES, (p + 1) * LANES)
        q = _rotate(q_ref[:, psl], c, s1, s2, RET_QK_DIM // 2)
        k = _rotate(k_ref[:, psl], c, s1, s2, RET_QK_DIM // 2) * (RET_QK_DIM ** -0.5)
        qd = q * qdec_ref[p]
        kd = k * kdec_ref[p]
        kb.append(k.astype(BF16))
        for hh in range(2):
            mine = (lane < RET_QK_DIM) if hh == 0 else (lane >= RET_QK_DIM)
            qm.append(jnp.where(mine, q, 0.0).astype(BF16))
            qdm.append(jnp.where(mine, qd, 0.0).astype(BF16))
            kdt.append(jnp.where(mine, kd, 0.0).T.astype(BF16))
    vb = [v_ref[:, vslice(h)].astype(BF16) for h in range(heads)]
    inner = [_dot_nt(qm[h], kb[h // 2]) for h in range(heads)]
    cross = [_dot(qdm[h], state_ref[h].astype(BF16)) for h in range(heads)]
    for h in range(heads):
        state_ref[h] = state_ref[h] * cdec_ref[h] + _dot(kdt[h], vb[h])
    out = [_dot((inner[h] * dec_ref[h]).astype(BF16), vb[h]) + cross[h] for h in range(heads)]
    for h in range(heads):
        mu = jnp.mean(out[h], axis=-1, keepdims=True)
        d = out[h] - mu
        var = jnp.mean(d * d, axis=-1, keepdims=True)
        rn = d * lax.rsqrt(var + LN_EPS) * gain_ref[:, vslice(h)]
        g = g_ref[:, vslice(h)]
        o_ref[:, vslice(h)] = (g * jax.nn.sigmoid(g) * rn).astype(o_ref.dtype)


def _retention(proj, rope, tables, gain, pairs=4):
    t = proj.shape[0]
    n = t // RET_CHUNK
    c, s1, s2 = rope
    dec, kdec, qdec, cdec = tables
    qk_w = pairs * LANES
    v_w = 2 * pairs * RET_V_DIM
    tbl = pl.BlockSpec((RET_CHUNK, LANES), lambda hp, ci: (ci, 0))
    return pl.pallas_call(
        _retention_kernel,
        grid=(RET_HEADS // (2 * pairs), n),
        in_specs=[pl.BlockSpec((RET_CHUNK, qk_w), lambda hp, ci: (ci, OFF_QR // qk_w + hp)),
                  pl.BlockSpec((RET_CHUNK, qk_w), lambda hp, ci: (ci, OFF_KR // qk_w + hp)),
                  pl.BlockSpec((RET_CHUNK, v_w), lambda hp, ci: (ci, OFF_VR // v_w + hp)),
                  pl.BlockSpec((RET_CHUNK, v_w), lambda hp, ci: (ci, OFF_GR // v_w + hp)),
                  tbl, tbl, tbl,
                  pl.BlockSpec((2 * pairs, RET_CHUNK, RET_CHUNK), lambda hp, ci: (hp, 0, 0)),
                  pl.BlockSpec((pairs, RET_CHUNK, LANES), lambda hp, ci: (hp, 0, 0)),
                  pl.BlockSpec((pairs, RET_CHUNK, LANES), lambda hp, ci: (hp, 0, 0)),
                  pl.BlockSpec((2 * pairs, 1, RET_V_DIM), lambda hp, ci: (hp, 0, 0)),
                  pl.BlockSpec((1, v_w), lambda hp, ci: (0, hp))],
        out_specs=pl.BlockSpec((RET_CHUNK, v_w), lambda hp, ci: (ci, hp)),
        out_shape=jax.ShapeDtypeStruct((t, RET_V_WIDTH), BF16),
        scratch_shapes=[pltpu.VMEM((2 * pairs, LANES, RET_V_DIM), F32)],
        compiler_params=_params(2),
        name="retention",
    )(proj, proj, proj, proj, c, s1, s2, dec, kdec, qdec, cdec, gain)


def _pool_kernel(cur_ref, prev_ref, w_ref, scale_ref, o_ref, buf_a, buf_b, *, tb):
    i = pl.program_id(0)
    g = pl.program_id(1)
    x = cur_ref[...]
    buf_a[0:POOL_HALO, :] = jnp.where(i > 0, prev_ref[...], 0.0)
    buf_a[POOL_HALO:, :] = x
    rows = tb + POOL_HALO
    pos = (i * tb + 1 + lax.broadcasted_iota(jnp.int32, (tb, 1), 0)).astype(F32)
    for gi, w in enumerate(POOL_WINDOWS):
        @pl.when(g == gi)
        def _(w=w):
            src, dst = buf_a, buf_b
            k, lo = 1, SUBLANES
            while k < w:
                dst[lo:rows, :] = src[lo:rows, :] + src[lo - k:rows - k, :]
                src, dst = dst, src
                k, lo = 2 * k, lo + SUBLANES
            tot = src[POOL_HALO:rows, :]
            d = tot / jnp.minimum(pos, float(w)) - x
            y = _dot(d.astype(BF16), w_ref[0])
            o_ref[...] = (y * scale_ref[...]).astype(o_ref.dtype)


def _pool(proj, w_pool, layer, scale, tb=1024):
    t = proj.shape[0]
    gd = POOL_GROUP_DIM
    assert SUBLANES * (max(POOL_WINDOWS).bit_length() - 1) <= POOL_HALO
    return pl.pallas_call(
        functools.partial(_pool_kernel, tb=tb),
        grid=(t // tb, POOL_GROUPS),
        in_specs=[pl.BlockSpec((tb, gd), lambda i, g: (i, OFF_POOL // gd + g)),
                  pl.BlockSpec((POOL_HALO, gd),
                               lambda i, g: (jnp.maximum(i * (tb // POOL_HALO) - 1, 0), OFF_POOL // gd + g)),
                  pl.BlockSpec((None, 1, gd, gd), lambda i, g: (layer, g, 0, 0)),
                  pl.BlockSpec((1, gd), lambda i, g: (0, g))],
        out_specs=pl.BlockSpec((tb, gd), lambda i, g: (i, g)),
        out_shape=jax.ShapeDtypeStruct((t, POOL_WIDTH), BF16),
        scratch_shapes=[pltpu.VMEM((tb + POOL_HALO, gd), F32), pltpu.VMEM((tb + POOL_HALO, gd), F32)],
        compiler_params=_params(2),
        name="pool",
    )(proj, proj, w_pool, scale)


def _merge_kernel(ya_ref, yr_ref, yp_ref, w_ref, ga_ref, gr_ref, gp_ref, o_ref):
    acc = ga_ref[...] * _dot(ya_ref[...], w_ref[0])
    acc = acc + gr_ref[...] * _dot(yr_ref[...], w_ref[1])
    acc = acc + gp_ref[...] * _dot(yp_ref[...], w_ref[2])
    o_ref[...] = acc.astype(o_ref.dtype)


def _merge(ya, yr, yp, w_branch, layer, gates, tm=1024, tn=512):
    t = ya.shape[0]
    d = w_branch.shape[3]
    br = pl.BlockSpec((tm, BRANCH_WIDTH), lambda i, j: (i, 0))

    def gate(nb):
        return pl.BlockSpec((tm, tn), lambda i, j: (i, (nb * d) // tn + j))

    return pl.pallas_call(
        _merge_kernel,
        grid=(t // tm, d // tn),
        in_specs=[br, br, br,
                  pl.BlockSpec((None, N_BRANCH, BRANCH_WIDTH, tn), lambda i, j: (layer, 0, 0, j)),
                  gate(0), gate(1), gate(2)],
        out_specs=pl.BlockSpec((tm, tn), lambda i, j: (i, j)),
        out_shape=jax.ShapeDtypeStruct((t, d), BF16),
        compiler_params=_params(2),
        name="merge",
    )(ya, yr, yp, w_branch, gates, gates, gates)


def _outproj_ln_kernel(m_ref, w_ref, x_ref, g_ref, b_ref, o_ref, *, alpha):
    y = _dot(m_ref[...], w_ref[...])
    o_ref[...] = _layer_norm(alpha * x_ref[...] + y, g_ref[...], b_ref[...])


def _outproj_ln(merged, w_out, layer, x, g, b, alpha, tm=512):
    t, d = x.shape
    row = pl.BlockSpec((tm, d), lambda i: (i, 0))
    vec = pl.BlockSpec((1, d), lambda i: (0, 0))
    return pl.pallas_call(
        functools.partial(_outproj_ln_kernel, alpha=alpha),
        grid=(t // tm,),
        in_specs=[row, pl.BlockSpec((None, d, d), lambda i: (layer, 0, 0)), row, vec, vec],
        out_specs=row,
        out_shape=jax.ShapeDtypeStruct((t, d), F32),
        compiler_params=_params(1),
        name="outproj_ln",
    )(merged, w_out, x, g, b)


def _router_kernel(x_ref, wh_ref, wl_ref, b_ref, e_ref, p_ref):
    x = x_ref[...]
    xh, xl = _split_bf16(x)
    z = _dot(xh, wh_ref[...]) + (_dot(xh, wl_ref[...]) + _dot(xl, wh_ref[...])) + b_ref[...]
    lane = lax.broadcasted_iota(jnp.int32, z.shape, 1)
    big = jnp.int32(ROUTER_LANES)

    def masked_softmax(mask):
        zm = jnp.where(mask, z, NEG)
        e = jnp.where(mask, jnp.exp(zm - jnp.max(zm, axis=-1, keepdims=True)), 0.0)
        return e / jnp.sum(e, axis=-1, keepdims=True)

    def top1(p, mask):
        pm = jnp.where(mask, p, -1.0)
        top = jnp.max(pm, axis=-1, keepdims=True)
        idx = jnp.min(jnp.where(pm == top, lane, big), axis=-1, keepdims=True)
        return top, idx

    in_groups = lane < N_GROUPS
    p1 = masked_softmax(in_groups)
    g_top, g_idx = top1(p1, in_groups)
    lo = N_GROUPS + g_idx * EXPERTS_PER_GROUP
    in_group = (lane >= lo) & (lane < lo + EXPERTS_PER_GROUP)
    p2 = masked_softmax(in_group)
    e1, i1 = top1(p2, in_group)
    e2, i2 = top1(p2, in_group & (lane != i1))
    denom = e1 + e2
    w1 = g_top * e1 / denom
    w2 = g_top * e2 / denom
    e_ref[...] = jnp.where(lane == 0, i1 - N_GROUPS, jnp.where(lane == 1, i2 - N_GROUPS, 0))
    p_ref[...] = jnp.where(lane == 0, w1, jnp.where(lane == 1, w2, 0.0))


def _router(x, w_hi, w_lo, bias, tm=512):
    t, d = x.shape
    row = pl.BlockSpec((tm, d), lambda i: (i, 0))
    wsp = pl.BlockSpec((d, ROUTER_LANES), lambda i: (0, 0))
    out = pl.BlockSpec((tm, ROUTER_LANES), lambda i: (i, 0))
    return pl.pallas_call(
        _router_kernel,
        grid=(t // tm,),
        in_specs=[row, wsp, wsp, pl.BlockSpec((1, ROUTER_LANES), lambda i: (0, 0))],
        out_specs=[out, out],
        out_shape=[jax.ShapeDtypeStruct((t, ROUTER_LANES), jnp.int32),
                   jax.ShapeDtypeStruct((t, ROUTER_LANES), F32)],
        compiler_params=_params(1),
        name="router",
    )(x, w_hi, w_lo, bias)


WAIT_CHUNKS = (64, 8, 1)


def _start_rows(n, copy):
    for r in range(MOE_BLOCK):
        @pl.when(r < n)
        def _(r=r):
            copy(r).start()


def _wait_rows(n, span):
    rem = n
    for c in WAIT_CHUNKS:
        cnt = lax.shift_right_logical(rem, c.bit_length() - 1)

        def one(_, carry, c=c):
            span(c).wait()
            return carry

        lax.fori_loop(0, cnt, one, 0)
        rem = rem & (c - 1)


def _experts_kernel(be_ref, base_ref, nvalid_ref, src_ref, dst_ref, nused_ref,
                    x_hbm, wg_ref, wu_ref, wd_ref, y_hbm,
                    xbuf, ybuf, wgb, wub, wdb, sem_in, sem_out):
    b = pl.program_id(0)
    n_used = nused_ref[0]
    slot = b & 1

    def start_gather(bb, sl):
        base = base_ref[bb]
        _start_rows(nvalid_ref[bb], lambda r: pltpu.make_async_copy(
            x_hbm.at[pl.ds(src_ref[base + r], 1), :], xbuf.at[sl, pl.ds(r, 1), :], sem_in.at[sl]))

    def wait_gather(bb, sl):
        _wait_rows(nvalid_ref[bb], lambda c: pltpu.make_async_copy(
            x_hbm.at[pl.ds(0, c), :], xbuf.at[sl, pl.ds(0, c), :], sem_in.at[sl]))

    def start_scatter(bb, sl):
        base = base_ref[bb]
        _start_rows(nvalid_ref[bb], lambda r: pltpu.make_async_copy(
            ybuf.at[sl, pl.ds(r, 1), :], y_hbm.at[pl.ds(dst_ref[base + r], 1), :], sem_out.at[sl]))

    def wait_scatter(bb, sl):
        _wait_rows(nvalid_ref[bb], lambda c: pltpu.make_async_copy(
            ybuf.at[sl, pl.ds(0, c), :], y_hbm.at[pl.ds(0, c), :], sem_out.at[sl]))

    @pl.when(b == 0)
    def _():
        xbuf[...] = jnp.zeros_like(xbuf)
        start_gather(0, 0)

    @pl.when(b < n_used)
    def _():
        @pl.when(b + 1 < n_used)
        def _():
            start_gather(b + 1, 1 - slot)

        new_expert = (b == 0) | (be_ref[b] != be_ref[jnp.maximum(b - 1, 0)])

        @pl.when(new_expert)
        def _():
            wgb[...] = wg_ref[...].astype(BF16)
            wub[...] = wu_ref[...].astype(BF16)
            wdb[...] = wd_ref[...].astype(BF16)

        wait_gather(b, slot)

        @pl.when(b >= 2)
        def _():
            wait_scatter(b - 2, slot)

        x = xbuf[slot].astype(BF16)
        gate = _dot(x, wgb[...])
        up = _dot(x, wub[...])
        hid = (gate * jax.nn.sigmoid(gate) * up).astype(BF16)
        ybuf[slot] = _dot(hid, wdb[...])
        start_scatter(b, slot)

        @pl.when(b == n_used - 1)
        def _():
            @pl.when(b >= 1)
            def _():
                wait_scatter(b - 1, 1 - slot)

            wait_scatter(b, slot)


def _experts(x, wg, wu, wd, layer, plan):
    t, d = x.shape
    block_expert, base, nvalid, src, dst, n_used = plan
    n_blocks = block_expert.shape[0]

    def wspec(shape):
        return pl.BlockSpec((None, None) + shape, lambda b, be, *_: (layer, be[b], 0, 0))

    grid_spec = pltpu.PrefetchScalarGridSpec(
        num_scalar_prefetch=6,
        grid=(n_blocks,),
        in_specs=[pl.BlockSpec(memory_space=pl.ANY),
                  wspec((d, D_EXPERT)), wspec((d, D_EXPERT)), wspec((D_EXPERT, d))],
        out_specs=pl.BlockSpec(memory_space=pl.ANY),
        scratch_shapes=[pltpu.VMEM((2, MOE_BLOCK, d), F32),
                        pltpu.VMEM((2, MOE_BLOCK, d), F32),
                        pltpu.VMEM((d, D_EXPERT), BF16),
                        pltpu.VMEM((d, D_EXPERT), BF16),
                        pltpu.VMEM((D_EXPERT, d), BF16),
                        pltpu.SemaphoreType.DMA((2,)),
                        pltpu.SemaphoreType.DMA((2,))],
    )
    y = pl.pallas_call(
        _experts_kernel,
        grid_spec=grid_spec,
        out_shape=jax.ShapeDtypeStruct((TOP_K * t, d), F32),
        compiler_params=_params(1),
        name="experts",
    )(block_expert, base, nvalid, src, dst, n_used, x, wg, wu, wd)
    return y.reshape(TOP_K, t, d)


def _combine_ln_kernel(x_ref, y0_ref, y1_ref, p_ref, g_ref, b_ref, o_ref, ob_ref, *, alpha):
    p = p_ref[...]
    y = p[:, 0:1] * y0_ref[...] + p[:, 1:2] * y1_ref[...]
    out = _layer_norm(alpha * x_ref[...] + y, g_ref[...], b_ref[...])
    o_ref[...] = out
    ob_ref[...] = out.astype(BF16)


def _combine_ln(x, ycomb, probs, g, b, alpha, tm=512):
    t, d = x.shape
    row = pl.BlockSpec((tm, d), lambda i: (i, 0))
    vec = pl.BlockSpec((1, d), lambda i: (0, 0))
    return pl.pallas_call(
        functools.partial(_combine_ln_kernel, alpha=alpha),
        grid=(t // tm,),
        in_specs=[row,
                  pl.BlockSpec((None, tm, d), lambda i: (0, i, 0)),
                  pl.BlockSpec((None, tm, d), lambda i: (1, i, 0)),
                  pl.BlockSpec((tm, ROUTER_LANES), lambda i: (i, 0)), vec, vec],
        out_specs=[row, row],
        out_shape=[jax.ShapeDtypeStruct((t, d), F32), jax.ShapeDtypeStruct((t, d), BF16)],
        compiler_params=_params(1),
        name="combine_ln",
    )(x, ycomb, ycomb, probs, g, b)


def _rope_tables(t, rot_dim, theta, head_dim):
    half = rot_dim // 2
    inv_freq = 1.0 / (theta ** (jnp.arange(0, rot_dim, 2, dtype=F32) / rot_dim))
    ang = jnp.arange(t, dtype=F32)[:, None] * inv_freq[None, :]
    cos, sin = jnp.cos(ang), jnp.sin(ang)
    rest = head_dim - rot_dim
    ones, zeros = jnp.ones((t, rest), F32), jnp.zeros((t, rest), F32)
    zh = jnp.zeros((t, half), F32)
    c = jnp.concatenate([cos, cos, ones], axis=1)
    s1 = jnp.concatenate([zh, sin, zeros], axis=1)
    s2 = jnp.concatenate([-sin, zh, zeros], axis=1)
    rep = LANES // head_dim
    return tuple(jnp.tile(a, (1, rep)) for a in (c, s1, s2))


def _retention_tables():
    h, c = RET_HEADS, RET_CHUNK
    log_gamma = jnp.log1p(-jnp.exp2(-5.0 - jnp.arange(h, dtype=F32)))
    pos = jnp.arange(c, dtype=F32)
    diff = pos[:, None] - pos[None, :]
    decay = jnp.where(diff >= 0, jnp.exp(log_gamma[:, None, None] * jnp.maximum(diff, 0.0)), 0.0)
    k_decay = jnp.exp(log_gamma[:, None] * (c - 1.0 - pos)[None, :])
    q_decay = jnp.exp(log_gamma[:, None] * (pos + 1.0)[None, :])
    chunk_decay = jnp.exp(log_gamma * c)

    def pair_lanes(a):
        a = a.reshape(h // 2, 2, c).transpose(0, 2, 1)
        return jnp.repeat(a, RET_QK_DIM, axis=2)

    cdec = jnp.broadcast_to(chunk_decay[:, None, None], (h, 1, RET_V_DIM))
    return decay, pair_lanes(k_decay), pair_lanes(q_decay), cdec


def _dispatch_plan(expert, t):
    a = t * TOP_K
    flat_e = expert.reshape(a)
    order = jnp.argsort(flat_e).astype(jnp.int32)
    counts = jnp.sum(flat_e[None, :] == jnp.arange(N_EXPERTS, dtype=flat_e.dtype)[:, None],
                     axis=1, dtype=jnp.int32)
    start = jnp.cumsum(counts) - counts
    blocks = (counts + MOE_BLOCK - 1) // MOE_BLOCK
    blk_end = jnp.cumsum(blocks)
    n_blocks = -(-a // MOE_BLOCK) + N_EXPERTS
    b = jnp.arange(n_blocks, dtype=jnp.int32)
    block_expert = jnp.minimum(jnp.sum(blk_end[None, :] <= b[:, None], axis=1, dtype=jnp.int32),
                               N_EXPERTS - 1)
    first = (b - (blk_end - blocks)[block_expert]) * MOE_BLOCK
    base = start[block_expert] + first
    nvalid = jnp.clip(counts[block_expert] - first, 0, MOE_BLOCK)
    nvalid = jnp.where(b < blk_end[-1], nvalid, 0)
    n_used = blk_end[-1:].astype(jnp.int32)
    order = jnp.pad(order, (0, MOE_BLOCK))
    src = order // TOP_K
    dst = (order % TOP_K) * t + src
    return block_expert, base.astype(jnp.int32), nvalid.astype(jnp.int32), src, dst, n_used


def _token_mixer_ln(h, hb, w_in, ret_gain, w_pool, pool_scale, w_branch, w_out, layer, ln_g, ln_b,
                    rope_a, rope_r, ret_tables, alpha):
    proj = _inproj(hb, w_in, layer, 0, OFF_GATE)
    gates = _inproj(hb, w_in, layer, OFF_GATE, N_IN - OFF_GATE, gate=True)
    qt, kk, vt = _moba_prep(proj, rope_a)
    y_a = _moba_attn(qt, kk, vt)
    y_r = _retention(proj, rope_r, ret_tables, ret_gain[None, :])
    y_p = _pool(proj, w_pool, layer, pool_scale[None, :])
    merged = _merge(y_a, y_r, y_p, w_branch, layer, gates)
    return _outproj_ln(merged, w_out, layer, h, ln_g[None, :], ln_b[None, :], alpha)


def _moe_ln(h, wr_hi, wr_lo, b_router, w_gate, w_up, w_down, layer, ln_g, ln_b, alpha):
    e_pad, p_pad = _router(h, wr_hi, wr_lo, b_router)
    plan = _dispatch_plan(e_pad[:, :TOP_K], h.shape[0])
    ycomb = _experts(h, w_gate, w_up, w_down, layer, plan)
    return _combine_ln(h, ycomb, p_pad, ln_g[None, :], ln_b[None, :], alpha)


def kernel(x, w_in, ret_gain, w_pool, pool_scale, w_branch, w_out, ln1_g, ln1_b, w_r1, b_r1, w_r2, b_r2, w_e_gate, w_e_up, w_e_down, ln2_g, ln2_b):
    bsz, seq, d = x.shape
    depth = w_in.shape[0]
    assert bsz == 1 and d == D_MODEL and seq % 1024 == 0
    t = seq
    alpha = float((2 * depth) ** 0.25)

    rope_a = _rope_tables(t, ROPE_DIM, ROPE_THETA, ATT_HEAD_DIM)
    rope_r = _rope_tables(t, RET_QK_DIM, RET_ROPE_THETA, RET_QK_DIM)
    ret_tables = _retention_tables()

    w_router = jnp.concatenate([w_r1, w_r2], axis=2)
    w_router = jnp.pad(w_router, ((0, 0), (0, 0), (0, ROUTER_LANES - w_router.shape[2])))
    wr_hi = w_router.astype(BF16)
    wr_lo = (w_router - wr_hi.astype(F32)).astype(BF16)
    b_router = jnp.pad(jnp.concatenate([b_r1, b_r2], axis=1),
                       ((0, 0), (0, ROUTER_LANES - N_GROUPS - N_EXPERTS)))[:, None, :]

    w_pool_b, w_branch_b, w_out_b = w_pool.astype(BF16), w_branch.astype(BF16), w_out.astype(BF16)
    h = x.reshape(t, d)
    hb = h.astype(BF16)
    for l in range(depth):
        h = _token_mixer_ln(h, hb, w_in, ret_gain[l], w_pool_b, pool_scale[l], w_branch_b, w_out_b, l,
                            ln1_g[l], ln1_b[l], rope_a, rope_r, ret_tables, alpha)
        h, hb = _moe_ln(h, wr_hi[l], wr_lo[l], b_router[l], w_e_gate, w_e_up, w_e_down, l,
                        ln2_g[l], ln2_b[l], alpha)
    return h.reshape(bsz, seq, d)
```

```python
import functools

import numpy as np
import jax
import jax.numpy as jnp
from jax import lax
from jax.experimental import pallas as pl
from jax.experimental.pallas import tpu as pltpu

D_MODEL = 2048

ATT_HEADS = 8
ATT_HEAD_DIM = 128
ATT_WIDTH = ATT_HEADS * ATT_HEAD_DIM
ROPE_DIM = ATT_HEAD_DIM // 4
ROPE_THETA = 500000.0
MOBA_BLOCK = 256
MOBA_TOPK = 3

RET_HEADS = 8
RET_QK_DIM = 64
RET_V_DIM = 128
RET_QK_WIDTH = RET_HEADS * RET_QK_DIM
RET_V_WIDTH = RET_HEADS * RET_V_DIM
RET_CHUNK = 256
RET_ROPE_THETA = 10000.0

POOL_WINDOWS = (2, 4, 8, 16)
POOL_GROUPS = 4
POOL_GROUP_DIM = 256
POOL_WIDTH = POOL_GROUPS * POOL_GROUP_DIM
POOL_HALO = 32

N_BRANCH = 3
BRANCH_WIDTH = 1024

OFF_QA = 0
OFF_KA = OFF_QA + ATT_WIDTH
OFF_VA = OFF_KA + ATT_WIDTH
OFF_QR = OFF_VA + ATT_WIDTH
OFF_KR = OFF_QR + RET_QK_WIDTH
OFF_VR = OFF_KR + RET_QK_WIDTH
OFF_GR = OFF_VR + RET_V_WIDTH
OFF_POOL = OFF_GR + RET_V_WIDTH
OFF_GATE = OFF_POOL + POOL_WIDTH
N_IN = OFF_GATE + N_BRANCH * D_MODEL

N_GROUPS = 4
EXPERTS_PER_GROUP = 8
N_EXPERTS = N_GROUPS * EXPERTS_PER_GROUP
TOP_K = 2
D_EXPERT = 512
MOE_BLOCK = 256
ROUTER_LANES = 128

LN_EPS = 1e-5
NEG = -1e30
TAKEN = -3e38
LOG2_E = 1.4426950408889634

LANES = 128
SUBLANES = 8
VMEM_LIMIT = 56 * 1024 * 1024

F32 = jnp.float32
BF16 = jnp.bfloat16


def _params(n_axes):
    return pltpu.CompilerParams(dimension_semantics=("arbitrary",) * n_axes,
                                vmem_limit_bytes=VMEM_LIMIT)


def _dot(a, b):
    return jnp.dot(a, b, preferred_element_type=F32)


def _dot_nt(a, b):
    return lax.dot_general(a, b, (((1,), (1,)), ((), ())), preferred_element_type=F32)


def _dot_tn(a, b):
    return lax.dot_general(a, b, (((0,), (0,)), ((), ())), preferred_element_type=F32)


def _split_bf16(a):
    hi = a.astype(BF16)
    lo = (a - hi.astype(F32)).astype(BF16)
    return hi, lo


def _dot_nt_f32ish(a, b):
    ah, al = _split_bf16(a)
    bh, bl = _split_bf16(b)
    return _dot_nt(ah, bh) + (_dot_nt(ah, bl) + _dot_nt(al, bh))


def _rotate(x, c, s1, s2, shift):
    return x * c + pltpu.roll(x, shift, 1) * s1 + pltpu.roll(x, LANES - shift, 1) * s2


def _layer_norm(v, g, b):
    mu = jnp.mean(v, axis=-1, keepdims=True)
    d = v - mu
    var = jnp.mean(d * d, axis=-1, keepdims=True)
    return d * lax.rsqrt(var + LN_EPS) * g + b


def _inproj_kernel(x_ref, w_ref, o_ref, wb_ref, *, gate):
    @pl.when(pl.program_id(1) == 0)
    def _():
        wb_ref[...] = w_ref[...].astype(BF16)

    y = _dot(x_ref[...], wb_ref[...])
    o_ref[...] = jax.nn.sigmoid(y).astype(o_ref.dtype) if gate else y


def _inproj(xb, w, layer, col0, ncols, gate=False, tm=1024, tn=1024):
    t, d = xb.shape
    return pl.pallas_call(
        functools.partial(_inproj_kernel, gate=gate),
        grid=(ncols // tn, t // tm),
        in_specs=[pl.BlockSpec((tm, d), lambda j, i: (i, 0)),
                  pl.BlockSpec((None, d, tn), lambda j, i: (layer, 0, col0 // tn + j))],
        out_specs=pl.BlockSpec((tm, tn), lambda j, i: (i, j)),
        out_shape=jax.ShapeDtypeStruct((t, ncols), BF16 if gate else F32),
        scratch_shapes=[pltpu.VMEM((d, tn), BF16)],
        compiler_params=_params(2),
        name="inproj_gates" if gate else "inproj",
    )(xb, w)


ATT_KDIM = 2 * ATT_HEAD_DIM
ATT_VROWS = ATT_HEAD_DIM + 16


def _moba_prep_kernel(q_ref, k_ref, v_ref, c_ref, s1_ref, s2_ref, qt_ref, ko_ref, vt_ref, km_ref):
    i = pl.program_id(0)
    nblk = km_ref.shape[0]

    @pl.when(i == 0)
    def _():
        km_ref[...] = jnp.zeros_like(km_ref)

    c, s1, s2 = c_ref[...], s1_ref[...], s2_ref[...]
    blk = lax.broadcasted_iota(jnp.int32, (nblk, MOBA_BLOCK), 0)
    blkf = blk.astype(F32)
    scale = ATT_HEAD_DIM ** -0.5 * LOG2_E
    lane = lax.broadcasted_iota(jnp.int32, (MOBA_BLOCK, ATT_KDIM - ATT_HEAD_DIM), 1)
    this_block = jnp.where(lane == i, 1.0, 0.0).astype(BF16)
    row = lax.broadcasted_iota(jnp.int32, (ATT_VROWS - ATT_HEAD_DIM, MOBA_BLOCK), 0)
    ones_row = jnp.where(row == 0, 1.0, 0.0).astype(BF16)
    pad_rows = ATT_KDIM - ATT_HEAD_DIM - nblk
    eye = jnp.where(lax.broadcasted_iota(jnp.int32, (ATT_HEAD_DIM, ATT_HEAD_DIM), 0)
                    == lax.broadcasted_iota(jnp.int32, (ATT_HEAD_DIM, ATT_HEAD_DIM), 1), 1.0, 0.0).astype(BF16)
    for h in range(ATT_HEADS):
        sl = slice(h * ATT_HEAD_DIM, (h + 1) * ATT_HEAD_DIM)
        qr = _rotate(q_ref[:, sl], c, s1, s2, ROPE_DIM // 2)
        kr = _rotate(k_ref[:, sl], c, s1, s2, ROPE_DIM // 2)
        ko_ref[h, 0, :, :ATT_HEAD_DIM] = kr.astype(BF16)
        ko_ref[h, 0, :, ATT_HEAD_DIM:] = this_block
        km_blk = lax.broadcasted_iota(jnp.int32, (nblk, ATT_HEAD_DIM), 0)
        km_ref[:, sl] = jnp.where(km_blk == i, jnp.mean(kr, axis=0, keepdims=True), km_ref[:, sl])
        qt_ref[h, 0, :ATT_HEAD_DIM, :] = _dot_nt(eye, (qr * scale).astype(BF16)).astype(BF16)
        vt_ref[h, 0, :ATT_HEAD_DIM, :] = _dot_nt(eye, v_ref[:, sl].astype(BF16)).astype(BF16)
        vt_ref[h, 0, ATT_HEAD_DIM:, :] = ones_row
        gate = _dot_nt_f32ish(km_ref[:, sl], qr)
        gate = jnp.where(blk < i, gate, NEG)
        for _ in range(MOBA_TOPK):
            top = jnp.max(gate, axis=0, keepdims=True)
            first = jnp.min(jnp.where(gate == top, blkf, float(nblk)), axis=0, keepdims=True)
            gate = jnp.where(blkf == first, TAKEN, gate)
        bias = jnp.where(gate == TAKEN, jnp.where(blk < i, 0.0, NEG), NEG)
        qt_ref[h, 0, ATT_HEAD_DIM:ATT_HEAD_DIM + nblk, :] = bias.astype(BF16)
        if pad_rows:
            qt_ref[h, 0, ATT_HEAD_DIM + nblk:, :] = jnp.zeros((pad_rows, MOBA_BLOCK), BF16)


def _moba_prep(proj, rope):
    t = proj.shape[0]
    nblk = t // MOBA_BLOCK
    assert ATT_HEAD_DIM + nblk <= ATT_KDIM and nblk % 16 == 0
    c, s1, s2 = rope
    tbl = pl.BlockSpec((MOBA_BLOCK, LANES), lambda i: (i, 0))
    hshape = (ATT_HEADS, nblk)
    return pl.pallas_call(
        _moba_prep_kernel,
        grid=(nblk,),
        in_specs=[pl.BlockSpec((MOBA_BLOCK, ATT_WIDTH), lambda i: (i, OFF_QA // ATT_WIDTH)),
                  pl.BlockSpec((MOBA_BLOCK, ATT_WIDTH), lambda i: (i, OFF_KA // ATT_WIDTH)),
                  pl.BlockSpec((MOBA_BLOCK, ATT_WIDTH), lambda i: (i, OFF_VA // ATT_WIDTH)),
                  tbl, tbl, tbl],
        out_specs=[pl.BlockSpec((ATT_HEADS, 1, ATT_KDIM, MOBA_BLOCK), lambda i: (0, i, 0, 0)),
                   pl.BlockSpec((ATT_HEADS, 1, MOBA_BLOCK, ATT_KDIM), lambda i: (0, i, 0, 0)),
                   pl.BlockSpec((ATT_HEADS, 1, ATT_VROWS, MOBA_BLOCK), lambda i: (0, i, 0, 0))],
        out_shape=[jax.ShapeDtypeStruct(hshape + (ATT_KDIM, MOBA_BLOCK), BF16),
                   jax.ShapeDtypeStruct(hshape + (MOBA_BLOCK, ATT_KDIM), BF16),
                   jax.ShapeDtypeStruct(hshape + (ATT_VROWS, MOBA_BLOCK), BF16)],
        scratch_shapes=[pltpu.VMEM((nblk, ATT_WIDTH), F32)],
        compiler_params=_params(1),
        name="moba_prep",
    )(proj, proj, proj, c, s1, s2)


def _moba_attn_kernel(qt_ref, k_ref, vt_ref, o_ref, s_ref, p_ref, a_ref, acc_ref):
    i = pl.program_id(1)
    heads = qt_ref.shape[0]
    nblk = k_ref.shape[1]
    shape = (MOBA_BLOCK, MOBA_BLOCK)
    causal = lax.broadcasted_iota(jnp.int32, shape, 0) <= lax.broadcasted_iota(jnp.int32, shape, 1)

    ms = []
    for h in range(heads):
        s = _dot(k_ref[h, i, :, :ATT_HEAD_DIM], qt_ref[h, 0, :ATT_HEAD_DIM, :])
        s = jnp.where(causal, s, NEG)
        m = jnp.max(s, axis=0, keepdims=True)
        ms.append(m)
        acc_ref[h] = _dot(vt_ref[h, i], jnp.exp2(s - m).astype(BF16))
        p_ref[1, h] = jnp.zeros(shape, BF16)
        a_ref[1, h] = jnp.ones((1, MOBA_BLOCK), F32)

    def scores(j, slot):
        jc = jnp.minimum(j, nblk - 1)
        for h in range(heads):
            s_ref[slot, h] = _dot(k_ref[h, jc], qt_ref[h, 0])

    def pv(j, slot):
        jc = jnp.maximum(j, 0)
        for h in range(heads):
            acc_ref[h] = a_ref[slot, h] * acc_ref[h] + _dot(vt_ref[h, jc], p_ref[slot, h])

    def softmax(slot, ms):
        out = []
        for h in range(heads):
            s = s_ref[slot, h]
            m_new = jnp.maximum(ms[h], jnp.max(s, axis=0, keepdims=True))
            a_ref[slot, h] = jnp.exp2(ms[h] - m_new)
            p_ref[slot, h] = jnp.exp2(s - m_new).astype(BF16)
            out.append(m_new)
        return out

    scores(0, 0)

    def body(jj, ms):
        j = 2 * jj
        scores(j + 1, 1)
        pv(j - 1, 1)
        ms = softmax(0, ms)
        scores(j + 2, 0)
        pv(j, 0)
        return softmax(1, ms)

    lax.fori_loop(0, (i + 1) // 2, body, ms)
    pv(i - 1, 1)
    for h in range(heads):
        acc = acc_ref[h]
        out = acc[:ATT_HEAD_DIM] / acc[ATT_HEAD_DIM:ATT_HEAD_DIM + 1]
        o_ref[:, h * ATT_HEAD_DIM:(h + 1) * ATT_HEAD_DIM] = out.T.astype(o_ref.dtype)


def _moba_attn(qt, k, vt, heads=2):
    nblk = qt.shape[1]
    t = nblk * MOBA_BLOCK
    blk2 = (MOBA_BLOCK, MOBA_BLOCK)
    return pl.pallas_call(
        _moba_attn_kernel,
        grid=(ATT_HEADS // heads, nblk),
        in_specs=[pl.BlockSpec((heads, 1, ATT_KDIM, MOBA_BLOCK), lambda h, i: (h, i, 0, 0)),
                  pl.BlockSpec((heads, nblk, MOBA_BLOCK, ATT_KDIM), lambda h, i: (h, 0, 0, 0)),
                  pl.BlockSpec((heads, nblk, ATT_VROWS, MOBA_BLOCK), lambda h, i: (h, 0, 0, 0))],
        out_specs=pl.BlockSpec((MOBA_BLOCK, heads * ATT_HEAD_DIM), lambda h, i: (i, h)),
        out_shape=jax.ShapeDtypeStruct((t, ATT_WIDTH), BF16),
        scratch_shapes=[pltpu.VMEM((2, heads) + blk2, F32),
                        pltpu.VMEM((2, heads) + blk2, BF16),
                        pltpu.VMEM((2, heads, 1, MOBA_BLOCK), F32),
                        pltpu.VMEM((heads, ATT_VROWS, MOBA_BLOCK), F32)],
        compiler_params=_params(2),
        name="moba_attn",
    )(qt, k, vt)


def _retention_kernel(q_ref, k_ref, v_ref, g_ref, c_ref, s1_ref, s2_ref,
                      dec_ref, kdec_ref, qdec_ref, cdec_ref, gain_ref, o_ref, state_ref):
    @pl.when(pl.program_id(1) == 0)
    def _():
        state_ref[...] = jnp.zeros_like(state_ref)

    pairs = kdec_ref.shape[0]
    heads = 2 * pairs
    c, s1, s2 = c_ref[...], s1_ref[...], s2_ref[...]
    lane = lax.broadcasted_iota(jnp.int32, (RET_CHUNK, LANES), 1)

    def vslice(h):
        return slice(h * RET_V_DIM, (h + 1) * RET_V_DIM)

    qm, qdm, kb, kdt = [], [], [], []
    for p in range(pairs):
        psl = slice(p * LANES, (p + 1) * LANES)
        q = _rotate(q_ref[:, psl], c, s1, s2, RET_QK_DIM // 2)
        k = _rotate(k_ref[:, psl], c, s1, s2, RET_QK_DIM // 2) * (RET_QK_DIM ** -0.5)
        qd = q * qdec_ref[p]
        kd = k * kdec_ref[p]
        kb.append(k.astype(BF16))
        for hh in range(2):
            mine = (lane < RET_QK_DIM) if hh == 0 else (lane >= RET_QK_DIM)
            qm.append(jnp.where(mine, q, 0.0).astype(BF16))
            qdm.append(jnp.where(mine, qd, 0.0).astype(BF16))
            kdt.append(jnp.where(mine, kd, 0.0).astype(BF16))
    vb = [v_ref[:, vslice(h)].astype(BF16) for h in range(heads)]
    inner = [_dot_nt(qm[h], kb[h // 2]) for h in range(heads)]
    cross = [_dot(qdm[h], state_ref[h].astype(BF16)) for h in range(heads)]
    for h in range(heads):
        state_ref[h] = state_ref[h] * cdec_ref[h] + _dot_tn(kdt[h], vb[h])
    out = [_dot((inner[h] * dec_ref[h]).astype(BF16), vb[h]) + cross[h] for h in range(heads)]
    for h in range(heads):
        mu = jnp.mean(out[h], axis=-1, keepdims=True)
        d = out[h] - mu
        var = jnp.mean(d * d, axis=-1, keepdims=True)
        rn = d * lax.rsqrt(var + LN_EPS) * gain_ref[:, vslice(h)]
        g = g_ref[:, vslice(h)]
        o_ref[:, vslice(h)] = (g * jax.nn.sigmoid(g) * rn).astype(o_ref.dtype)


def _retention(proj, rope, tables, gain, pairs=4):
    t = proj.shape[0]
    n = t // RET_CHUNK
    c, s1, s2 = rope
    dec, kdec, qdec, cdec = tables
    qk_w = pairs * LANES
    v_w = 2 * pairs * RET_V_DIM
    tbl = pl.BlockSpec((RET_CHUNK, LANES), lambda hp, ci: (ci, 0))
    return pl.pallas_call(
        _retention_kernel,
        grid=(RET_HEADS // (2 * pairs), n),
        in_specs=[pl.BlockSpec((RET_CHUNK, qk_w), lambda hp, ci: (ci, OFF_QR // qk_w + hp)),
                  pl.BlockSpec((RET_CHUNK, qk_w), lambda hp, ci: (ci, OFF_KR // qk_w + hp)),
                  pl.BlockSpec((RET_CHUNK, v_w), lambda hp, ci: (ci, OFF_VR // v_w + hp)),
                  pl.BlockSpec((RET_CHUNK, v_w), lambda hp, ci: (ci, OFF_GR // v_w + hp)),
                  tbl, tbl, tbl,
                  pl.BlockSpec((2 * pairs, RET_CHUNK, RET_CHUNK), lambda hp, ci: (hp, 0, 0)),
                  pl.BlockSpec((pairs, RET_CHUNK, LANES), lambda hp, ci: (hp, 0, 0)),
                  pl.BlockSpec((pairs, RET_CHUNK, LANES), lambda hp, ci: (hp, 0, 0)),
                  pl.BlockSpec((2 * pairs, 1, RET_V_DIM), lambda hp, ci: (hp, 0, 0)),
                  pl.BlockSpec((1, v_w), lambda hp, ci: (0, hp))],
        out_specs=pl.BlockSpec((RET_CHUNK, v_w), lambda hp, ci: (ci, hp)),
        out_shape=jax.ShapeDtypeStruct((t, RET_V_WIDTH), BF16),
        scratch_shapes=[pltpu.VMEM((2 * pairs, LANES, RET_V_DIM), F32)],
        compiler_params=_params(2),
        name="retention",
    )(proj, proj, proj, proj, c, s1, s2, dec, kdec, qdec, cdec, gain)


def _pool_kernel(cur_ref, prev_ref, w_ref, scale_ref, o_ref, buf_a, buf_b, *, tb):
    i = pl.program_id(0)
    g = pl.program_id(1)
    x = cur_ref[...]
    buf_a[0:POOL_HALO, :] = jnp.where(i > 0, prev_ref[...], 0.0)
    buf_a[POOL_HALO:, :] = x
    rows = tb + POOL_HALO
    pos = (i * tb + 1 + lax.broadcasted_iota(jnp.int32, (tb, 1), 0)).astype(F32)
    for gi, w in enumerate(POOL_WINDOWS):
        @pl.when(g == gi)
        def _(w=w):
            src, dst = buf_a, buf_b
            k, lo = 1, SUBLANES
            while k < w:
                dst[lo:rows, :] = src[lo:rows, :] + src[lo - k:rows - k, :]
                src, dst = dst, src
                k, lo = 2 * k, lo + SUBLANES
            tot = src[POOL_HALO:rows, :]
            d = tot / jnp.minimum(pos, float(w)) - x
            y = _dot(d.astype(BF16), w_ref[0])
            o_ref[...] = (y * scale_ref[...]).astype(o_ref.dtype)


def _pool(proj, w_pool, layer, scale, tb=1024):
    t = proj.shape[0]
    gd = POOL_GROUP_DIM
    assert SUBLANES * (max(POOL_WINDOWS).bit_length() - 1) <= POOL_HALO
    return pl.pallas_call(
        functools.partial(_pool_kernel, tb=tb),
        grid=(t // tb, POOL_GROUPS),
        in_specs=[pl.BlockSpec((tb, gd), lambda i, g: (i, OFF_POOL // gd + g)),
                  pl.BlockSpec((POOL_HALO, gd),
                               lambda i, g: (jnp.maximum(i * (tb // POOL_HALO) - 1, 0), OFF_POOL // gd + g)),
                  pl.BlockSpec((None, 1, gd, gd), lambda i, g: (layer, g, 0, 0)),
                  pl.BlockSpec((1, gd), lambda i, g: (0, g))],
        out_specs=pl.BlockSpec((tb, gd), lambda i, g: (i, g)),
        out_shape=jax.ShapeDtypeStruct((t, POOL_WIDTH), BF16),
        scratch_shapes=[pltpu.VMEM((tb + POOL_HALO, gd), F32), pltpu.VMEM((tb + POOL_HALO, gd), F32)],
        compiler_params=_params(2),
        name="pool",
    )(proj, proj, w_pool, scale)


def _merge_kernel(ya_ref, yr_ref, yp_ref, w_ref, ga_ref, gr_ref, gp_ref, o_ref):
    acc = ga_ref[...] * _dot(ya_ref[...], w_ref[0])
    acc = acc + gr_ref[...] * _dot(yr_ref[...], w_ref[1])
    acc = acc + gp_ref[...] * _dot(yp_ref[...], w_ref[2])
    o_ref[...] = acc.astype(o_ref.dtype)


def _merge(ya, yr, yp, w_branch, layer, gates, tm=1024, tn=1024):
    t = ya.shape[0]
    d = w_branch.shape[3]
    br = pl.BlockSpec((tm, BRANCH_WIDTH), lambda i, j: (i, 0))

    def gate(nb):
        return pl.BlockSpec((tm, tn), lambda i, j: (i, (nb * d) // tn + j))

    return pl.pallas_call(
        _merge_kernel,
        grid=(t // tm, d // tn),
        in_specs=[br, br, br,
                  pl.BlockSpec((None, N_BRANCH, BRANCH_WIDTH, tn), lambda i, j: (layer, 0, 0, j)),
                  gate(0), gate(1), gate(2)],
        out_specs=pl.BlockSpec((tm, tn), lambda i, j: (i, j)),
        out_shape=jax.ShapeDtypeStruct((t, d), BF16),
        compiler_params=_params(2),
        name="merge",
    )(ya, yr, yp, w_branch, gates, gates, gates)


def _outproj_ln_kernel(m_ref, w_ref, x_ref, g_ref, b_ref, o_ref, *, alpha):
    y = _dot(m_ref[...], w_ref[...])
    o_ref[...] = _layer_norm(alpha * x_ref[...] + y, g_ref[...], b_ref[...])


def _outproj_ln(merged, w_out, layer, x, g, b, alpha, tm=512):
    t, d = x.shape
    row = pl.BlockSpec((tm, d), lambda i: (i, 0))
    vec = pl.BlockSpec((1, d), lambda i: (0, 0))
    return pl.pallas_call(
        functools.partial(_outproj_ln_kernel, alpha=alpha),
        grid=(t // tm,),
        in_specs=[row, pl.BlockSpec((None, d, d), lambda i: (layer, 0, 0)), row, vec, vec],
        out_specs=row,
        out_shape=jax.ShapeDtypeStruct((t, d), F32),
        compiler_params=_params(1),
        name="outproj_ln",
    )(merged, w_out, x, g, b)


def _router_kernel(x_ref, wh_ref, wl_ref, b_ref, e_ref, p_ref):
    x = x_ref[...]
    xh, xl = _split_bf16(x)
    z = _dot(xh, wh_ref[...]) + (_dot(xh, wl_ref[...]) + _dot(xl, wh_ref[...])) + b_ref[...]
    lane = lax.broadcasted_iota(jnp.int32, z.shape, 1)
    big = jnp.int32(ROUTER_LANES)

    def masked_softmax(mask):
        zm = jnp.where(mask, z, NEG)
        e = jnp.where(mask, jnp.exp(zm - jnp.max(zm, axis=-1, keepdims=True)), 0.0)
        return e / jnp.sum(e, axis=-1, keepdims=True)

    def top1(p, mask):
        pm = jnp.where(mask, p, -1.0)
        top = jnp.max(pm, axis=-1, keepdims=True)
        idx = jnp.min(jnp.where(pm == top, lane, big), axis=-1, keepdims=True)
        return top, idx

    in_groups = lane < N_GROUPS
    p1 = masked_softmax(in_groups)
    g_top, g_idx = top1(p1, in_groups)
    lo = N_GROUPS + g_idx * EXPERTS_PER_GROUP
    in_group = (lane >= lo) & (lane < lo + EXPERTS_PER_GROUP)
    p2 = masked_softmax(in_group)
    e1, i1 = top1(p2, in_group)
    e2, i2 = top1(p2, in_group & (lane != i1))
    denom = e1 + e2
    w1 = g_top * e1 / denom
    w2 = g_top * e2 / denom
    e_ref[...] = jnp.where(lane == 0, i1 - N_GROUPS, jnp.where(lane == 1, i2 - N_GROUPS, 0))
    p_ref[...] = jnp.where(lane == 0, w1, jnp.where(lane == 1, w2, 0.0))


def _router(x, w_hi, w_lo, bias, tm=512):
    t, d = x.shape
    row = pl.BlockSpec((tm, d), lambda i: (i, 0))
    wsp = pl.BlockSpec((d, ROUTER_LANES), lambda i: (0, 0))
    out = pl.BlockSpec((tm, ROUTER_LANES), lambda i: (i, 0))
    return pl.pallas_call(
        _router_kernel,
        grid=(t // tm,),
        in_specs=[row, wsp, wsp, pl.BlockSpec((1, ROUTER_LANES), lambda i: (0, 0))],
        out_specs=[out, out],
        out_shape=[jax.ShapeDtypeStruct((t, ROUTER_LANES), jnp.int32),
                   jax.ShapeDtypeStruct((t, ROUTER_LANES), F32)],
        compiler_params=_params(1),
        name="router",
    )(x, w_hi, w_lo, bias)


WAIT_CHUNKS = (64, 8, 1)


def _start_rows(n, copy):
    for r in range(MOE_BLOCK):
        @pl.when(r < n)
        def _(r=r):
            copy(r).start()


def _wait_rows(n, span):
    rem = n
    for c in WAIT_CHUNKS:
        cnt = lax.shift_right_logical(rem, c.bit_length() - 1)

        def one(_, carry, c=c):
            span(c).wait()
            return carry

        lax.fori_loop(0, cnt, one, 0)
        rem = rem & (c - 1)


def _experts_kernel(be_ref, base_ref, nvalid_ref, src_ref, dst_ref, nused_ref,
                    x_hbm, wg_ref, wu_ref, wd_ref, y_hbm,
                    xbuf, ybuf, wgb, wub, wdb, sem_in, sem_out):
    b = pl.program_id(0)
    n_used = nused_ref[0]
    slot = b & 1

    def start_gather(bb, sl):
        base = base_ref[bb]
        _start_rows(nvalid_ref[bb], lambda r: pltpu.make_async_copy(
            x_hbm.at[pl.ds(src_ref[base + r], 1), :], xbuf.at[sl, pl.ds(r, 1), :], sem_in.at[sl]))

    def wait_gather(bb, sl):
        _wait_rows(nvalid_ref[bb], lambda c: pltpu.make_async_copy(
            x_hbm.at[pl.ds(0, c), :], xbuf.at[sl, pl.ds(0, c), :], sem_in.at[sl]))

    def start_scatter(bb, sl):
        base = base_ref[bb]
        _start_rows(nvalid_ref[bb], lambda r: pltpu.make_async_copy(
            ybuf.at[sl, pl.ds(r, 1), :], y_hbm.at[pl.ds(dst_ref[base + r], 1), :], sem_out.at[sl]))

    def wait_scatter(bb, sl):
        _wait_rows(nvalid_ref[bb], lambda c: pltpu.make_async_copy(
            ybuf.at[sl, pl.ds(0, c), :], y_hbm.at[pl.ds(0, c), :], sem_out.at[sl]))

    @pl.when(b == 0)
    def _():
        xbuf[...] = jnp.zeros_like(xbuf)
        start_gather(0, 0)

    @pl.when(b < n_used)
    def _():
        @pl.when(b + 1 < n_used)
        def _():
            start_gather(b + 1, 1 - slot)

        new_expert = (b == 0) | (be_ref[b] != be_ref[jnp.maximum(b - 1, 0)])

        @pl.when(new_expert)
        def _():
            wgb[...] = wg_ref[...].astype(BF16)
            wub[...] = wu_ref[...].astype(BF16)
            wdb[...] = wd_ref[...].astype(BF16)

        wait_gather(b, slot)

        @pl.when(b >= 2)
        def _():
            wait_scatter(b - 2, slot)

        x = xbuf[slot].astype(BF16)
        gate = _dot(x, wgb[...])
        up = _dot(x, wub[...])
        hid = (gate * jax.nn.sigmoid(gate) * up).astype(BF16)
        ybuf[slot] = _dot(hid, wdb[...])
        start_scatter(b, slot)

        @pl.when(b == n_used - 1)
        def _():
            @pl.when(b >= 1)
            def _():
                wait_scatter(b - 1, 1 - slot)

            wait_scatter(b, slot)


def _experts(x, wg, wu, wd, layer, plan):
    t, d = x.shape
    block_expert, base, nvalid, src, dst, n_used = plan
    n_blocks = block_expert.shape[0]

    def wspec(shape):
        return pl.BlockSpec((None, None) + shape, lambda b, be, *_: (layer, be[b], 0, 0))

    grid_spec = pltpu.PrefetchScalarGridSpec(
        num_scalar_prefetch=6,
        grid=(n_blocks,),
        in_specs=[pl.BlockSpec(memory_space=pl.ANY),
                  wspec((d, D_EXPERT)), wspec((d, D_EXPERT)), wspec((D_EXPERT, d))],
        out_specs=pl.BlockSpec(memory_space=pl.ANY),
        scratch_shapes=[pltpu.VMEM((2, MOE_BLOCK, d), F32),
                        pltpu.VMEM((2, MOE_BLOCK, d), F32),
                        pltpu.VMEM((d, D_EXPERT), BF16),
                        pltpu.VMEM((d, D_EXPERT), BF16),
                        pltpu.VMEM((D_EXPERT, d), BF16),
                        pltpu.SemaphoreType.DMA((2,)),
                        pltpu.SemaphoreType.DMA((2,))],
    )
    y = pl.pallas_call(
        _experts_kernel,
        grid_spec=grid_spec,
        out_shape=jax.ShapeDtypeStruct((TOP_K * t, d), F32),
        compiler_params=_params(1),
        name="experts",
    )(block_expert, base, nvalid, src, dst, n_used, x, wg, wu, wd)
    return y.reshape(TOP_K, t, d)


def _combine_ln_kernel(x_ref, y0_ref, y1_ref, p_ref, g_ref, b_ref, o_ref, ob_ref, *, alpha):
    p = p_ref[...]
    y = p[:, 0:1] * y0_ref[...] + p[:, 1:2] * y1_ref[...]
    out = _layer_norm(alpha * x_ref[...] + y, g_ref[...], b_ref[...])
    o_ref[...] = out
    ob_ref[...] = out.astype(BF16)


def _combine_ln(x, ycomb, probs, g, b, alpha, tm=512):
    t, d = x.shape
    row = pl.BlockSpec((tm, d), lambda i: (i, 0))
    vec = pl.BlockSpec((1, d), lambda i: (0, 0))
    return pl.pallas_call(
        functools.partial(_combine_ln_kernel, alpha=alpha),
        grid=(t // tm,),
        in_specs=[row,
                  pl.BlockSpec((None, tm, d), lambda i: (0, i, 0)),
                  pl.BlockSpec((None, tm, d), lambda i: (1, i, 0)),
                  pl.BlockSpec((tm, ROUTER_LANES), lambda i: (i, 0)), vec, vec],
        out_specs=[row, row],
        out_shape=[jax.ShapeDtypeStruct((t, d), F32), jax.ShapeDtypeStruct((t, d), BF16)],
        compiler_params=_params(1),
        name="combine_ln",
    )(x, ycomb, ycomb, probs, g, b)


def _rope_tables(t, rot_dim, theta, head_dim):
    half = rot_dim // 2
    inv_freq = 1.0 / (theta ** (jnp.arange(0, rot_dim, 2, dtype=F32) / rot_dim))
    ang = jnp.arange(t, dtype=F32)[:, None] * inv_freq[None, :]
    cos, sin = jnp.cos(ang), jnp.sin(ang)
    rest = head_dim - rot_dim
    ones, zeros = jnp.ones((t, rest), F32), jnp.zeros((t, rest), F32)
    zh = jnp.zeros((t, half), F32)
    c = jnp.concatenate([cos, cos, ones], axis=1)
    s1 = jnp.concatenate([zh, sin, zeros], axis=1)
    s2 = jnp.concatenate([-sin, zh, zeros], axis=1)
    rep = LANES // head_dim
    return tuple(jnp.tile(a, (1, rep)) for a in (c, s1, s2))


def _retention_tables():
    h, c = RET_HEADS, RET_CHUNK
    log_gamma = jnp.log1p(-jnp.exp2(-5.0 - jnp.arange(h, dtype=F32)))
    pos = jnp.arange(c, dtype=F32)
    diff = pos[:, None] - pos[None, :]
    decay = jnp.where(diff >= 0, jnp.exp(log_gamma[:, None, None] * jnp.maximum(diff, 0.0)), 0.0)
    k_decay = jnp.exp(log_gamma[:, None] * (c - 1.0 - pos)[None, :])
    q_decay = jnp.exp(log_gamma[:, None] * (pos + 1.0)[None, :])
    chunk_decay = jnp.exp(log_gamma * c)

    def pair_lanes(a):
        a = a.reshape(h // 2, 2, c).transpose(0, 2, 1)
        return jnp.repeat(a, RET_QK_DIM, axis=2)

    cdec = jnp.broadcast_to(chunk_decay[:, None, None], (h, 1, RET_V_DIM))
    return decay, pair_lanes(k_decay), pair_lanes(q_decay), cdec


def _dispatch_plan(expert, t):
    a = t * TOP_K
    flat_e = expert.reshape(a)
    order = jnp.argsort(flat_e).astype(jnp.int32)
    counts = jnp.sum(flat_e[None, :] == jnp.arange(N_EXPERTS, dtype=flat_e.dtype)[:, None],
                     axis=1, dtype=jnp.int32)
    start = jnp.cumsum(counts) - counts
    blocks = (counts + MOE_BLOCK - 1) // MOE_BLOCK
    blk_end = jnp.cumsum(blocks)
    n_blocks = -(-a // MOE_BLOCK) + N_EXPERTS
    b = jnp.arange(n_blocks, dtype=jnp.int32)
    block_expert = jnp.minimum(jnp.sum(blk_end[None, :] <= b[:, None], axis=1, dtype=jnp.int32),
                               N_EXPERTS - 1)
    first = (b - (blk_end - blocks)[block_expert]) * MOE_BLOCK
    base = start[block_expert] + first
    nvalid = jnp.clip(counts[block_expert] - first, 0, MOE_BLOCK)
    nvalid = jnp.where(b < blk_end[-1], nvalid, 0)
    n_used = blk_end[-1:].astype(jnp.int32)
    order = jnp.pad(order, (0, MOE_BLOCK))
    src = order // TOP_K
    dst = (order % TOP_K) * t + src
    return block_expert, base.astype(jnp.int32), nvalid.astype(jnp.int32), src, dst, n_used


def _token_mixer_ln(h, hb, w_in, ret_gain, w_pool, pool_scale, w_branch, w_out, layer, ln_g, ln_b,
                    rope_a, rope_r, ret_tables, alpha):
    proj = _inproj(hb, w_in, layer, 0, OFF_GATE)
    gates = _inproj(hb, w_in, layer, OFF_GATE, N_IN - OFF_GATE, gate=True)
    qt, kk, vt = _moba_prep(proj, rope_a)
    y_a = _moba_attn(qt, kk, vt)
    y_r = _retention(proj, rope_r, ret_tables, ret_gain[None, :])
    y_p = _pool(proj, w_pool, layer, pool_scale[None, :])
    merged = _merge(y_a, y_r, y_p, w_branch, layer, gates)
    return _outproj_ln(merged, w_out, layer, h, ln_g[None, :], ln_b[None, :], alpha)


def _moe_ln(h, wr_hi, wr_lo, b_router, w_gate, w_up, w_down, layer, ln_g, ln_b, alpha):
    e_pad, p_pad = _router(h, wr_hi, wr_lo, b_router)
    plan = _dispatch_plan(e_pad[:, :TOP_K], h.shape[0])
    ycomb = _experts(h, w_gate, w_up, w_down, layer, plan)
    return _combine_ln(h, ycomb, p_pad, ln_g[None, :], ln_b[None, :], alpha)


def kernel(x, w_in, ret_gain, w_pool, pool_scale, w_branch, w_out, ln1_g, ln1_b, w_r1, b_r1, w_r2, b_r2, w_e_gate, w_e_up, w_e_down, ln2_g, ln2_b):
    bsz, seq, d = x.shape
    depth = w_in.shape[0]
    assert bsz == 1 and d == D_MODEL and seq % 1024 == 0
    t = seq
    alpha = float((2 * depth) ** 0.25)

    rope_a = _rope_tables(t, ROPE_DIM, ROPE_THETA, ATT_HEAD_DIM)
    rope_r = _rope_tables(t, RET_QK_DIM, RET_ROPE_THETA, RET_QK_DIM)
    ret_tables = _retention_tables()

    w_router = jnp.concatenate([w_r1, w_r2], axis=2)
    w_router = jnp.pad(w_router, ((0, 0), (0, 0), (0, ROUTER_LANES - w_router.shape[2])))
    wr_hi = w_router.astype(BF16)
    wr_lo = (w_router - wr_hi.astype(F32)).astype(BF16)
    b_router = jnp.pad(jnp.concatenate([b_r1, b_r2], axis=1),
                       ((0, 0), (0, ROUTER_LANES - N_GROUPS - N_EXPERTS)))[:, None, :]

    w_pool_b, w_branch_b, w_out_b = w_pool.astype(BF16), w_branch.astype(BF16), w_out.astype(BF16)
    h = x.reshape(t, d)
    hb = h.astype(BF16)
    for l in range(depth):
        h = _token_mixer_ln(h, hb, w_in, ret_gain[l], w_pool_b, pool_scale[l], w_branch_b, w_out_b, l,
                            ln1_g[l], ln1_b[l], rope_a, rope_r, ret_tables, alpha)
        h, hb = _moe_ln(h, wr_hi[l], wr_lo[l], b_router[l], w_e_gate, w_e_up, w_e_down, l,
                        ln2_g[l], ln2_b[l], alpha)
    return h.reshape(bsz, seq, d)
```

```python
import functools

import numpy as np
import jax
import jax.numpy as jnp
from jax import lax
from jax.experimental import pallas as pl
from jax.experimental.pallas import tpu as pltpu

D_MODEL = 2048

ATT_HEADS = 8
ATT_HEAD_DIM = 128
ATT_WIDTH = ATT_HEADS * ATT_HEAD_DIM
ROPE_DIM = ATT_HEAD_DIM // 4
ROPE_THETA = 500000.0
MOBA_BLOCK = 256
MOBA_TOPK = 3

RET_HEADS = 8
RET_QK_DIM = 64
RET_V_DIM = 128
RET_QK_WIDTH = RET_HEADS * RET_QK_DIM
RET_V_WIDTH = RET_HEADS * RET_V_DIM
RET_CHUNK = 256
RET_ROPE_THETA = 10000.0

POOL_WINDOWS = (2, 4, 8, 16)
POOL_GROUPS = 4
POOL_GROUP_DIM = 256
POOL_WIDTH = POOL_GROUPS * POOL_GROUP_DIM
POOL_HALO = 32

N_BRANCH = 3
BRANCH_WIDTH = 1024

OFF_QA = 0
OFF_KA = OFF_QA + ATT_WIDTH
OFF_VA = OFF_KA + ATT_WIDTH
OFF_QR = OFF_VA + ATT_WIDTH
OFF_KR = OFF_QR + RET_QK_WIDTH
OFF_VR = OFF_KR + RET_QK_WIDTH
OFF_GR = OFF_VR + RET_V_WIDTH
OFF_POOL = OFF_GR + RET_V_WIDTH
OFF_GATE = OFF_POOL + POOL_WIDTH
N_IN = OFF_GATE + N_BRANCH * D_MODEL

N_GROUPS = 4
EXPERTS_PER_GROUP = 8
N_EXPERTS = N_GROUPS * EXPERTS_PER_GROUP
TOP_K = 2
D_EXPERT = 512
MOE_BLOCK = 256
ROUTER_LANES = 128

LN_EPS = 1e-5
NEG = -1e30
TAKEN = -3e38
LOG2_E = 1.4426950408889634

LANES = 128
SUBLANES = 8
VMEM_LIMIT = 56 * 1024 * 1024

F32 = jnp.float32
BF16 = jnp.bfloat16


def _params(n_axes):
    return pltpu.CompilerParams(dimension_semantics=("arbitrary",) * n_axes,
                                vmem_limit_bytes=VMEM_LIMIT)


def _dot(a, b):
    return jnp.dot(a, b, preferred_element_type=F32)


def _dot_nt(a, b):
    return lax.dot_general(a, b, (((1,), (1,)), ((), ())), preferred_element_type=F32)


def _dot_tn(a, b):
    return lax.dot_general(a, b, (((0,), (0,)), ((), ())), preferred_element_type=F32)


def _split_bf16(a):
    hi = a.astype(BF16)
    lo = (a - hi.astype(F32)).astype(BF16)
    return hi, lo


def _dot_nt_f32ish(a, b):
    ah, al = _split_bf16(a)
    bh, bl = _split_bf16(b)
    return _dot_nt(ah, bh) + (_dot_nt(ah, bl) + _dot_nt(al, bh))


def _rotate(x, c, s1, s2, shift):
    return x * c + pltpu.roll(x, shift, 1) * s1 + pltpu.roll(x, LANES - shift, 1) * s2


def _sigmoid(x):
    return 0.5 * jnp.tanh(0.5 * x) + 0.5


def _layer_norm(v, g, b):
    mu = jnp.mean(v, axis=-1, keepdims=True)
    d = v - mu
    var = jnp.mean(d * d, axis=-1, keepdims=True)
    return d * lax.rsqrt(var + LN_EPS) * g + b


def _inproj_kernel(x_ref, w_ref, o_ref, wb_ref, *, gate):
    @pl.when(pl.program_id(1) == 0)
    def _():
        wb_ref[...] = w_ref[...].astype(BF16)

    y = _dot(x_ref[...], wb_ref[...])
    o_ref[...] = _sigmoid(y).astype(o_ref.dtype) if gate else y


def _inproj(xb, w, layer, col0, ncols, gate=False, tm=1024, tn=1024):
    t, d = xb.shape
    return pl.pallas_call(
        functools.partial(_inproj_kernel, gate=gate),
        grid=(ncols // tn, t // tm),
        in_specs=[pl.BlockSpec((tm, d), lambda j, i: (i, 0)),
                  pl.BlockSpec((None, d, tn), lambda j, i: (layer, 0, col0 // tn + j))],
        out_specs=pl.BlockSpec((tm, tn), lambda j, i: (i, j)),
        out_shape=jax.ShapeDtypeStruct((t, ncols), BF16 if gate else F32),
        scratch_shapes=[pltpu.VMEM((d, tn), BF16)],
        compiler_params=_params(2),
        name="inproj_gates" if gate else "inproj",
    )(xb, w)


ATT_KDIM = 2 * ATT_HEAD_DIM
ATT_VROWS = ATT_HEAD_DIM + 16


def _moba_prep_kernel(q_ref, k_ref, v_ref, c_ref, s1_ref, s2_ref, qt_ref, ko_ref, vt_ref, km_ref):
    i = pl.program_id(0)
    nblk = km_ref.shape[0]

    @pl.when(i == 0)
    def _():
        km_ref[...] = jnp.zeros_like(km_ref)

    c, s1, s2 = c_ref[...], s1_ref[...], s2_ref[...]
    blk = lax.broadcasted_iota(jnp.int32, (nblk, MOBA_BLOCK), 0)
    blkf = blk.astype(F32)
    scale = ATT_HEAD_DIM ** -0.5 * LOG2_E
    lane = lax.broadcasted_iota(jnp.int32, (MOBA_BLOCK, ATT_KDIM - ATT_HEAD_DIM), 1)
    this_block = jnp.where(lane == i, 1.0, 0.0).astype(BF16)
    row = lax.broadcasted_iota(jnp.int32, (ATT_VROWS - ATT_HEAD_DIM, MOBA_BLOCK), 0)
    ones_row = jnp.where(row == 0, 1.0, 0.0).astype(BF16)
    pad_rows = ATT_KDIM - ATT_HEAD_DIM - nblk
    eye = jnp.where(lax.broadcasted_iota(jnp.int32, (ATT_HEAD_DIM, ATT_HEAD_DIM), 0)
                    == lax.broadcasted_iota(jnp.int32, (ATT_HEAD_DIM, ATT_HEAD_DIM), 1), 1.0, 0.0).astype(BF16)
    for h in range(ATT_HEADS):
        sl = slice(h * ATT_HEAD_DIM, (h + 1) * ATT_HEAD_DIM)
        qr = _rotate(q_ref[:, sl], c, s1, s2, ROPE_DIM // 2)
        kr = _rotate(k_ref[:, sl], c, s1, s2, ROPE_DIM // 2)
        ko_ref[h, 0, :, :ATT_HEAD_DIM] = kr.astype(BF16)
        ko_ref[h, 0, :, ATT_HEAD_DIM:] = this_block
        km_blk = lax.broadcasted_iota(jnp.int32, (nblk, ATT_HEAD_DIM), 0)
        km_ref[:, sl] = jnp.where(km_blk == i, jnp.mean(kr, axis=0, keepdims=True), km_ref[:, sl])
        qt_ref[h, 0, :ATT_HEAD_DIM, :] = _dot_nt(eye, (qr * scale).astype(BF16)).astype(BF16)
        vt_ref[h, 0, :ATT_HEAD_DIM, :] = _dot_nt(eye, v_ref[:, sl].astype(BF16)).astype(BF16)
        vt_ref[h, 0, ATT_HEAD_DIM:, :] = ones_row
        gate = _dot_nt_f32ish(km_ref[:, sl], qr)
        gate = jnp.where(blk < i, gate, NEG)
        for _ in range(MOBA_TOPK):
            top = jnp.max(gate, axis=0, keepdims=True)
            first = jnp.min(jnp.where(gate == top, blkf, float(nblk)), axis=0, keepdims=True)
            gate = jnp.where(blkf == first, TAKEN, gate)
        bias = jnp.where(gate == TAKEN, jnp.where(blk < i, 0.0, NEG), NEG)
        qt_ref[h, 0, ATT_HEAD_DIM:ATT_HEAD_DIM + nblk, :] = bias.astype(BF16)
        if pad_rows:
            qt_ref[h, 0, ATT_HEAD_DIM + nblk:, :] = jnp.zeros((pad_rows, MOBA_BLOCK), BF16)


def _moba_prep(proj, rope):
    t = proj.shape[0]
    nblk = t // MOBA_BLOCK
    assert ATT_HEAD_DIM + nblk <= ATT_KDIM and nblk % 16 == 0
    c, s1, s2 = rope
    tbl = pl.BlockSpec((MOBA_BLOCK, LANES), lambda i: (i, 0))
    hshape = (ATT_HEADS, nblk)
    return pl.pallas_call(
        _moba_prep_kernel,
        grid=(nblk,),
        in_specs=[pl.BlockSpec((MOBA_BLOCK, ATT_WIDTH), lambda i: (i, OFF_QA // ATT_WIDTH)),
                  pl.BlockSpec((MOBA_BLOCK, ATT_WIDTH), lambda i: (i, OFF_KA // ATT_WIDTH)),
                  pl.BlockSpec((MOBA_BLOCK, ATT_WIDTH), lambda i: (i, OFF_VA // ATT_WIDTH)),
                  tbl, tbl, tbl],
        out_specs=[pl.BlockSpec((ATT_HEADS, 1, ATT_KDIM, MOBA_BLOCK), lambda i: (0, i, 0, 0)),
                   pl.BlockSpec((ATT_HEADS, 1, MOBA_BLOCK, ATT_KDIM), lambda i: (0, i, 0, 0)),
                   pl.BlockSpec((ATT_HEADS, 1, ATT_VROWS, MOBA_BLOCK), lambda i: (0, i, 0, 0))],
        out_shape=[jax.ShapeDtypeStruct(hshape + (ATT_KDIM, MOBA_BLOCK), BF16),
                   jax.ShapeDtypeStruct(hshape + (MOBA_BLOCK, ATT_KDIM), BF16),
                   jax.ShapeDtypeStruct(hshape + (ATT_VROWS, MOBA_BLOCK), BF16)],
        scratch_shapes=[pltpu.VMEM((nblk, ATT_WIDTH), F32)],
        compiler_params=_params(1),
        name="moba_prep",
    )(proj, proj, proj, c, s1, s2)


def _moba_attn_kernel(qt_ref, k_ref, vt_ref, o_ref, s_ref, p_ref, a_ref, acc_ref):
    i = pl.program_id(1)
    heads = qt_ref.shape[0]
    nblk = k_ref.shape[1]
    shape = (MOBA_BLOCK, MOBA_BLOCK)
    causal = lax.broadcasted_iota(jnp.int32, shape, 0) <= lax.broadcasted_iota(jnp.int32, shape, 1)

    ms = []
    for h in range(heads):
        s = _dot(k_ref[h, i, :, :ATT_HEAD_DIM], qt_ref[h, 0, :ATT_HEAD_DIM, :])
        s = jnp.where(causal, s, NEG)
        m = jnp.max(s, axis=0, keepdims=True)
        ms.append(m)
        acc_ref[h] = _dot(vt_ref[h, i], jnp.exp2(s - m).astype(BF16))
        p_ref[1, h] = jnp.zeros(shape, BF16)
        a_ref[1, h] = jnp.ones((1, MOBA_BLOCK), F32)

    def scores(j, slot):
        jc = jnp.minimum(j, nblk - 1)
        for h in range(heads):
            s_ref[slot, h] = _dot(k_ref[h, jc], qt_ref[h, 0])

    def pv(j, slot):
        jc = jnp.maximum(j, 0)
        for h in range(heads):
            acc_ref[h] = a_ref[slot, h] * acc_ref[h] + _dot(vt_ref[h, jc], p_ref[slot, h])

    def softmax(slot, ms):
        out = []
        for h in range(heads):
            s = s_ref[slot, h]
            m_new = jnp.maximum(ms[h], jnp.max(s, axis=0, keepdims=True))
            a_ref[slot, h] = jnp.exp2(ms[h] - m_new)
            p_ref[slot, h] = jnp.exp2(s - m_new).astype(BF16)
            out.append(m_new)
        return out

    scores(0, 0)

    def body(jj, ms):
        j = 2 * jj
        scores(j + 1, 1)
        pv(j - 1, 1)
        ms = softmax(0, ms)
        scores(j + 2, 0)
        pv(j, 0)
        return softmax(1, ms)

    lax.fori_loop(0, (i + 1) // 2, body, ms)
    pv(i - 1, 1)
    for h in range(heads):
        acc = acc_ref[h]
        out = acc[:ATT_HEAD_DIM] / acc[ATT_HEAD_DIM:ATT_HEAD_DIM + 1]
        o_ref[:, h * ATT_HEAD_DIM:(h + 1) * ATT_HEAD_DIM] = out.T.astype(o_ref.dtype)


def _moba_attn(qt, k, vt, heads=2):
    nblk = qt.shape[1]
    t = nblk * MOBA_BLOCK
    blk2 = (MOBA_BLOCK, MOBA_BLOCK)
    return pl.pallas_call(
        _moba_attn_kernel,
        grid=(ATT_HEADS // heads, nblk),
        in_specs=[pl.BlockSpec((heads, 1, ATT_KDIM, MOBA_BLOCK), lambda h, i: (h, i, 0, 0)),
                  pl.BlockSpec((heads, nblk, MOBA_BLOCK, ATT_KDIM), lambda h, i: (h, 0, 0, 0)),
                  pl.BlockSpec((heads, nblk, ATT_VROWS, MOBA_BLOCK), lambda h, i: (h, 0, 0, 0))],
        out_specs=pl.BlockSpec((MOBA_BLOCK, heads * ATT_HEAD_DIM), lambda h, i: (i, h)),
        out_shape=jax.ShapeDtypeStruct((t, ATT_WIDTH), BF16),
        scratch_shapes=[pltpu.VMEM((2, heads) + blk2, F32),
                        pltpu.VMEM((2, heads) + blk2, BF16),
                        pltpu.VMEM((2, heads, 1, MOBA_BLOCK), F32),
                        pltpu.VMEM((heads, ATT_VROWS, MOBA_BLOCK), F32)],
        compiler_params=_params(2),
        name="moba_attn",
    )(qt, k, vt)


def _retention_kernel(q_ref, k_ref, v_ref, g_ref, c_ref, s1_ref, s2_ref,
                      dec_ref, kdec_ref, qdec_ref, cdec_ref, gain_ref, o_ref, state_ref):
    @pl.when(pl.program_id(1) == 0)
    def _():
        state_ref[...] = jnp.zeros_like(state_ref)

    pairs = kdec_ref.shape[0]
    heads = 2 * pairs
    c, s1, s2 = c_ref[...], s1_ref[...], s2_ref[...]
    lane = lax.broadcasted_iota(jnp.int32, (RET_CHUNK, LANES), 1)

    def vslice(h):
        return slice(h * RET_V_DIM, (h + 1) * RET_V_DIM)

    qm, qdm, kb, kdt = [], [], [], []
    for p in range(pairs):
        psl = slice(p * LANES, (p + 1) * LANES)
        q = _rotate(q_ref[:, psl], c, s1, s2, RET_QK_DIM // 2)
        k = _rotate(k_ref[:, psl], c, s1, s2, RET_QK_DIM // 2) * (RET_QK_DIM ** -0.5)
        qd = q * qdec_ref[p]
        kd = k * kdec_ref[p]
        kb.append(k.astype(BF16))
        for hh in range(2):
            mine = (lane < RET_QK_DIM) if hh == 0 else (lane >= RET_QK_DIM)
            qm.append(jnp.where(mine, q, 0.0).astype(BF16))
            qdm.append(jnp.where(mine, qd, 0.0).astype(BF16))
            kdt.append(jnp.where(mine, kd, 0.0).astype(BF16))
    vb = [v_ref[:, vslice(h)].astype(BF16) for h in range(heads)]
    inner = [_dot_nt(qm[h], kb[h // 2]) for h in range(heads)]
    cross = [_dot(qdm[h], state_ref[h].astype(BF16)) for h in range(heads)]
    for h in range(heads):
        state_ref[h] = state_ref[h] * cdec_ref[h] + _dot_tn(kdt[h], vb[h])
    out = [_dot((inner[h] * dec_ref[h]).astype(BF16), vb[h]) + cross[h] for h in range(heads)]
    for h in range(heads):
        mu = jnp.mean(out[h], axis=-1, keepdims=True)
        d = out[h] - mu
        var = jnp.mean(d * d, axis=-1, keepdims=True)
        rn = d * lax.rsqrt(var + LN_EPS) * gain_ref[:, vslice(h)]
        g = g_ref[:, vslice(h)]
        o_ref[:, vslice(h)] = (g * _sigmoid(g) * rn).astype(o_ref.dtype)


def _retention(proj, rope, tables, gain, pairs=4):
    t = proj.shape[0]
    n = t // RET_CHUNK
    c, s1, s2 = rope
    dec, kdec, qdec, cdec = tables
    qk_w = pairs * LANES
    v_w = 2 * pairs * RET_V_DIM
    tbl = pl.BlockSpec((RET_CHUNK, LANES), lambda hp, ci: (ci, 0))
    return pl.pallas_call(
        _retention_kernel,
        grid=(RET_HEADS // (2 * pairs), n),
        in_specs=[pl.BlockSpec((RET_CHUNK, qk_w), lambda hp, ci: (ci, OFF_QR // qk_w + hp)),
                  pl.BlockSpec((RET_CHUNK, qk_w), lambda hp, ci: (ci, OFF_KR // qk_w + hp)),
                  pl.BlockSpec((RET_CHUNK, v_w), lambda hp, ci: (ci, OFF_VR // v_w + hp)),
                  pl.BlockSpec((RET_CHUNK, v_w), lambda hp, ci: (ci, OFF_GR // v_w + hp)),
                  tbl, tbl, tbl,
                  pl.BlockSpec((2 * pairs, RET_CHUNK, RET_CHUNK), lambda hp, ci: (hp, 0, 0)),
                  pl.BlockSpec((pairs, RET_CHUNK, LANES), lambda hp, ci: (hp, 0, 0)),
                  pl.BlockSpec((pairs, RET_CHUNK, LANES), lambda hp, ci: (hp, 0, 0)),
                  pl.BlockSpec((2 * pairs, 1, RET_V_DIM), lambda hp, ci: (hp, 0, 0)),
                  pl.BlockSpec((1, v_w), lambda hp, ci: (0, hp))],
        out_specs=pl.BlockSpec((RET_CHUNK, v_w), lambda hp, ci: (ci, hp)),
        out_shape=jax.ShapeDtypeStruct((t, RET_V_WIDTH), BF16),
        scratch_shapes=[pltpu.VMEM((2 * pairs, LANES, RET_V_DIM), F32)],
        compiler_params=_params(2),
        name="retention",
    )(proj, proj, proj, proj, c, s1, s2, dec, kdec, qdec, cdec, gain)


def _pool_kernel(cur_ref, prev_ref, w_ref, scale_ref, o_ref, buf_a, buf_b, *, tb):
    i = pl.program_id(0)
    g = pl.program_id(1)
    x = cur_ref[...]
    buf_a[0:POOL_HALO, :] = jnp.where(i > 0, prev_ref[...], 0.0)
    buf_a[POOL_HALO:, :] = x
    rows = tb + POOL_HALO
    pos = (i * tb + 1 + lax.broadcasted_iota(jnp.int32, (tb, 1), 0)).astype(F32)
    for gi, w in enumerate(POOL_WINDOWS):
        @pl.when(g == gi)
        def _(w=w):
            src, dst = buf_a, buf_b
            k, lo = 1, SUBLANES
            while k < w:
                dst[lo:rows, :] = src[lo:rows, :] + src[lo - k:rows - k, :]
                src, dst = dst, src
                k, lo = 2 * k, lo + SUBLANES
            tot = src[POOL_HALO:rows, :]
            d = tot / jnp.minimum(pos, float(w)) - x
            y = _dot(d.astype(BF16), w_ref[0])
            o_ref[...] = (y * scale_ref[...]).astype(o_ref.dtype)


def _pool(proj, w_pool, layer, scale, tb=1024):
    t = proj.shape[0]
    gd = POOL_GROUP_DIM
    assert SUBLANES * (max(POOL_WINDOWS).bit_length() - 1) <= POOL_HALO
    return pl.pallas_call(
        functools.partial(_pool_kernel, tb=tb),
        grid=(t // tb, POOL_GROUPS),
        in_specs=[pl.BlockSpec((tb, gd), lambda i, g: (i, OFF_POOL // gd + g)),
                  pl.BlockSpec((POOL_HALO, gd),
                               lambda i, g: (jnp.maximum(i * (tb // POOL_HALO) - 1, 0), OFF_POOL // gd + g)),
                  pl.BlockSpec((None, 1, gd, gd), lambda i, g: (layer, g, 0, 0)),
                  pl.BlockSpec((1, gd), lambda i, g: (0, g))],
        out_specs=pl.BlockSpec((tb, gd), lambda i, g: (i, g)),
        out_shape=jax.ShapeDtypeStruct((t, POOL_WIDTH), BF16),
        scratch_shapes=[pltpu.VMEM((tb + POOL_HALO, gd), F32), pltpu.VMEM((tb + POOL_HALO, gd), F32)],
        compiler_params=_params(2),
        name="pool",
    )(proj, proj, w_pool, scale)


def _merge_kernel(ya_ref, yr_ref, yp_ref, w_ref, ga_ref, gr_ref, gp_ref, o_ref):
    acc = ga_ref[...] * _dot(ya_ref[...], w_ref[0])
    acc = acc + gr_ref[...] * _dot(yr_ref[...], w_ref[1])
    acc = acc + gp_ref[...] * _dot(yp_ref[...], w_ref[2])
    o_ref[...] = acc.astype(o_ref.dtype)


def _merge(ya, yr, yp, w_branch, layer, gates, tm=1024, tn=1024):
    t = ya.shape[0]
    d = w_branch.shape[3]
    br = pl.BlockSpec((tm, BRANCH_WIDTH), lambda i, j: (i, 0))

    def gate(nb):
        return pl.BlockSpec((tm, tn), lambda i, j: (i, (nb * d) // tn + j))

    return pl.pallas_call(
        _merge_kernel,
        grid=(t // tm, d // tn),
        in_specs=[br, br, br,
                  pl.BlockSpec((None, N_BRANCH, BRANCH_WIDTH, tn), lambda i, j: (layer, 0, 0, j)),
                  gate(0), gate(1), gate(2)],
        out_specs=pl.BlockSpec((tm, tn), lambda i, j: (i, j)),
        out_shape=jax.ShapeDtypeStruct((t, d), BF16),
        compiler_params=_params(2),
        name="merge",
    )(ya, yr, yp, w_branch, gates, gates, gates)


def _outproj_ln_kernel(m_ref, w_ref, x_ref, g_ref, b_ref, o_ref, *, alpha):
    half = m_ref.shape[0] // 2
    for r in range(2):
        sl = slice(r * half, (r + 1) * half)
        y = _dot(m_ref[sl, :], w_ref[...])
        o_ref[sl, :] = _layer_norm(alpha * x_ref[sl, :] + y, g_ref[...], b_ref[...])


def _outproj_ln(merged, w_out, layer, x, g, b, alpha, tm=512):
    t, d = x.shape
    row = pl.BlockSpec((tm, d), lambda i: (i, 0))
    vec = pl.BlockSpec((1, d), lambda i: (0, 0))
    return pl.pallas_call(
        functools.partial(_outproj_ln_kernel, alpha=alpha),
        grid=(t // tm,),
        in_specs=[row, pl.BlockSpec((None, d, d), lambda i: (layer, 0, 0)), row, vec, vec],
        out_specs=row,
        out_shape=jax.ShapeDtypeStruct((t, d), F32),
        compiler_params=_params(1),
        name="outproj_ln",
    )(merged, w_out, x, g, b)


def _router_kernel(x_ref, wh_ref, wl_ref, b_ref, e_ref, p_ref):
    x = x_ref[...]
    xh, xl = _split_bf16(x)
    z = _dot(xh, wh_ref[...]) + (_dot(xh, wl_ref[...]) + _dot(xl, wh_ref[...])) + b_ref[...]
    lane = lax.broadcasted_iota(jnp.int32, z.shape, 1)
    big = jnp.int32(ROUTER_LANES)

    def masked_softmax(mask):
        zm = jnp.where(mask, z, NEG)
        e = jnp.where(mask, jnp.exp(zm - jnp.max(zm, axis=-1, keepdims=True)), 0.0)
        return e / jnp.sum(e, axis=-1, keepdims=True)

    def top1(p, mask):
        pm = jnp.where(mask, p, -1.0)
        top = jnp.max(pm, axis=-1, keepdims=True)
        idx = jnp.min(jnp.where(pm == top, lane, big), axis=-1, keepdims=True)
        return top, idx

    in_groups = lane < N_GROUPS
    p1 = masked_softmax(in_groups)
    g_top, g_idx = top1(p1, in_groups)
    lo = N_GROUPS + g_idx * EXPERTS_PER_GROUP
    in_group = (lane >= lo) & (lane < lo + EXPERTS_PER_GROUP)
    p2 = masked_softmax(in_group)
    e1, i1 = top1(p2, in_group)
    e2, i2 = top1(p2, in_group & (lane != i1))
    denom = e1 + e2
    w1 = g_top * e1 / denom
    w2 = g_top * e2 / denom
    e_ref[...] = jnp.where(lane == 0, i1 - N_GROUPS, jnp.where(lane == 1, i2 - N_GROUPS, 0))
    p_ref[...] = jnp.where(lane == 0, w1, jnp.where(lane == 1, w2, 0.0))


def _router(x, w_hi, w_lo, bias, tm=512):
    t, d = x.shape
    row = pl.BlockSpec((tm, d), lambda i: (i, 0))
    wsp = pl.BlockSpec((d, ROUTER_LANES), lambda i: (0, 0))
    out = pl.BlockSpec((tm, ROUTER_LANES), lambda i: (i, 0))
    return pl.pallas_call(
        _router_kernel,
        grid=(t // tm,),
        in_specs=[row, wsp, wsp, pl.BlockSpec((1, ROUTER_LANES), lambda i: (0, 0))],
        out_specs=[out, out],
        out_shape=[jax.ShapeDtypeStruct((t, ROUTER_LANES), jnp.int32),
                   jax.ShapeDtypeStruct((t, ROUTER_LANES), F32)],
        compiler_params=_params(1),
        name="router",
    )(x, w_hi, w_lo, bias)


WAIT_CHUNKS = (64, 8, 1)


def _start_rows(n, copy):
    for r in range(MOE_BLOCK):
        @pl.when(r < n)
        def _(r=r):
            copy(r).start()


def _wait_rows(n, span):
    rem = n
    for c in WAIT_CHUNKS:
        cnt = lax.shift_right_logical(rem, c.bit_length() - 1)

        def one(_, carry, c=c):
            span(c).wait()
            return carry

        lax.fori_loop(0, cnt, one, 0)
        rem = rem & (c - 1)


def _experts_kernel(be_ref, base_ref, nvalid_ref, src_ref, dst_ref, nused_ref,
                    x_hbm, wg_ref, wu_ref, wd_ref, y_hbm,
                    xbuf, ybuf, wgb, wub, wdb, sem_in, sem_out):
    b = pl.program_id(0)
    n_used = nused_ref[0]
    slot = b & 1

    def start_gather(bb, sl):
        base = base_ref[bb]
        _start_rows(nvalid_ref[bb], lambda r: pltpu.make_async_copy(
            x_hbm.at[pl.ds(src_ref[base + r], 1), :], xbuf.at[sl, pl.ds(r, 1), :], sem_in.at[sl]))

    def wait_gather(bb, sl):
        _wait_rows(nvalid_ref[bb], lambda c: pltpu.make_async_copy(
            x_hbm.at[pl.ds(0, c), :], xbuf.at[sl, pl.ds(0, c), :], sem_in.at[sl]))

    def start_scatter(bb, sl):
        base = base_ref[bb]
        _start_rows(nvalid_ref[bb], lambda r: pltpu.make_async_copy(
            ybuf.at[sl, pl.ds(r, 1), :], y_hbm.at[pl.ds(dst_ref[base + r], 1), :], sem_out.at[sl]))

    def wait_scatter(bb, sl):
        _wait_rows(nvalid_ref[bb], lambda c: pltpu.make_async_copy(
            ybuf.at[sl, pl.ds(0, c), :], y_hbm.at[pl.ds(0, c), :], sem_out.at[sl]))

    @pl.when(b == 0)
    def _():
        xbuf[...] = jnp.zeros_like(xbuf)
        start_gather(0, 0)

    @pl.when(b < n_used)
    def _():
        @pl.when(b + 1 < n_used)
        def _():
            start_gather(b + 1, 1 - slot)

        new_expert = (b == 0) | (be_ref[b] != be_ref[jnp.maximum(b - 1, 0)])

        @pl.when(new_expert)
        def _():
            wgb[...] = wg_ref[...].astype(BF16)
            wub[...] = wu_ref[...].astype(BF16)
            wdb[...] = wd_ref[...].astype(BF16)

        wait_gather(b, slot)

        @pl.when(b >= 2)
        def _():
            wait_scatter(b - 2, slot)

        x = xbuf[slot].astype(BF16)
        gate = _dot(x, wgb[...])
        up = _dot(x, wub[...])
        hid = (gate * _sigmoid(gate) * up).astype(BF16)
        ybuf[slot] = _dot(hid, wdb[...])
        start_scatter(b, slot)

        @pl.when(b == n_used - 1)
        def _():
            @pl.when(b >= 1)
            def _():
                wait_scatter(b - 1, 1 - slot)

            wait_scatter(b, slot)


def _experts(x, wg, wu, wd, layer, plan):
    t, d = x.shape
    block_expert, base, nvalid, src, dst, n_used = plan
    n_blocks = block_expert.shape[0]

    def wspec(shape):
        return pl.BlockSpec((None, None) + shape, lambda b, be, *_: (layer, be[b], 0, 0))

    grid_spec = pltpu.PrefetchScalarGridSpec(
        num_scalar_prefetch=6,
        grid=(n_blocks,),
        in_specs=[pl.BlockSpec(memory_space=pl.ANY),
                  wspec((d, D_EXPERT)), wspec((d, D_EXPERT)), wspec((D_EXPERT, d))],
        out_specs=pl.BlockSpec(memory_space=pl.ANY),
        scratch_shapes=[pltpu.VMEM((2, MOE_BLOCK, d), F32),
                        pltpu.VMEM((2, MOE_BLOCK, d), F32),
                        pltpu.VMEM((d, D_EXPERT), BF16),
                        pltpu.VMEM((d, D_EXPERT), BF16),
                        pltpu.VMEM((D_EXPERT, d), BF16),
                        pltpu.SemaphoreType.DMA((2,)),
                        pltpu.SemaphoreType.DMA((2,))],
    )
    y = pl.pallas_call(
        _experts_kernel,
        grid_spec=grid_spec,
        out_shape=jax.ShapeDtypeStruct((TOP_K * t, d), F32),
        compiler_params=_params(1),
        name="experts",
    )(block_expert, base, nvalid, src, dst, n_used, x, wg, wu, wd)
    return y.reshape(TOP_K, t, d)


def _combine_ln_kernel(x_ref, y0_ref, y1_ref, p_ref, g_ref, b_ref, o_ref, ob_ref, *, alpha):
    p = p_ref[...]
    y = p[:, 0:1] * y0_ref[...] + p[:, 1:2] * y1_ref[...]
    out = _layer_norm(alpha * x_ref[...] + y, g_ref[...], b_ref[...])
    o_ref[...] = out
    ob_ref[...] = out.astype(BF16)


def _combine_ln(x, ycomb, probs, g, b, alpha, tm=512):
    t, d = x.shape
    row = pl.BlockSpec((tm, d), lambda i: (i, 0))
    vec = pl.BlockSpec((1, d), lambda i: (0, 0))
    return pl.pallas_call(
        functools.partial(_combine_ln_kernel, alpha=alpha),
        grid=(t // tm,),
        in_specs=[row,
                  pl.BlockSpec((None, tm, d), lambda i: (0, i, 0)),
                  pl.BlockSpec((None, tm, d), lambda i: (1, i, 0)),
                  pl.BlockSpec((tm, ROUTER_LANES), lambda i: (i, 0)), vec, vec],
        out_specs=[row, row],
        out_shape=[jax.ShapeDtypeStruct((t, d), F32), jax.ShapeDtypeStruct((t, d), BF16)],
        compiler_params=_params(1),
        name="combine_ln",
    )(x, ycomb, ycomb, probs, g, b)


def _rope_tables(t, rot_dim, theta, head_dim):
    half = rot_dim // 2
    inv_freq = 1.0 / (theta ** (jnp.arange(0, rot_dim, 2, dtype=F32) / rot_dim))
    ang = jnp.arange(t, dtype=F32)[:, None] * inv_freq[None, :]
    cos, sin = jnp.cos(ang), jnp.sin(ang)
    rest = head_dim - rot_dim
    ones, zeros = jnp.ones((t, rest), F32), jnp.zeros((t, rest), F32)
    zh = jnp.zeros((t, half), F32)
    c = jnp.concatenate([cos, cos, ones], axis=1)
    s1 = jnp.concatenate([zh, sin, zeros], axis=1)
    s2 = jnp.concatenate([-sin, zh, zeros], axis=1)
    rep = LANES // head_dim
    return tuple(jnp.tile(a, (1, rep)) for a in (c, s1, s2))


def _retention_tables():
    h, c = RET_HEADS, RET_CHUNK
    log_gamma = jnp.log1p(-jnp.exp2(-5.0 - jnp.arange(h, dtype=F32)))
    pos = jnp.arange(c, dtype=F32)
    diff = pos[:, None] - pos[None, :]
    decay = jnp.where(diff >= 0, jnp.exp(log_gamma[:, None, None] * jnp.maximum(diff, 0.0)), 0.0)
    k_decay = jnp.exp(log_gamma[:, None] * (c - 1.0 - pos)[None, :])
    q_decay = jnp.exp(log_gamma[:, None] * (pos + 1.0)[None, :])
    chunk_decay = jnp.exp(log_gamma * c)

    def pair_lanes(a):
        a = a.reshape(h // 2, 2, c).transpose(0, 2, 1)
        return jnp.repeat(a, RET_QK_DIM, axis=2)

    cdec = jnp.broadcast_to(chunk_decay[:, None, None], (h, 1, RET_V_DIM))
    return decay, pair_lanes(k_decay), pair_lanes(q_decay), cdec


def _dispatch_plan(expert, t):
    a = t * TOP_K
    flat_e = expert.reshape(a)
    order = jnp.argsort(flat_e).astype(jnp.int32)
    counts = jnp.sum(flat_e[None, :] == jnp.arange(N_EXPERTS, dtype=flat_e.dtype)[:, None],
                     axis=1, dtype=jnp.int32)
    start = jnp.cumsum(counts) - counts
    blocks = (counts + MOE_BLOCK - 1) // MOE_BLOCK
    blk_end = jnp.cumsum(blocks)
    n_blocks = -(-a // MOE_BLOCK) + N_EXPERTS
    b = jnp.arange(n_blocks, dtype=jnp.int32)
    block_expert = jnp.minimum(jnp.sum(blk_end[None, :] <= b[:, None], axis=1, dtype=jnp.int32),
                               N_EXPERTS - 1)
    first = (b - (blk_end - blocks)[block_expert]) * MOE_BLOCK
    base = start[block_expert] + first
    nvalid = jnp.clip(counts[block_expert] - first, 0, MOE_BLOCK)
    nvalid = jnp.where(b < blk_end[-1], nvalid, 0)
    n_used = blk_end[-1:].astype(jnp.int32)
    order = jnp.pad(order, (0, MOE_BLOCK))
    src = order // TOP_K
    dst = (order % TOP_K) * t + src
    return block_expert, base.astype(jnp.int32), nvalid.astype(jnp.int32), src, dst, n_used


def _token_mixer_ln(h, hb, w_in, ret_gain, w_pool, pool_scale, w_branch, w_out, layer, ln_g, ln_b,
                    rope_a, rope_r, ret_tables, alpha):
    proj = _inproj(hb, w_in, layer, 0, OFF_GATE)
    gates = _inproj(hb, w_in, layer, OFF_GATE, N_IN - OFF_GATE, gate=True)
    qt, kk, vt = _moba_prep(proj, rope_a)
    y_a = _moba_attn(qt, kk, vt)
    y_r = _retention(proj, rope_r, ret_tables, ret_gain[None, :])
    y_p = _pool(proj, w_pool, layer, pool_scale[None, :])
    merged = _merge(y_a, y_r, y_p, w_branch, layer, gates)
    return _outproj_ln(merged, w_out, layer, h, ln_g[None, :], ln_b[None, :], alpha)


def _moe_ln(h, wr_hi, wr_lo, b_router, w_gate, w_up, w_down, layer, ln_g, ln_b, alpha):
    e_pad, p_pad = _router(h, wr_hi, wr_lo, b_router)
    plan = _dispatch_plan(e_pad[:, :TOP_K], h.shape[0])
    ycomb = _experts(h, w_gate, w_up, w_down, layer, plan)
    return _combine_ln(h, ycomb, p_pad, ln_g[None, :], ln_b[None, :], alpha)


def kernel(x, w_in, ret_gain, w_pool, pool_scale, w_branch, w_out, ln1_g, ln1_b, w_r1, b_r1, w_r2, b_r2, w_e_gate, w_e_up, w_e_down, ln2_g, ln2_b):
    bsz, seq, d = x.shape
    depth = w_in.shape[0]
    assert bsz == 1 and d == D_MODEL and seq % 1024 == 0
    t = seq
    alpha = float((2 * depth) ** 0.25)

    rope_a = _rope_tables(t, ROPE_DIM, ROPE_THETA, ATT_HEAD_DIM)
    rope_r = _rope_tables(t, RET_QK_DIM, RET_ROPE_THETA, RET_QK_DIM)
    ret_tables = _retention_tables()

    w_router = jnp.concatenate([w_r1, w_r2], axis=2)
    w_router = jnp.pad(w_router, ((0, 0), (0, 0), (0, ROUTER_LANES - w_router.shape[2])))
    wr_hi = w_router.astype(BF16)
    wr_lo = (w_router - wr_hi.astype(F32)).astype(BF16)
    b_router = jnp.pad(jnp.concatenate([b_r1, b_r2], axis=1),
                       ((0, 0), (0, ROUTER_LANES - N_GROUPS - N_EXPERTS)))[:, None, :]

    w_pool_b, w_branch_b, w_out_b = w_pool.astype(BF16), w_branch.astype(BF16), w_out.astype(BF16)
    h = x.reshape(t, d)
    hb = h.astype(BF16)
    for l in range(depth):
        h = _token_mixer_ln(h, hb, w_in, ret_gain[l], w_pool_b, pool_scale[l], w_branch_b, w_out_b, l,
                            ln1_g[l], ln1_b[l], rope_a, rope_r, ret_tables, alpha)
        h, hb = _moe_ln(h, wr_hi[l], wr_lo[l], b_router[l], w_e_gate, w_e_up, w_e_down, l,
                        ln2_g[l], ln2_b[l], alpha)
    return h.reshape(bsz, seq, d)
```

```python
import functools

import jax
import jax.numpy as jnp
from jax import lax
from jax.experimental import pallas as pl
from jax.experimental.pallas import tpu as pltpu

D_MODEL = 2048

ATT_HEADS = 8
ATT_HEAD_DIM = 128
ATT_WIDTH = ATT_HEADS * ATT_HEAD_DIM
ROPE_DIM = ATT_HEAD_DIM // 4
ROPE_THETA = 500000.0
MOBA_BLOCK = 256
MOBA_TOPK = 3

RET_HEADS = 8
RET_QK_DIM = 64
RET_V_DIM = 128
RET_QK_WIDTH = RET_HEADS * RET_QK_DIM
RET_V_WIDTH = RET_HEADS * RET_V_DIM
RET_CHUNK = 256
RET_ROPE_THETA = 10000.0

POOL_WINDOWS = (2, 4, 8, 16)
POOL_GROUPS = 4
POOL_GROUP_DIM = 256
POOL_WIDTH = POOL_GROUPS * POOL_GROUP_DIM
POOL_HALO = 32

N_BRANCH = 3
BRANCH_WIDTH = 1024

OFF_QA = 0
OFF_KA = OFF_QA + ATT_WIDTH
OFF_VA = OFF_KA + ATT_WIDTH
OFF_QR = OFF_VA + ATT_WIDTH
OFF_KR = OFF_QR + RET_QK_WIDTH
OFF_VR = OFF_KR + RET_QK_WIDTH
OFF_GR = OFF_VR + RET_V_WIDTH
OFF_POOL = OFF_GR + RET_V_WIDTH
OFF_GATE = OFF_POOL + POOL_WIDTH
N_IN = OFF_GATE + N_BRANCH * D_MODEL

N_GROUPS = 4
EXPERTS_PER_GROUP = 8
N_EXPERTS = N_GROUPS * EXPERTS_PER_GROUP
TOP_K = 2
D_EXPERT = 512
MOE_BLOCK = 256
ROUTER_LANES = 128

LN_EPS = 1e-5
NEG = -1e30
TAKEN = -3e38
LOG2_E = 1.4426950408889634

LANES = 128
SUBLANES = 8
VMEM_LIMIT = 56 * 1024 * 1024

F32 = jnp.float32
BF16 = jnp.bfloat16


def _params(n_axes):
    return pltpu.CompilerParams(dimension_semantics=("arbitrary",) * n_axes,
                                vmem_limit_bytes=VMEM_LIMIT)


def _dot(a, b):
    return jnp.dot(a, b, preferred_element_type=F32)


def _dot_nt(a, b):
    return lax.dot_general(a, b, (((1,), (1,)), ((), ())), preferred_element_type=F32)


def _dot_tn(a, b):
    return lax.dot_general(a, b, (((0,), (0,)), ((), ())), preferred_element_type=F32)


def _split_bf16(a):
    hi = a.astype(BF16)
    lo = (a - hi.astype(F32)).astype(BF16)
    return hi, lo


def _dot_nt_f32ish(a, b):
    ah, al = _split_bf16(a)
    bh, bl = _split_bf16(b)
    return _dot_nt(ah, bh) + (_dot_nt(ah, bl) + _dot_nt(al, bh))


def _rotate(x, c, s1, s2, shift):
    return x * c + pltpu.roll(x, shift, 1) * s1 + pltpu.roll(x, LANES - shift, 1) * s2


def _sigmoid(x):
    return 0.5 * jnp.tanh(0.5 * x) + 0.5


def _layer_norm(v, g, b):
    mu = jnp.mean(v, axis=-1, keepdims=True)
    d = v - mu
    var = jnp.mean(d * d, axis=-1, keepdims=True)
    return d * lax.rsqrt(var + LN_EPS) * g + b


def _inproj_kernel(x_ref, w_ref, o_ref, wb_ref, *, gate):
    @pl.when(pl.program_id(1) == 0)
    def _():
        wb_ref[...] = w_ref[...].astype(BF16)

    y = _dot(x_ref[...], wb_ref[...])
    o_ref[...] = _sigmoid(y).astype(o_ref.dtype) if gate else y


def _inproj(xb, w, layer, col0, ncols, gate=False, tm=1024, tn=1024):
    t, d = xb.shape
    return pl.pallas_call(
        functools.partial(_inproj_kernel, gate=gate),
        grid=(ncols // tn, t // tm),
        in_specs=[pl.BlockSpec((tm, d), lambda j, i: (i, 0)),
                  pl.BlockSpec((None, d, tn), lambda j, i: (layer, 0, col0 // tn + j))],
        out_specs=pl.BlockSpec((tm, tn), lambda j, i: (i, j)),
        out_shape=jax.ShapeDtypeStruct((t, ncols), BF16 if gate else F32),
        scratch_shapes=[pltpu.VMEM((d, tn), BF16)],
        compiler_params=_params(2),
        name="inproj_gates" if gate else "inproj",
    )(xb, w)


ATT_KDIM = 2 * ATT_HEAD_DIM
ATT_VROWS = ATT_HEAD_DIM + 16


def _moba_prep_kernel(q_ref, k_ref, v_ref, c_ref, s1_ref, s2_ref, qt_ref, ko_ref, vt_ref, km_ref):
    i = pl.program_id(0)
    nblk = km_ref.shape[0]

    @pl.when(i == 0)
    def _():
        km_ref[...] = jnp.zeros_like(km_ref)

    c, s1, s2 = c_ref[...], s1_ref[...], s2_ref[...]
    blk = lax.broadcasted_iota(jnp.int32, (nblk, MOBA_BLOCK), 0)
    blkf = blk.astype(F32)
    scale = ATT_HEAD_DIM ** -0.5 * LOG2_E
    lane = lax.broadcasted_iota(jnp.int32, (MOBA_BLOCK, ATT_KDIM - ATT_HEAD_DIM), 1)
    this_block = jnp.where(lane == i, 1.0, 0.0).astype(BF16)
    row = lax.broadcasted_iota(jnp.int32, (ATT_VROWS - ATT_HEAD_DIM, MOBA_BLOCK), 0)
    ones_row = jnp.where(row == 0, 1.0, 0.0).astype(BF16)
    pad_rows = ATT_KDIM - ATT_HEAD_DIM - nblk
    eye = jnp.where(lax.broadcasted_iota(jnp.int32, (ATT_HEAD_DIM, ATT_HEAD_DIM), 0)
                    == lax.broadcasted_iota(jnp.int32, (ATT_HEAD_DIM, ATT_HEAD_DIM), 1), 1.0, 0.0).astype(BF16)
    for h in range(ATT_HEADS):
        sl = slice(h * ATT_HEAD_DIM, (h + 1) * ATT_HEAD_DIM)
        qr = _rotate(q_ref[:, sl], c, s1, s2, ROPE_DIM // 2)
        kr = _rotate(k_ref[:, sl], c, s1, s2, ROPE_DIM // 2)
        ko_ref[h, 0, :, :ATT_HEAD_DIM] = kr.astype(BF16)
        ko_ref[h, 0, :, ATT_HEAD_DIM:] = this_block
        km_blk = lax.broadcasted_iota(jnp.int32, (nblk, ATT_HEAD_DIM), 0)
        km_ref[:, sl] = jnp.where(km_blk == i, jnp.mean(kr, axis=0, keepdims=True), km_ref[:, sl])
        qt_ref[h, 0, :ATT_HEAD_DIM, :] = _dot_nt(eye, (qr * scale).astype(BF16)).astype(BF16)
        vt_ref[h, 0, :ATT_HEAD_DIM, :] = _dot_nt(eye, v_ref[:, sl].astype(BF16)).astype(BF16)
        vt_ref[h, 0, ATT_HEAD_DIM:, :] = ones_row
        gate = _dot_nt_f32ish(km_ref[:, sl], qr)
        gate = jnp.where(blk < i, gate, NEG)
        for _ in range(MOBA_TOPK):
            top = jnp.max(gate, axis=0, keepdims=True)
            first = jnp.min(jnp.where(gate == top, blkf, float(nblk)), axis=0, keepdims=True)
            gate = jnp.where(blkf == first, TAKEN, gate)
        bias = jnp.where(gate == TAKEN, jnp.where(blk < i, 0.0, NEG), NEG)
        qt_ref[h, 0, ATT_HEAD_DIM:ATT_HEAD_DIM + nblk, :] = bias.astype(BF16)
        if pad_rows:
            qt_ref[h, 0, ATT_HEAD_DIM + nblk:, :] = jnp.zeros((pad_rows, MOBA_BLOCK), BF16)


def _moba_prep(proj, rope):
    t = proj.shape[0]
    nblk = t // MOBA_BLOCK
    assert ATT_HEAD_DIM + nblk <= ATT_KDIM and nblk % 16 == 0
    c, s1, s2 = rope
    tbl = pl.BlockSpec((MOBA_BLOCK, LANES), lambda i: (i, 0))
    hshape = (ATT_HEADS, nblk)
    return pl.pallas_call(
        _moba_prep_kernel,
        grid=(nblk,),
        in_specs=[pl.BlockSpec((MOBA_BLOCK, ATT_WIDTH), lambda i: (i, OFF_QA // ATT_WIDTH)),
                  pl.BlockSpec((MOBA_BLOCK, ATT_WIDTH), lambda i: (i, OFF_KA // ATT_WIDTH)),
                  pl.BlockSpec((MOBA_BLOCK, ATT_WIDTH), lambda i: (i, OFF_VA // ATT_WIDTH)),
                  tbl, tbl, tbl],
        out_specs=[pl.BlockSpec((ATT_HEADS, 1, ATT_KDIM, MOBA_BLOCK), lambda i: (0, i, 0, 0)),
                   pl.BlockSpec((ATT_HEADS, 1, MOBA_BLOCK, ATT_KDIM), lambda i: (0, i, 0, 0)),
                   pl.BlockSpec((ATT_HEADS, 1, ATT_VROWS, MOBA_BLOCK), lambda i: (0, i, 0, 0))],
        out_shape=[jax.ShapeDtypeStruct(hshape + (ATT_KDIM, MOBA_BLOCK), BF16),
                   jax.ShapeDtypeStruct(hshape + (MOBA_BLOCK, ATT_KDIM), BF16),
                   jax.ShapeDtypeStruct(hshape + (ATT_VROWS, MOBA_BLOCK), BF16)],
        scratch_shapes=[pltpu.VMEM((nblk, ATT_WIDTH), F32)],
        compiler_params=_params(1),
        name="moba_prep",
    )(proj, proj, proj, c, s1, s2)


def _moba_attn_kernel(qt_ref, k_ref, vt_ref, o_ref, s_ref, p_ref, a_ref, acc_ref):
    i = pl.program_id(1)
    heads = qt_ref.shape[0]
    nblk = k_ref.shape[1]
    shape = (MOBA_BLOCK, MOBA_BLOCK)
    causal = lax.broadcasted_iota(jnp.int32, shape, 0) <= lax.broadcasted_iota(jnp.int32, shape, 1)

    ms = []
    for h in range(heads):
        s = _dot(k_ref[h, i, :, :ATT_HEAD_DIM], qt_ref[h, 0, :ATT_HEAD_DIM, :])
        s = jnp.where(causal, s, NEG)
        m = jnp.max(s, axis=0, keepdims=True)
        ms.append(m)
        acc_ref[h] = _dot(vt_ref[h, i], jnp.exp2(s - m).astype(BF16))
        p_ref[1, h] = jnp.zeros(shape, BF16)
        a_ref[1, h] = jnp.ones((1, MOBA_BLOCK), F32)

    def scores(j, slot):
        jc = jnp.minimum(j, nblk - 1)
        for h in range(heads):
            s_ref[slot, h] = _dot(k_ref[h, jc], qt_ref[h, 0])

    def pv(j, slot):
        jc = jnp.maximum(j, 0)
        for h in range(heads):
            acc_ref[h] = a_ref[slot, h] * acc_ref[h] + _dot(vt_ref[h, jc], p_ref[slot, h])

    def softmax(slot, ms):
        out = []
        for h in range(heads):
            s = s_ref[slot, h]
            m_new = jnp.maximum(ms[h], jnp.max(s, axis=0, keepdims=True))
            a_ref[slot, h] = jnp.exp2(ms[h] - m_new)
            p_ref[slot, h] = jnp.exp2(s - m_new).astype(BF16)
            out.append(m_new)
        return out

    scores(0, 0)

    def body(jj, ms):
        j = 2 * jj
        scores(j + 1, 1)
        pv(j - 1, 1)
        ms = softmax(0, ms)
        scores(j + 2, 0)
        pv(j, 0)
        return softmax(1, ms)

    lax.fori_loop(0, (i + 1) // 2, body, ms)
    pv(i - 1, 1)
    for h in range(heads):
        acc = acc_ref[h]
        out = acc[:ATT_HEAD_DIM] / acc[ATT_HEAD_DIM:ATT_HEAD_DIM + 1]
        o_ref[:, h * ATT_HEAD_DIM:(h + 1) * ATT_HEAD_DIM] = out.T.astype(o_ref.dtype)


def _moba_attn(qt, k, vt, heads=2):
    nblk = qt.shape[1]
    t = nblk * MOBA_BLOCK
    blk2 = (MOBA_BLOCK, MOBA_BLOCK)
    return pl.pallas_call(
        _moba_attn_kernel,
        grid=(ATT_HEADS // heads, nblk),
        in_specs=[pl.BlockSpec((heads, 1, ATT_KDIM, MOBA_BLOCK), lambda h, i: (h, i, 0, 0)),
                  pl.BlockSpec((heads, nblk, MOBA_BLOCK, ATT_KDIM), lambda h, i: (h, 0, 0, 0)),
                  pl.BlockSpec((heads, nblk, ATT_VROWS, MOBA_BLOCK), lambda h, i: (h, 0, 0, 0))],
        out_specs=pl.BlockSpec((MOBA_BLOCK, heads * ATT_HEAD_DIM), lambda h, i: (i, h)),
        out_shape=jax.ShapeDtypeStruct((t, ATT_WIDTH), BF16),
        scratch_shapes=[pltpu.VMEM((2, heads) + blk2, F32),
                        pltpu.VMEM((2, heads) + blk2, BF16),
                        pltpu.VMEM((2, heads, 1, MOBA_BLOCK), F32),
                        pltpu.VMEM((heads, ATT_VROWS, MOBA_BLOCK), F32)],
        compiler_params=_params(2),
        name="moba_attn",
    )(qt, k, vt)


def _retention_kernel(q_ref, k_ref, v_ref, g_ref, c_ref, s1_ref, s2_ref,
                      dec_ref, kdec_ref, qdec_ref, cdec_ref, gain_ref, o_ref, state_ref):
    @pl.when(pl.program_id(1) == 0)
    def _():
        state_ref[...] = jnp.zeros_like(state_ref)

    pairs = kdec_ref.shape[0]
    heads = 2 * pairs
    c, s1, s2 = c_ref[...], s1_ref[...], s2_ref[...]
    lane = lax.broadcasted_iota(jnp.int32, (RET_CHUNK, LANES), 1)

    def vslice(h):
        return slice(h * RET_V_DIM, (h + 1) * RET_V_DIM)

    qm, qdm, kb, kdt = [], [], [], []
    for p in range(pairs):
        psl = slice(p * LANES, (p + 1) * LANES)
        q = _rotate(q_ref[:, psl], c, s1, s2, RET_QK_DIM // 2)
        k = _rotate(k_ref[:, psl], c, s1, s2, RET_QK_DIM // 2) * (RET_QK_DIM ** -0.5)
        qd = q * qdec_ref[p]
        kd = k * kdec_ref[p]
        kb.append(k.astype(BF16))
        for hh in range(2):
            mine = (lane < RET_QK_DIM) if hh == 0 else (lane >= RET_QK_DIM)
            qm.append(jnp.where(mine, q, 0.0).astype(BF16))
            qdm.append(jnp.where(mine, qd, 0.0).astype(BF16))
            kdt.append(jnp.where(mine, kd, 0.0).astype(BF16))
    vb = [v_ref[:, vslice(h)].astype(BF16) for h in range(heads)]
    inner = [_dot_nt(qm[h], kb[h // 2]) for h in range(heads)]
    cross = [_dot(qdm[h], state_ref[h].astype(BF16)) for h in range(heads)]
    for h in range(heads):
        state_ref[h] = state_ref[h] * cdec_ref[h] + _dot_tn(kdt[h], vb[h])
    out = [_dot((inner[h] * dec_ref[h]).astype(BF16), vb[h]) + cross[h] for h in range(heads)]
    for h in range(heads):
        mu = jnp.mean(out[h], axis=-1, keepdims=True)
        d = out[h] - mu
        var = jnp.mean(d * d, axis=-1, keepdims=True)
        rn = d * lax.rsqrt(var + LN_EPS) * gain_ref[:, vslice(h)]
        g = g_ref[:, vslice(h)]
        o_ref[:, vslice(h)] = (g * _sigmoid(g) * rn).astype(o_ref.dtype)


def _retention(proj, rope, tables, gain, pairs=4):
    t = proj.shape[0]
    n = t // RET_CHUNK
    c, s1, s2 = rope
    dec, kdec, qdec, cdec = tables
    qk_w = pairs * LANES
    v_w = 2 * pairs * RET_V_DIM
    tbl = pl.BlockSpec((RET_CHUNK, LANES), lambda hp, ci: (ci, 0))
    return pl.pallas_call(
        _retention_kernel,
        grid=(RET_HEADS // (2 * pairs), n),
        in_specs=[pl.BlockSpec((RET_CHUNK, qk_w), lambda hp, ci: (ci, OFF_QR // qk_w + hp)),
                  pl.BlockSpec((RET_CHUNK, qk_w), lambda hp, ci: (ci, OFF_KR // qk_w + hp)),
                  pl.BlockSpec((RET_CHUNK, v_w), lambda hp, ci: (ci, OFF_VR // v_w + hp)),
                  pl.BlockSpec((RET_CHUNK, v_w), lambda hp, ci: (ci, OFF_GR // v_w + hp)),
                  tbl, tbl, tbl,
                  pl.BlockSpec((2 * pairs, RET_CHUNK, RET_CHUNK), lambda hp, ci: (hp, 0, 0)),
                  pl.BlockSpec((pairs, RET_CHUNK, LANES), lambda hp, ci: (hp, 0, 0)),
                  pl.BlockSpec((pairs, RET_CHUNK, LANES), lambda hp, ci: (hp, 0, 0)),
                  pl.BlockSpec((2 * pairs, 1, RET_V_DIM), lambda hp, ci: (hp, 0, 0)),
                  pl.BlockSpec((1, v_w), lambda hp, ci: (0, hp))],
        out_specs=pl.BlockSpec((RET_CHUNK, v_w), lambda hp, ci: (ci, hp)),
        out_shape=jax.ShapeDtypeStruct((t, RET_V_WIDTH), BF16),
        scratch_shapes=[pltpu.VMEM((2 * pairs, LANES, RET_V_DIM), F32)],
        compiler_params=_params(2),
        name="retention",
    )(proj, proj, proj, proj, c, s1, s2, dec, kdec, qdec, cdec, gain)


def _pool_kernel(cur_ref, prev_ref, w_ref, scale_ref, o_ref, buf_a, buf_b, *, tb):
    i = pl.program_id(0)
    g = pl.program_id(1)
    x = cur_ref[...]
    buf_a[0:POOL_HALO, :] = jnp.where(i > 0, prev_ref[...], 0.0)
    buf_a[POOL_HALO:, :] = x
    rows = tb + POOL_HALO
    pos = (i * tb + 1 + lax.broadcasted_iota(jnp.int32, (tb, 1), 0)).astype(F32)
    for gi, w in enumerate(POOL_WINDOWS):
        @pl.when(g == gi)
        def _(w=w):
            src, dst = buf_a, buf_b
            k, lo = 1, SUBLANES
            while k < w:
                dst[lo:rows, :] = src[lo:rows, :] + src[lo - k:rows - k, :]
                src, dst = dst, src
                k, lo = 2 * k, lo + SUBLANES
            tot = src[POOL_HALO:rows, :]
            d = tot / jnp.minimum(pos, float(w)) - x
            y = _dot(d.astype(BF16), w_ref[0])
            o_ref[...] = (y * scale_ref[...]).astype(o_ref.dtype)


def _pool(proj, w_pool, layer, scale, tb=2048):
    t = proj.shape[0]
    tb = min(tb, t)
    gd = POOL_GROUP_DIM
    assert SUBLANES * (max(POOL_WINDOWS).bit_length() - 1) <= POOL_HALO
    return pl.pallas_call(
        functools.partial(_pool_kernel, tb=tb),
        grid=(t // tb, POOL_GROUPS),
        in_specs=[pl.BlockSpec((tb, gd), lambda i, g: (i, OFF_POOL // gd + g)),
                  pl.BlockSpec((POOL_HALO, gd),
                               lambda i, g: (jnp.maximum(i * (tb // POOL_HALO) - 1, 0), OFF_POOL // gd + g)),
                  pl.BlockSpec((None, 1, gd, gd), lambda i, g: (layer, g, 0, 0)),
                  pl.BlockSpec((1, gd), lambda i, g: (0, g))],
        out_specs=pl.BlockSpec((tb, gd), lambda i, g: (i, g)),
        out_shape=jax.ShapeDtypeStruct((t, POOL_WIDTH), BF16),
        scratch_shapes=[pltpu.VMEM((tb + POOL_HALO, gd), F32), pltpu.VMEM((tb + POOL_HALO, gd), F32)],
        compiler_params=_params(2),
        name="pool",
    )(proj, proj, w_pool, scale)


def _merge_kernel(ya_ref, yr_ref, yp_ref, w_ref, ga_ref, gr_ref, gp_ref, o_ref):
    acc = ga_ref[...] * _dot(ya_ref[...], w_ref[0])
    acc = acc + gr_ref[...] * _dot(yr_ref[...], w_ref[1])
    acc = acc + gp_ref[...] * _dot(yp_ref[...], w_ref[2])
    o_ref[...] = acc.astype(o_ref.dtype)


def _merge(ya, yr, yp, w_branch, layer, gates, tm=1024, tn=1024):
    t = ya.shape[0]
    d = w_branch.shape[3]
    br = pl.BlockSpec((tm, BRANCH_WIDTH), lambda i, j: (i, 0))

    def gate(nb):
        return pl.BlockSpec((tm, tn), lambda i, j: (i, (nb * d) // tn + j))

    return pl.pallas_call(
        _merge_kernel,
        grid=(t // tm, d // tn),
        in_specs=[br, br, br,
                  pl.BlockSpec((None, N_BRANCH, BRANCH_WIDTH, tn), lambda i, j: (layer, 0, 0, j)),
                  gate(0), gate(1), gate(2)],
        out_specs=pl.BlockSpec((tm, tn), lambda i, j: (i, j)),
        out_shape=jax.ShapeDtypeStruct((t, d), BF16),
        compiler_params=_params(2),
        name="merge",
    )(ya, yr, yp, w_branch, gates, gates, gates)


def _outproj_ln_kernel(m_ref, w_ref, x_ref, g_ref, b_ref, o_ref, *, alpha):
    half = m_ref.shape[0] // 2
    for r in range(2):
        sl = slice(r * half, (r + 1) * half)
        y = _dot(m_ref[sl, :], w_ref[...])
        o_ref[sl, :] = _layer_norm(alpha * x_ref[sl, :] + y, g_ref[...], b_ref[...])


def _outproj_ln(merged, w_out, layer, x, g, b, alpha, tm=512):
    t, d = x.shape
    row = pl.BlockSpec((tm, d), lambda i: (i, 0))
    vec = pl.BlockSpec((1, d), lambda i: (0, 0))
    return pl.pallas_call(
        functools.partial(_outproj_ln_kernel, alpha=alpha),
        grid=(t // tm,),
        in_specs=[row, pl.BlockSpec((None, d, d), lambda i: (layer, 0, 0)), row, vec, vec],
        out_specs=row,
        out_shape=jax.ShapeDtypeStruct((t, d), F32),
        compiler_params=_params(1),
        name="outproj_ln",
    )(merged, w_out, x, g, b)


def _router_kernel(x_ref, wh_ref, wl_ref, b_ref, e_ref, p_ref):
    x = x_ref[...]
    xh, xl = _split_bf16(x)
    z = _dot(xh, wh_ref[...]) + (_dot(xh, wl_ref[...]) + _dot(xl, wh_ref[...])) + b_ref[...]
    lane = lax.broadcasted_iota(jnp.int32, z.shape, 1)
    big = jnp.int32(ROUTER_LANES)

    def masked_softmax(mask):
        zm = jnp.where(mask, z, NEG)
        e = jnp.where(mask, jnp.exp(zm - jnp.max(zm, axis=-1, keepdims=True)), 0.0)
        return e / jnp.sum(e, axis=-1, keepdims=True)

    def top1(p, mask):
        pm = jnp.where(mask, p, -1.0)
        top = jnp.max(pm, axis=-1, keepdims=True)
        idx = jnp.min(jnp.where(pm == top, lane, big), axis=-1, keepdims=True)
        return top, idx

    in_groups = lane < N_GROUPS
    p1 = masked_softmax(in_groups)
    g_top, g_idx = top1(p1, in_groups)
    lo = N_GROUPS + g_idx * EXPERTS_PER_GROUP
    in_group = (lane >= lo) & (lane < lo + EXPERTS_PER_GROUP)
    p2 = masked_softmax(in_group)
    e1, i1 = top1(p2, in_group)
    e2, i2 = top1(p2, in_group & (lane != i1))
    denom = e1 + e2
    w1 = g_top * e1 / denom
    w2 = g_top * e2 / denom
    e_ref[...] = jnp.where(lane == 0, i1 - N_GROUPS, jnp.where(lane == 1, i2 - N_GROUPS, 0))
    p_ref[...] = jnp.where(lane == 0, w1, jnp.where(lane == 1, w2, 0.0))


def _router(x, w_hi, w_lo, bias, tm=1024):
    t, d = x.shape
    row = pl.BlockSpec((tm, d), lambda i: (i, 0))
    wsp = pl.BlockSpec((d, ROUTER_LANES), lambda i: (0, 0))
    out = pl.BlockSpec((tm, ROUTER_LANES), lambda i: (i, 0))
    return pl.pallas_call(
        _router_kernel,
        grid=(t // tm,),
        in_specs=[row, wsp, wsp, pl.BlockSpec((1, ROUTER_LANES), lambda i: (0, 0))],
        out_specs=[out, out],
        out_shape=[jax.ShapeDtypeStruct((t, ROUTER_LANES), jnp.int32),
                   jax.ShapeDtypeStruct((t, ROUTER_LANES), F32)],
        compiler_params=_params(1),
        name="router",
    )(x, w_hi, w_lo, bias)


WAIT_CHUNKS = (64, 8, 1)


def _start_rows(n, copy):
    for r in range(MOE_BLOCK):
        @pl.when(r < n)
        def _(r=r):
            copy(r).start()


def _wait_rows(n, span):
    rem = n
    for c in WAIT_CHUNKS:
        cnt = lax.shift_right_logical(rem, c.bit_length() - 1)

        def one(_, carry, c=c):
            span(c).wait()
            return carry

        lax.fori_loop(0, cnt, one, 0)
        rem = rem & (c - 1)


def _experts_kernel(be_ref, base_ref, nvalid_ref, src_ref, dst_ref, nused_ref,
                    x_hbm, wg_ref, wu_ref, wd_ref, y_hbm,
                    xbuf, ybuf, wgb, wub, wdb, sem_in, sem_out):
    b = pl.program_id(0)
    n_used = nused_ref[0]
    slot = b & 1

    def start_gather(bb, sl):
        base = base_ref[bb]
        _start_rows(nvalid_ref[bb], lambda r: pltpu.make_async_copy(
            x_hbm.at[pl.ds(src_ref[base + r], 1), :], xbuf.at[sl, pl.ds(r, 1), :], sem_in.at[sl]))

    def wait_gather(bb, sl):
        _wait_rows(nvalid_ref[bb], lambda c: pltpu.make_async_copy(
            x_hbm.at[pl.ds(0, c), :], xbuf.at[sl, pl.ds(0, c), :], sem_in.at[sl]))

    def start_scatter(bb, sl):
        base = base_ref[bb]
        _start_rows(nvalid_ref[bb], lambda r: pltpu.make_async_copy(
            ybuf.at[sl, pl.ds(r, 1), :], y_hbm.at[pl.ds(dst_ref[base + r], 1), :], sem_out.at[sl]))

    def wait_scatter(bb, sl):
        _wait_rows(nvalid_ref[bb], lambda c: pltpu.make_async_copy(
            ybuf.at[sl, pl.ds(0, c), :], y_hbm.at[pl.ds(0, c), :], sem_out.at[sl]))

    @pl.when(b == 0)
    def _():
        xbuf[...] = jnp.zeros_like(xbuf)
        start_gather(0, 0)

    @pl.when(b < n_used)
    def _():
        @pl.when(b + 1 < n_used)
        def _():
            start_gather(b + 1, 1 - slot)

        new_expert = (b == 0) | (be_ref[b] != be_ref[jnp.maximum(b - 1, 0)])

        @pl.when(new_expert)
        def _():
            wgb[...] = wg_ref[...].astype(BF16)
            wub[...] = wu_ref[...].astype(BF16)
            wdb[...] = wd_ref[...].astype(BF16)

        wait_gather(b, slot)

        @pl.when(b >= 2)
        def _():
            wait_scatter(b - 2, slot)

        x = xbuf[slot].astype(BF16)
        gate = _dot(x, wgb[...])
        up = _dot(x, wub[...])
        hid = (gate * _sigmoid(gate) * up).astype(BF16)
        ybuf[slot] = _dot(hid, wdb[...])
        start_scatter(b, slot)

        @pl.when(b == n_used - 1)
        def _():
            @pl.when(b >= 1)
            def _():
                wait_scatter(b - 1, 1 - slot)

            wait_scatter(b, slot)


def _experts(x, wg, wu, wd, layer, plan):
    t, d = x.shape
    block_expert, base, nvalid, src, dst, n_used = plan
    n_blocks = block_expert.shape[0]

    def wspec(shape):
        return pl.BlockSpec((None, None) + shape, lambda b, be, *_: (layer, be[b], 0, 0))

    grid_spec = pltpu.PrefetchScalarGridSpec(
        num_scalar_prefetch=6,
        grid=(n_blocks,),
        in_specs=[pl.BlockSpec(memory_space=pl.ANY),
                  wspec((d, D_EXPERT)), wspec((d, D_EXPERT)), wspec((D_EXPERT, d))],
        out_specs=pl.BlockSpec(memory_space=pl.ANY),
        scratch_shapes=[pltpu.VMEM((2, MOE_BLOCK, d), F32),
                        pltpu.VMEM((2, MOE_BLOCK, d), F32),
                        pltpu.VMEM((d, D_EXPERT), BF16),
                        pltpu.VMEM((d, D_EXPERT), BF16),
                        pltpu.VMEM((D_EXPERT, d), BF16),
                        pltpu.SemaphoreType.DMA((2,)),
                        pltpu.SemaphoreType.DMA((2,))],
    )
    y = pl.pallas_call(
        _experts_kernel,
        grid_spec=grid_spec,
        out_shape=jax.ShapeDtypeStruct((TOP_K * t, d), F32),
        compiler_params=_params(1),
        name="experts",
    )(block_expert, base, nvalid, src, dst, n_used, x, wg, wu, wd)
    return y.reshape(TOP_K, t, d)


def _combine_ln_kernel(x_ref, y0_ref, y1_ref, p_ref, g_ref, b_ref, o_ref, ob_ref, *, alpha):
    p = p_ref[...]
    y = p[:, 0:1] * y0_ref[...] + p[:, 1:2] * y1_ref[...]
    out = _layer_norm(alpha * x_ref[...] + y, g_ref[...], b_ref[...])
    o_ref[...] = out
    ob_ref[...] = out.astype(BF16)


def _combine_ln(x, ycomb, probs, g, b, alpha, tm=512):
    t, d = x.shape
    row = pl.BlockSpec((tm, d), lambda i: (i, 0))
    vec = pl.BlockSpec((1, d), lambda i: (0, 0))
    return pl.pallas_call(
        functools.partial(_combine_ln_kernel, alpha=alpha),
        grid=(t // tm,),
        in_specs=[row,
                  pl.BlockSpec((None, tm, d), lambda i: (0, i, 0)),
                  pl.BlockSpec((None, tm, d), lambda i: (1, i, 0)),
                  pl.BlockSpec((tm, ROUTER_LANES), lambda i: (i, 0)), vec, vec],
        out_specs=[row, row],
        out_shape=[jax.ShapeDtypeStruct((t, d), F32), jax.ShapeDtypeStruct((t, d), BF16)],
        compiler_params=_params(1),
        name="combine_ln",
    )(x, ycomb, ycomb, probs, g, b)


def _rope_tables(t, rot_dim, theta, head_dim):
    half = rot_dim // 2
    inv_freq = 1.0 / (theta ** (jnp.arange(0, rot_dim, 2, dtype=F32) / rot_dim))
    ang = jnp.arange(t, dtype=F32)[:, None] * inv_freq[None, :]
    cos, sin = jnp.cos(ang), jnp.sin(ang)
    rest = head_dim - rot_dim
    ones, zeros = jnp.ones((t, rest), F32), jnp.zeros((t, rest), F32)
    zh = jnp.zeros((t, half), F32)
    c = jnp.concatenate([cos, cos, ones], axis=1)
    s1 = jnp.concatenate([zh, sin, zeros], axis=1)
    s2 = jnp.concatenate([-sin, zh, zeros], axis=1)
    rep = LANES // head_dim
    return tuple(jnp.tile(a, (1, rep)) for a in (c, s1, s2))


def _retention_tables():
    h, c = RET_HEADS, RET_CHUNK
    log_gamma = jnp.log1p(-jnp.exp2(-5.0 - jnp.arange(h, dtype=F32)))
    pos = jnp.arange(c, dtype=F32)
    diff = pos[:, None] - pos[None, :]
    decay = jnp.where(diff >= 0, jnp.exp(log_gamma[:, None, None] * jnp.maximum(diff, 0.0)), 0.0)
    k_decay = jnp.exp(log_gamma[:, None] * (c - 1.0 - pos)[None, :])
    q_decay = jnp.exp(log_gamma[:, None] * (pos + 1.0)[None, :])
    chunk_decay = jnp.exp(log_gamma * c)

    def pair_lanes(a):
        a = a.reshape(h // 2, 2, c).transpose(0, 2, 1)
        return jnp.repeat(a, RET_QK_DIM, axis=2)

    cdec = jnp.broadcast_to(chunk_decay[:, None, None], (h, 1, RET_V_DIM))
    return decay, pair_lanes(k_decay), pair_lanes(q_decay), cdec


def _dispatch_plan(expert, t):
    a = t * TOP_K
    flat_e = expert.reshape(a)
    order = jnp.argsort(flat_e).astype(jnp.int32)
    counts = jnp.sum(flat_e[None, :] == jnp.arange(N_EXPERTS, dtype=flat_e.dtype)[:, None],
                     axis=1, dtype=jnp.int32)
    start = jnp.cumsum(counts) - counts
    blocks = (counts + MOE_BLOCK - 1) // MOE_BLOCK
    blk_end = jnp.cumsum(blocks)
    n_blocks = -(-a // MOE_BLOCK) + N_EXPERTS
    b = jnp.arange(n_blocks, dtype=jnp.int32)
    block_expert = jnp.minimum(jnp.sum(blk_end[None, :] <= b[:, None], axis=1, dtype=jnp.int32),
                               N_EXPERTS - 1)
    first = (b - (blk_end - blocks)[block_expert]) * MOE_BLOCK
    base = start[block_expert] + first
    nvalid = jnp.clip(counts[block_expert] - first, 0, MOE_BLOCK)
    nvalid = jnp.where(b < blk_end[-1], nvalid, 0)
    n_used = blk_end[-1:].astype(jnp.int32)
    order = jnp.pad(order, (0, MOE_BLOCK))
    src = order // TOP_K
    dst = (order % TOP_K) * t + src
    return block_expert, base.astype(jnp.int32), nvalid.astype(jnp.int32), src, dst, n_used


def _token_mixer_ln(h, hb, w_in, ret_gain, w_pool, pool_scale, w_branch, w_out, layer, ln_g, ln_b,
                    rope_a, rope_r, ret_tables, alpha):
    proj = _inproj(hb, w_in, layer, 0, OFF_GATE)
    gates = _inproj(hb, w_in, layer, OFF_GATE, N_IN - OFF_GATE, gate=True)
    qt, kk, vt = _moba_prep(proj, rope_a)
    y_a = _moba_attn(qt, kk, vt)
    y_r = _retention(proj, rope_r, ret_tables, ret_gain[None, :])
    y_p = _pool(proj, w_pool, layer, pool_scale[None, :])
    merged = _merge(y_a, y_r, y_p, w_branch, layer, gates)
    return _outproj_ln(merged, w_out, layer, h, ln_g[None, :], ln_b[None, :], alpha)


def _moe_ln(h, wr_hi, wr_lo, b_router, w_gate, w_up, w_down, layer, ln_g, ln_b, alpha):
    e_pad, p_pad = _router(h, wr_hi, wr_lo, b_router)
    plan = _dispatch_plan(e_pad[:, :TOP_K], h.shape[0])
    ycomb = _experts(h, w_gate, w_up, w_down, layer, plan)
    return _combine_ln(h, ycomb, p_pad, ln_g[None, :], ln_b[None, :], alpha)


def kernel(x, w_in, ret_gain, w_pool, pool_scale, w_branch, w_out, ln1_g, ln1_b, w_r1, b_r1, w_r2, b_r2, w_e_gate, w_e_up, w_e_down, ln2_g, ln2_b):
    bsz, seq, d = x.shape
    depth = w_in.shape[0]
    assert bsz == 1 and d == D_MODEL and seq % 1024 == 0
    t = seq
    alpha = float((2 * depth) ** 0.25)

    rope_a = _rope_tables(t, ROPE_DIM, ROPE_THETA, ATT_HEAD_DIM)
    rope_r = _rope_tables(t, RET_QK_DIM, RET_ROPE_THETA, RET_QK_DIM)
    ret_tables = _retention_tables()

    w_router = jnp.concatenate([w_r1, w_r2], axis=2)
    w_router = jnp.pad(w_router, ((0, 0), (0, 0), (0, ROUTER_LANES - w_router.shape[2])))
    wr_hi = w_router.astype(BF16)
    wr_lo = (w_router - wr_hi.astype(F32)).astype(BF16)
    b_router = jnp.pad(jnp.concatenate([b_r1, b_r2], axis=1),
                       ((0, 0), (0, ROUTER_LANES - N_GROUPS - N_EXPERTS)))[:, None, :]

    w_pool_b, w_branch_b, w_out_b = w_pool.astype(BF16), w_branch.astype(BF16), w_out.astype(BF16)
    h = x.reshape(t, d)
    hb = h.astype(BF16)
    for l in range(depth):
        h = _token_mixer_ln(h, hb, w_in, ret_gain[l], w_pool_b, pool_scale[l], w_branch_b, w_out_b, l,
                            ln1_g[l], ln1_b[l], rope_a, rope_r, ret_tables, alpha)
        h, hb = _moe_ln(h, wr_hi[l], wr_lo[l], b_router[l], w_e_gate, w_e_up, w_e_down, l,
                        ln2_g[l], ln2_b[l], alpha)
    return h.reshape(bsz, seq, d)
```

```python
import functools

import jax
import jax.numpy as jnp
from jax import lax
from jax.experimental import pallas as pl
from jax.experimental.pallas import tpu as pltpu

D_MODEL = 2048

ATT_HEADS = 8
ATT_HEAD_DIM = 128
ATT_WIDTH = ATT_HEADS * ATT_HEAD_DIM
ROPE_DIM = ATT_HEAD_DIM // 4
ROPE_THETA = 500000.0
MOBA_BLOCK = 256
MOBA_TOPK = 3

RET_HEADS = 8
RET_QK_DIM = 64
RET_V_DIM = 128
RET_QK_WIDTH = RET_HEADS * RET_QK_DIM
RET_V_WIDTH = RET_HEADS * RET_V_DIM
RET_CHUNK = 256
RET_ROPE_THETA = 10000.0

POOL_WINDOWS = (2, 4, 8, 16)
POOL_GROUPS = 4
POOL_GROUP_DIM = 256
POOL_WIDTH = POOL_GROUPS * POOL_GROUP_DIM
POOL_HALO = 32

N_BRANCH = 3
BRANCH_WIDTH = 1024

OFF_QA = 0
OFF_KA = OFF_QA + ATT_WIDTH
OFF_VA = OFF_KA + ATT_WIDTH
OFF_QR = OFF_VA + ATT_WIDTH
OFF_KR = OFF_QR + RET_QK_WIDTH
OFF_VR = OFF_KR + RET_QK_WIDTH
OFF_GR = OFF_VR + RET_V_WIDTH
OFF_POOL = OFF_GR + RET_V_WIDTH
OFF_GATE = OFF_POOL + POOL_WIDTH
N_IN = OFF_GATE + N_BRANCH * D_MODEL

N_GROUPS = 4
EXPERTS_PER_GROUP = 8
N_EXPERTS = N_GROUPS * EXPERTS_PER_GROUP
TOP_K = 2
D_EXPERT = 512
MOE_BLOCK = 256
ROUTER_LANES = 128

LN_EPS = 1e-5
NEG = -1e30
TAKEN = -3e38
LOG2_E = 1.4426950408889634

LANES = 128
SUBLANES = 8
VMEM_LIMIT = 56 * 1024 * 1024

F32 = jnp.float32
BF16 = jnp.bfloat16


def _params(n_axes):
    return pltpu.CompilerParams(dimension_semantics=("arbitrary",) * n_axes,
                                vmem_limit_bytes=VMEM_LIMIT)


def _dot(a, b):
    return jnp.dot(a, b, preferred_element_type=F32)


def _dot_nt(a, b):
    return lax.dot_general(a, b, (((1,), (1,)), ((), ())), preferred_element_type=F32)


def _dot_tn(a, b):
    return lax.dot_general(a, b, (((0,), (0,)), ((), ())), preferred_element_type=F32)


def _split_bf16(a):
    hi = a.astype(BF16)
    lo = (a - hi.astype(F32)).astype(BF16)
    return hi, lo


def _dot_nt_f32ish(a, b):
    ah, al = _split_bf16(a)
    bh, bl = _split_bf16(b)
    return _dot_nt(ah, bh) + (_dot_nt(ah, bl) + _dot_nt(al, bh))


def _rotate(x, c, s1, s2, shift):
    return x * c + pltpu.roll(x, shift, 1) * s1 + pltpu.roll(x, LANES - shift, 1) * s2


def _sigmoid(x):
    return 0.5 * jnp.tanh(0.5 * x) + 0.5


def _layer_norm(v, g, b):
    mu = jnp.mean(v, axis=-1, keepdims=True)
    d = v - mu
    var = jnp.mean(d * d, axis=-1, keepdims=True)
    return d * lax.rsqrt(var + LN_EPS) * g + b


def _inproj_kernel(x_ref, w_ref, o_ref, wb_ref, *, gate):
    @pl.when(pl.program_id(1) == 0)
    def _():
        wb_ref[...] = w_ref[...].astype(BF16)

    y = _dot(x_ref[...], wb_ref[...])
    o_ref[...] = _sigmoid(y).astype(o_ref.dtype) if gate else y


def _inproj(xb, w, layer, col0, ncols, gate=False, tm=1024, tn=1024):
    t, d = xb.shape
    return pl.pallas_call(
        functools.partial(_inproj_kernel, gate=gate),
        grid=(ncols // tn, t // tm),
        in_specs=[pl.BlockSpec((tm, d), lambda j, i: (i, 0)),
                  pl.BlockSpec((None, d, tn), lambda j, i: (layer, 0, col0 // tn + j))],
        out_specs=pl.BlockSpec((tm, tn), lambda j, i: (i, j)),
        out_shape=jax.ShapeDtypeStruct((t, ncols), BF16 if gate else F32),
        scratch_shapes=[pltpu.VMEM((d, tn), BF16)],
        compiler_params=_params(2),
        name="inproj_gates" if gate else "inproj",
    )(xb, w)


ATT_KDIM = 2 * ATT_HEAD_DIM
ATT_VROWS = ATT_HEAD_DIM + 16
ATT_SLOTS = 4


def _moba_prep_kernel(q_ref, k_ref, v_ref, c_ref, s1_ref, s2_ref, qt_ref, ko_ref, vt_ref, km_ref):
    i = pl.program_id(0)
    nblk = km_ref.shape[0]

    @pl.when(i == 0)
    def _():
        km_ref[...] = jnp.zeros_like(km_ref)

    c, s1, s2 = c_ref[...], s1_ref[...], s2_ref[...]
    blk = lax.broadcasted_iota(jnp.int32, (nblk, MOBA_BLOCK), 0)
    blkf = blk.astype(F32)
    scale = ATT_HEAD_DIM ** -0.5 * LOG2_E
    lane = lax.broadcasted_iota(jnp.int32, (MOBA_BLOCK, ATT_KDIM - ATT_HEAD_DIM), 1)
    this_block = jnp.where(lane == i, 1.0, 0.0).astype(BF16)
    row = lax.broadcasted_iota(jnp.int32, (ATT_VROWS - ATT_HEAD_DIM, MOBA_BLOCK), 0)
    ones_row = jnp.where(row == 0, 1.0, 0.0).astype(BF16)
    pad_rows = ATT_KDIM - ATT_HEAD_DIM - nblk
    eye = jnp.where(lax.broadcasted_iota(jnp.int32, (ATT_HEAD_DIM, ATT_HEAD_DIM), 0)
                    == lax.broadcasted_iota(jnp.int32, (ATT_HEAD_DIM, ATT_HEAD_DIM), 1), 1.0, 0.0).astype(BF16)
    for h in range(ATT_HEADS):
        sl = slice(h * ATT_HEAD_DIM, (h + 1) * ATT_HEAD_DIM)
        qr = _rotate(q_ref[:, sl], c, s1, s2, ROPE_DIM // 2)
        kr = _rotate(k_ref[:, sl], c, s1, s2, ROPE_DIM // 2)
        ko_ref[h, 0, :, :ATT_HEAD_DIM] = kr.astype(BF16)
        ko_ref[h, 0, :, ATT_HEAD_DIM:] = this_block
        km_blk = lax.broadcasted_iota(jnp.int32, (nblk, ATT_HEAD_DIM), 0)
        km_ref[:, sl] = jnp.where(km_blk == i, jnp.mean(kr, axis=0, keepdims=True), km_ref[:, sl])
        qt_ref[h, 0, :ATT_HEAD_DIM, :] = _dot_nt(eye, (qr * scale).astype(BF16)).astype(BF16)
        vt_ref[h, 0, :ATT_HEAD_DIM, :] = _dot_nt(eye, v_ref[:, sl].astype(BF16)).astype(BF16)
        vt_ref[h, 0, ATT_HEAD_DIM:, :] = ones_row
        gate = _dot_nt_f32ish(km_ref[:, sl], qr)
        gate = jnp.where(blk < i, gate, NEG)
        for _ in range(MOBA_TOPK):
            top = jnp.max(gate, axis=0, keepdims=True)
            first = jnp.min(jnp.where(gate == top, blkf, float(nblk)), axis=0, keepdims=True)
            gate = jnp.where(blkf == first, TAKEN, gate)
        bias = jnp.where(gate == TAKEN, jnp.where(blk < i, 0.0, NEG), NEG)
        qt_ref[h, 0, ATT_HEAD_DIM:ATT_HEAD_DIM + nblk, :] = bias.astype(BF16)
        if pad_rows:
            qt_ref[h, 0, ATT_HEAD_DIM + nblk:, :] = jnp.zeros((pad_rows, MOBA_BLOCK), BF16)


def _moba_prep(proj, rope):
    t = proj.shape[0]
    nblk = t // MOBA_BLOCK
    assert ATT_HEAD_DIM + nblk <= ATT_KDIM and nblk % 16 == 0
    c, s1, s2 = rope
    tbl = pl.BlockSpec((MOBA_BLOCK, LANES), lambda i: (i, 0))
    hshape = (ATT_HEADS, nblk)
    return pl.pallas_call(
        _moba_prep_kernel,
        grid=(nblk,),
        in_specs=[pl.BlockSpec((MOBA_BLOCK, ATT_WIDTH), lambda i: (i, OFF_QA // ATT_WIDTH)),
                  pl.BlockSpec((MOBA_BLOCK, ATT_WIDTH), lambda i: (i, OFF_KA // ATT_WIDTH)),
                  pl.BlockSpec((MOBA_BLOCK, ATT_WIDTH), lambda i: (i, OFF_VA // ATT_WIDTH)),
                  tbl, tbl, tbl],
        out_specs=[pl.BlockSpec((ATT_HEADS, 1, ATT_KDIM, MOBA_BLOCK), lambda i: (0, i, 0, 0)),
                   pl.BlockSpec((ATT_HEADS, 1, MOBA_BLOCK, ATT_KDIM), lambda i: (0, i, 0, 0)),
                   pl.BlockSpec((ATT_HEADS, 1, ATT_VROWS, MOBA_BLOCK), lambda i: (0, i, 0, 0))],
        out_shape=[jax.ShapeDtypeStruct(hshape + (ATT_KDIM, MOBA_BLOCK), BF16),
                   jax.ShapeDtypeStruct(hshape + (MOBA_BLOCK, ATT_KDIM), BF16),
                   jax.ShapeDtypeStruct(hshape + (ATT_VROWS, MOBA_BLOCK), BF16)],
        scratch_shapes=[pltpu.VMEM((nblk, ATT_WIDTH), F32)],
        compiler_params=_params(1),
        name="moba_prep",
    )(proj, proj, proj, c, s1, s2)


def _moba_attn_kernel(qt_ref, k_ref, vt_ref, o_ref, s_ref, p_ref, a_ref, acc_ref):
    i = pl.program_id(1)
    heads = qt_ref.shape[0]
    nblk = k_ref.shape[1]
    shape = (MOBA_BLOCK, MOBA_BLOCK)
    causal = lax.broadcasted_iota(jnp.int32, shape, 0) <= lax.broadcasted_iota(jnp.int32, shape, 1)

    ms = []
    for h in range(heads):
        s = _dot(k_ref[h, i, :, :ATT_HEAD_DIM], qt_ref[h, 0, :ATT_HEAD_DIM, :])
        s = jnp.where(causal, s, NEG)
        m = jnp.max(s, axis=0, keepdims=True)
        ms.append(m)
        acc_ref[h] = _dot(vt_ref[h, i], jnp.exp2(s - m).astype(BF16))
        for slot in (2, 3):
            p_ref[slot, h] = jnp.zeros(shape, BF16)
            a_ref[slot, h] = jnp.ones((1, MOBA_BLOCK), F32)

    def scores2(j, slot):
        jc = jnp.minimum(j, nblk - 2)
        for h in range(heads):
            kk = k_ref[h, pl.ds(jc, 2)].reshape(2 * MOBA_BLOCK, ATT_KDIM)
            s2 = _dot(kk, qt_ref[h, 0])
            s_ref[slot, h] = s2[:MOBA_BLOCK]
            s_ref[slot + 1, h] = s2[MOBA_BLOCK:]

    def pv(j, slot):
        jc = jnp.clip(j, 0, nblk - 1)
        for h in range(heads):
            acc_ref[h] = a_ref[slot, h] * acc_ref[h] + _dot(vt_ref[h, jc], p_ref[slot, h])

    def softmax(slot, ms):
        out = []
        for h in range(heads):
            s = s_ref[slot, h]
            m_new = jnp.maximum(ms[h], jnp.max(s, axis=0, keepdims=True))
            a_ref[slot, h] = jnp.exp2(ms[h] - m_new)
            p_ref[slot, h] = jnp.exp2(s - m_new).astype(BF16)
            out.append(m_new)
        return out

    scores2(0, 0)

    def body(jj, ms):
        b = 4 * jj
        pv(b - 2, 2)
        pv(b - 1, 3)
        scores2(b + 2, 2)
        ms = softmax(0, ms)
        ms = softmax(1, ms)
        pv(b, 0)
        pv(b + 1, 1)
        scores2(b + 4, 0)
        ms = softmax(2, ms)
        return softmax(3, ms)

    trips = lax.shift_right_logical(i + 3, 2)
    lax.fori_loop(0, trips, body, ms)
    pv(4 * trips - 2, 2)
    pv(4 * trips - 1, 3)
    for h in range(heads):
        acc = acc_ref[h]
        out = acc[:ATT_HEAD_DIM] / acc[ATT_HEAD_DIM:ATT_HEAD_DIM + 1]
        o_ref[:, h * ATT_HEAD_DIM:(h + 1) * ATT_HEAD_DIM] = out.T.astype(o_ref.dtype)


def _moba_attn(qt, k, vt, heads=2):
    nblk = qt.shape[1]
    t = nblk * MOBA_BLOCK
    blk2 = (MOBA_BLOCK, MOBA_BLOCK)
    return pl.pallas_call(
        _moba_attn_kernel,
        grid=(ATT_HEADS // heads, nblk),
        in_specs=[pl.BlockSpec((heads, 1, ATT_KDIM, MOBA_BLOCK), lambda h, i: (h, i, 0, 0)),
                  pl.BlockSpec((heads, nblk, MOBA_BLOCK, ATT_KDIM), lambda h, i: (h, 0, 0, 0)),
                  pl.BlockSpec((heads, nblk, ATT_VROWS, MOBA_BLOCK), lambda h, i: (h, 0, 0, 0))],
        out_specs=pl.BlockSpec((MOBA_BLOCK, heads * ATT_HEAD_DIM), lambda h, i: (i, h)),
        out_shape=jax.ShapeDtypeStruct((t, ATT_WIDTH), BF16),
        scratch_shapes=[pltpu.VMEM((ATT_SLOTS, heads) + blk2, F32),
                        pltpu.VMEM((ATT_SLOTS, heads) + blk2, BF16),
                        pltpu.VMEM((ATT_SLOTS, heads, 1, MOBA_BLOCK), F32),
                        pltpu.VMEM((heads, ATT_VROWS, MOBA_BLOCK), F32)],
        compiler_params=_params(2),
        name="moba_attn",
    )(qt, k, vt)


def _retention_kernel(q_ref, k_ref, v_ref, g_ref, c_ref, s1_ref, s2_ref,
                      dec_ref, kdec_ref, qdec_ref, cdec_ref, gain_ref, o_ref, state_ref):
    @pl.when(pl.program_id(1) == 0)
    def _():
        state_ref[...] = jnp.zeros_like(state_ref)

    pairs = kdec_ref.shape[0]
    heads = 2 * pairs
    c, s1, s2 = c_ref[...], s1_ref[...], s2_ref[...]
    lane = lax.broadcasted_iota(jnp.int32, (RET_CHUNK, LANES), 1)

    def vslice(h):
        return slice(h * RET_V_DIM, (h + 1) * RET_V_DIM)

    qm, qdm, kb, kdt = [], [], [], []
    for p in range(pairs):
        psl = slice(p * LANES, (p + 1) * LANES)
        q = _rotate(q_ref[:, psl], c, s1, s2, RET_QK_DIM // 2)
        k = _rotate(k_ref[:, psl], c, s1, s2, RET_QK_DIM // 2) * (RET_QK_DIM ** -0.5)
        qd = q * qdec_ref[p]
        kd = k * kdec_ref[p]
        kb.append(k.astype(BF16))
        for hh in range(2):
            mine = (lane < RET_QK_DIM) if hh == 0 else (lane >= RET_QK_DIM)
            qm.append(jnp.where(mine, q, 0.0).astype(BF16))
            qdm.append(jnp.where(mine, qd, 0.0).astype(BF16))
            kdt.append(jnp.where(mine, kd, 0.0).astype(BF16))
    vb = [v_ref[:, vslice(h)].astype(BF16) for h in range(heads)]
    inner = [_dot_nt(qm[h], kb[h // 2]) for h in range(heads)]
    cross = [_dot(qdm[h], state_ref[h].astype(BF16)) for h in range(heads)]
    for h in range(heads):
        state_ref[h] = state_ref[h] * cdec_ref[h] + _dot_tn(kdt[h], vb[h])
    out = [_dot((inner[h] * dec_ref[h]).astype(BF16), vb[h]) + cross[h] for h in range(heads)]
    for h in range(heads):
        mu = jnp.mean(out[h], axis=-1, keepdims=True)
        d = out[h] - mu
        var = jnp.mean(d * d, axis=-1, keepdims=True)
        rn = d * lax.rsqrt(var + LN_EPS) * gain_ref[:, vslice(h)]
        g = g_ref[:, vslice(h)]
        o_ref[:, vslice(h)] = (g * _sigmoid(g) * rn).astype(o_ref.dtype)


def _retention(proj, rope, tables, gain, pairs=4):
    t = proj.shape[0]
    n = t // RET_CHUNK
    c, s1, s2 = rope
    dec, kdec, qdec, cdec = tables
    qk_w = pairs * LANES
    v_w = 2 * pairs * RET_V_DIM
    tbl = pl.BlockSpec((RET_CHUNK, LANES), lambda hp, ci: (ci, 0))
    return pl.pallas_call(
        _retention_kernel,
        grid=(RET_HEADS // (2 * pairs), n),
        in_specs=[pl.BlockSpec((RET_CHUNK, qk_w), lambda hp, ci: (ci, OFF_QR // qk_w + hp)),
                  pl.BlockSpec((RET_CHUNK, qk_w), lambda hp, ci: (ci, OFF_KR // qk_w + hp)),
                  pl.BlockSpec((RET_CHUNK, v_w), lambda hp, ci: (ci, OFF_VR // v_w + hp)),
                  pl.BlockSpec((RET_CHUNK, v_w), lambda hp, ci: (ci, OFF_GR // v_w + hp)),
                  tbl, tbl, tbl,
                  pl.BlockSpec((2 * pairs, RET_CHUNK, RET_CHUNK), lambda hp, ci: (hp, 0, 0)),
                  pl.BlockSpec((pairs, RET_CHUNK, LANES), lambda hp, ci: (hp, 0, 0)),
                  pl.BlockSpec((pairs, RET_CHUNK, LANES), lambda hp, ci: (hp, 0, 0)),
                  pl.BlockSpec((2 * pairs, 1, RET_V_DIM), lambda hp, ci: (hp, 0, 0)),
                  pl.BlockSpec((1, v_w), lambda hp, ci: (0, hp))],
        out_specs=pl.BlockSpec((RET_CHUNK, v_w), lambda hp, ci: (ci, hp)),
        out_shape=jax.ShapeDtypeStruct((t, RET_V_WIDTH), BF16),
        scratch_shapes=[pltpu.VMEM((2 * pairs, LANES, RET_V_DIM), F32)],
        compiler_params=_params(2),
        name="retention",
    )(proj, proj, proj, proj, c, s1, s2, dec, kdec, qdec, cdec, gain)


def _pool_kernel(cur_ref, prev_ref, w_ref, scale_ref, o_ref, buf_a, buf_b, *, tb):
    i = pl.program_id(0)
    g = pl.program_id(1)
    x = cur_ref[...]
    buf_a[0:POOL_HALO, :] = jnp.where(i > 0, prev_ref[...], 0.0)
    buf_a[POOL_HALO:, :] = x
    rows = tb + POOL_HALO
    pos = (i * tb + 1 + lax.broadcasted_iota(jnp.int32, (tb, 1), 0)).astype(F32)
    for gi, w in enumerate(POOL_WINDOWS):
        @pl.when(g == gi)
        def _(w=w):
            src, dst = buf_a, buf_b
            k, lo = 1, SUBLANES
            while k < w:
                dst[lo:rows, :] = src[lo:rows, :] + src[lo - k:rows - k, :]
                src, dst = dst, src
                k, lo = 2 * k, lo + SUBLANES
            tot = src[POOL_HALO:rows, :]
            d = tot / jnp.minimum(pos, float(w)) - x
            y = _dot(d.astype(BF16), w_ref[0])
            o_ref[...] = (y * scale_ref[...]).astype(o_ref.dtype)


def _pool(proj, w_pool, layer, scale, tb=2048):
    t = proj.shape[0]
    tb = min(tb, t)
    gd = POOL_GROUP_DIM
    assert SUBLANES * (max(POOL_WINDOWS).bit_length() - 1) <= POOL_HALO
    return pl.pallas_call(
        functools.partial(_pool_kernel, tb=tb),
        grid=(t // tb, POOL_GROUPS),
        in_specs=[pl.BlockSpec((tb, gd), lambda i, g: (i, OFF_POOL // gd + g)),
                  pl.BlockSpec((POOL_HALO, gd),
                               lambda i, g: (jnp.maximum(i * (tb // POOL_HALO) - 1, 0), OFF_POOL // gd + g)),
                  pl.BlockSpec((None, 1, gd, gd), lambda i, g: (layer, g, 0, 0)),
                  pl.BlockSpec((1, gd), lambda i, g: (0, g))],
        out_specs=pl.BlockSpec((tb, gd), lambda i, g: (i, g)),
        out_shape=jax.ShapeDtypeStruct((t, POOL_WIDTH), BF16),
        scratch_shapes=[pltpu.VMEM((tb + POOL_HALO, gd), F32), pltpu.VMEM((tb + POOL_HALO, gd), F32)],
        compiler_params=_params(2),
        name="pool",
    )(proj, proj, w_pool, scale)


def _merge_kernel(ya_ref, yr_ref, yp_ref, w_ref, ga_ref, gr_ref, gp_ref, o_ref):
    acc = ga_ref[...] * _dot(ya_ref[...], w_ref[0])
    acc = acc + gr_ref[...] * _dot(yr_ref[...], w_ref[1])
    acc = acc + gp_ref[...] * _dot(yp_ref[...], w_ref[2])
    o_ref[...] = acc.astype(o_ref.dtype)


def _merge(ya, yr, yp, w_branch, layer, gates, tm=1024, tn=1024):
    t = ya.shape[0]
    d = w_branch.shape[3]
    br = pl.BlockSpec((tm, BRANCH_WIDTH), lambda i, j: (i, 0))

    def gate(nb):
        return pl.BlockSpec((tm, tn), lambda i, j: (i, (nb * d) // tn + j))

    return pl.pallas_call(
        _merge_kernel,
        grid=(t // tm, d // tn),
        in_specs=[br, br, br,
                  pl.BlockSpec((None, N_BRANCH, BRANCH_WIDTH, tn), lambda i, j: (layer, 0, 0, j)),
                  gate(0), gate(1), gate(2)],
        out_specs=pl.BlockSpec((tm, tn), lambda i, j: (i, j)),
        out_shape=jax.ShapeDtypeStruct((t, d), BF16),
        compiler_params=_params(2),
        name="merge",
    )(ya, yr, yp, w_branch, gates, gates, gates)


def _outproj_ln_kernel(m_ref, w_ref, x_ref, g_ref, b_ref, o_ref, *, alpha):
    half = m_ref.shape[0] // 2
    for r in range(2):
        sl = slice(r * half, (r + 1) * half)
        y = _dot(m_ref[sl, :], w_ref[...])
        o_ref[sl, :] = _layer_norm(alpha * x_ref[sl, :] + y, g_ref[...], b_ref[...])


def _outproj_ln(merged, w_out, layer, x, g, b, alpha, tm=512):
    t, d = x.shape
    row = pl.BlockSpec((tm, d), lambda i: (i, 0))
    vec = pl.BlockSpec((1, d), lambda i: (0, 0))
    return pl.pallas_call(
        functools.partial(_outproj_ln_kernel, alpha=alpha),
        grid=(t // tm,),
        in_specs=[row, pl.BlockSpec((None, d, d), lambda i: (layer, 0, 0)), row, vec, vec],
        out_specs=row,
        out_shape=jax.ShapeDtypeStruct((t, d), F32),
        compiler_params=_params(1),
        name="outproj_ln",
    )(merged, w_out, x, g, b)


def _router_kernel(x_ref, wh_ref, wl_ref, b_ref, e_ref, p_ref):
    x = x_ref[...]
    xh, xl = _split_bf16(x)
    z = _dot(xh, wh_ref[...]) + (_dot(xh, wl_ref[...]) + _dot(xl, wh_ref[...])) + b_ref[...]
    lane = lax.broadcasted_iota(jnp.int32, z.shape, 1)
    big = jnp.int32(ROUTER_LANES)

    def masked_softmax(mask):
        zm = jnp.where(mask, z, NEG)
        e = jnp.where(mask, jnp.exp(zm - jnp.max(zm, axis=-1, keepdims=True)), 0.0)
        return e / jnp.sum(e, axis=-1, keepdims=True)

    def top1(p, mask):
        pm = jnp.where(mask, p, -1.0)
        top = jnp.max(pm, axis=-1, keepdims=True)
        idx = jnp.min(jnp.where(pm == top, lane, big), axis=-1, keepdims=True)
        return top, idx

    in_groups = lane < N_GROUPS
    p1 = masked_softmax(in_groups)
    g_top, g_idx = top1(p1, in_groups)
    lo = N_GROUPS + g_idx * EXPERTS_PER_GROUP
    in_group = (lane >= lo) & (lane < lo + EXPERTS_PER_GROUP)
    p2 = masked_softmax(in_group)
    e1, i1 = top1(p2, in_group)
    e2, i2 = top1(p2, in_group & (lane != i1))
    denom = e1 + e2
    w1 = g_top * e1 / denom
    w2 = g_top * e2 / denom
    e_ref[...] = jnp.where(lane == 0, i1 - N_GROUPS, jnp.where(lane == 1, i2 - N_GROUPS, 0))
    p_ref[...] = jnp.where(lane == 0, w1, jnp.where(lane == 1, w2, 0.0))


def _router(x, w_hi, w_lo, bias, tm=1024):
    t, d = x.shape
    row = pl.BlockSpec((tm, d), lambda i: (i, 0))
    wsp = pl.BlockSpec((d, ROUTER_LANES), lambda i: (0, 0))
    out = pl.BlockSpec((tm, ROUTER_LANES), lambda i: (i, 0))
    return pl.pallas_call(
        _router_kernel,
        grid=(t // tm,),
        in_specs=[row, wsp, wsp, pl.BlockSpec((1, ROUTER_LANES), lambda i: (0, 0))],
        out_specs=[out, out],
        out_shape=[jax.ShapeDtypeStruct((t, ROUTER_LANES), jnp.int32),
                   jax.ShapeDtypeStruct((t, ROUTER_LANES), F32)],
        compiler_params=_params(1),
        name="router",
    )(x, w_hi, w_lo, bias)


WAIT_CHUNKS = (64, 8, 1)


def _start_rows(n, copy):
    for r in range(MOE_BLOCK):
        @pl.when(r < n)
        def _(r=r):
            copy(r).start()


def _wait_rows(n, span):
    rem = n
    for c in WAIT_CHUNKS:
        cnt = lax.shift_right_logical(rem, c.bit_length() - 1)

        def one(_, carry, c=c):
            span(c).wait()
            return carry

        lax.fori_loop(0, cnt, one, 0)
        rem = rem & (c - 1)


def _experts_kernel(be_ref, base_ref, nvalid_ref, src_ref, dst_ref, nused_ref,
                    x_hbm, wg_ref, wu_ref, wd_ref, y_hbm,
                    xbuf, ybuf, wgb, wub, wdb, sem_in, sem_out):
    b = pl.program_id(0)
    n_used = nused_ref[0]
    slot = b & 1

    def start_gather(bb, sl):
        base = base_ref[bb]
        _start_rows(nvalid_ref[bb], lambda r: pltpu.make_async_copy(
            x_hbm.at[pl.ds(src_ref[base + r], 1), :], xbuf.at[sl, pl.ds(r, 1), :], sem_in.at[sl]))

    def wait_gather(bb, sl):
        _wait_rows(nvalid_ref[bb], lambda c: pltpu.make_async_copy(
            x_hbm.at[pl.ds(0, c), :], xbuf.at[sl, pl.ds(0, c), :], sem_in.at[sl]))

    def start_scatter(bb, sl):
        base = base_ref[bb]
        _start_rows(nvalid_ref[bb], lambda r: pltpu.make_async_copy(
            ybuf.at[sl, pl.ds(r, 1), :], y_hbm.at[pl.ds(dst_ref[base + r], 1), :], sem_out.at[sl]))

    def wait_scatter(bb, sl):
        _wait_rows(nvalid_ref[bb], lambda c: pltpu.make_async_copy(
            ybuf.at[sl, pl.ds(0, c), :], y_hbm.at[pl.ds(0, c), :], sem_out.at[sl]))

    @pl.when(b == 0)
    def _():
        xbuf[...] = jnp.zeros_like(xbuf)
        start_gather(0, 0)

    @pl.when(b < n_used)
    def _():
        @pl.when(b + 1 < n_used)
        def _():
            start_gather(b + 1, 1 - slot)

        new_expert = (b == 0) | (be_ref[b] != be_ref[jnp.maximum(b - 1, 0)])

        @pl.when(new_expert)
        def _():
            wgb[...] = wg_ref[...].astype(BF16)
            wub[...] = wu_ref[...].astype(BF16)
            wdb[...] = wd_ref[...].astype(BF16)

        wait_gather(b, slot)

        @pl.when(b >= 2)
        def _():
            wait_scatter(b - 2, slot)

        x = xbuf[slot].astype(BF16)
        gate = _dot(x, wgb[...])
        up = _dot(x, wub[...])
        hid = (gate * _sigmoid(gate) * up).astype(BF16)
        ybuf[slot] = _dot(hid, wdb[...])
        start_scatter(b, slot)

        @pl.when(b == n_used - 1)
        def _():
            @pl.when(b >= 1)
            def _():
                wait_scatter(b - 1, 1 - slot)

            wait_scatter(b, slot)


def _experts(x, wg, wu, wd, layer, plan):
    t, d = x.shape
    block_expert, base, nvalid, src, dst, n_used = plan
    n_blocks = block_expert.shape[0]

    def wspec(shape):
        return pl.BlockSpec((None, None) + shape, lambda b, be, *_: (layer, be[b], 0, 0))

    grid_spec = pltpu.PrefetchScalarGridSpec(
        num_scalar_prefetch=6,
        grid=(n_blocks,),
        in_specs=[pl.BlockSpec(memory_space=pl.ANY),
                  wspec((d, D_EXPERT)), wspec((d, D_EXPERT)), wspec((D_EXPERT, d))],
        out_specs=pl.BlockSpec(memory_space=pl.ANY),
        scratch_shapes=[pltpu.VMEM((2, MOE_BLOCK, d), F32),
                        pltpu.VMEM((2, MOE_BLOCK, d), F32),
                        pltpu.VMEM((d, D_EXPERT), BF16),
                        pltpu.VMEM((d, D_EXPERT), BF16),
                        pltpu.VMEM((D_EXPERT, d), BF16),
                        pltpu.SemaphoreType.DMA((2,)),
                        pltpu.SemaphoreType.DMA((2,))],
    )
    y = pl.pallas_call(
        _experts_kernel,
        grid_spec=grid_spec,
        out_shape=jax.ShapeDtypeStruct((TOP_K * t, d), F32),
        compiler_params=_params(1),
        name="experts",
    )(block_expert, base, nvalid, src, dst, n_used, x, wg, wu, wd)
    return y.reshape(TOP_K, t, d)


def _combine_ln_kernel(x_ref, y0_ref, y1_ref, p_ref, g_ref, b_ref, o_ref, ob_ref, *, alpha):
    p = p_ref[...]
    y = p[:, 0:1] * y0_ref[...] + p[:, 1:2] * y1_ref[...]
    out = _layer_norm(alpha * x_ref[...] + y, g_ref[...], b_ref[...])
    o_ref[...] = out
    ob_ref[...] = out.astype(BF16)


def _combine_ln(x, ycomb, probs, g, b, alpha, tm=512):
    t, d = x.shape
    row = pl.BlockSpec((tm, d), lambda i: (i, 0))
    vec = pl.BlockSpec((1, d), lambda i: (0, 0))
    return pl.pallas_call(
        functools.partial(_combine_ln_kernel, alpha=alpha),
        grid=(t // tm,),
        in_specs=[row,
                  pl.BlockSpec((None, tm, d), lambda i: (0, i, 0)),
                  pl.BlockSpec((None, tm, d), lambda i: (1, i, 0)),
                  pl.BlockSpec((tm, ROUTER_LANES), lambda i: (i, 0)), vec, vec],
        out_specs=[row, row],
        out_shape=[jax.ShapeDtypeStruct((t, d), F32), jax.ShapeDtypeStruct((t, d), BF16)],
        compiler_params=_params(1),
        name="combine_ln",
    )(x, ycomb, ycomb, probs, g, b)


def _rope_tables(t, rot_dim, theta, head_dim):
    half = rot_dim // 2
    inv_freq = 1.0 / (theta ** (jnp.arange(0, rot_dim, 2, dtype=F32) / rot_dim))
    ang = jnp.arange(t, dtype=F32)[:, None] * inv_freq[None, :]
    cos, sin = jnp.cos(ang), jnp.sin(ang)
    rest = head_dim - rot_dim
    ones, zeros = jnp.ones((t, rest), F32), jnp.zeros((t, rest), F32)
    zh = jnp.zeros((t, half), F32)
    c = jnp.concatenate([cos, cos, ones], axis=1)
    s1 = jnp.concatenate([zh, sin, zeros], axis=1)
    s2 = jnp.concatenate([-sin, zh, zeros], axis=1)
    rep = LANES // head_dim
    return tuple(jnp.tile(a, (1, rep)) for a in (c, s1, s2))


def _retention_tables():
    h, c = RET_HEADS, RET_CHUNK
    log_gamma = jnp.log1p(-jnp.exp2(-5.0 - jnp.arange(h, dtype=F32)))
    pos = jnp.arange(c, dtype=F32)
    diff = pos[:, None] - pos[None, :]
    decay = jnp.where(diff >= 0, jnp.exp(log_gamma[:, None, None] * jnp.maximum(diff, 0.0)), 0.0)
    k_decay = jnp.exp(log_gamma[:, None] * (c - 1.0 - pos)[None, :])
    q_decay = jnp.exp(log_gamma[:, None] * (pos + 1.0)[None, :])
    chunk_decay = jnp.exp(log_gamma * c)

    def pair_lanes(a):
        a = a.reshape(h // 2, 2, c).transpose(0, 2, 1)
        return jnp.repeat(a, RET_QK_DIM, axis=2)

    cdec = jnp.broadcast_to(chunk_decay[:, None, None], (h, 1, RET_V_DIM))
    return decay, pair_lanes(k_decay), pair_lanes(q_decay), cdec


def _dispatch_plan(expert, t):
    a = t * TOP_K
    flat_e = expert.reshape(a)
    order = jnp.argsort(flat_e).astype(jnp.int32)
    counts = jnp.sum(flat_e[None, :] == jnp.arange(N_EXPERTS, dtype=flat_e.dtype)[:, None],
                     axis=1, dtype=jnp.int32)
    start = jnp.cumsum(counts) - counts
    blocks = (counts + MOE_BLOCK - 1) // MOE_BLOCK
    blk_end = jnp.cumsum(blocks)
    n_blocks = -(-a // MOE_BLOCK) + N_EXPERTS
    b = jnp.arange(n_blocks, dtype=jnp.int32)
    block_expert = jnp.minimum(jnp.sum(blk_end[None, :] <= b[:, None], axis=1, dtype=jnp.int32),
                               N_EXPERTS - 1)
    first = (b - (blk_end - blocks)[block_expert]) * MOE_BLOCK
    base = start[block_expert] + first
    nvalid = jnp.clip(counts[block_expert] - first, 0, MOE_BLOCK)
    nvalid = jnp.where(b < blk_end[-1], nvalid, 0)
    n_used = blk_end[-1:].astype(jnp.int32)
    order = jnp.pad(order, (0, MOE_BLOCK))
    src = order // TOP_K
    dst = (order % TOP_K) * t + src
    return block_expert, base.astype(jnp.int32), nvalid.astype(jnp.int32), src, dst, n_used


def _token_mixer_ln(h, hb, w_in, ret_gain, w_pool, pool_scale, w_branch, w_out, layer, ln_g, ln_b,
                    rope_a, rope_r, ret_tables, alpha):
    proj = _inproj(hb, w_in, layer, 0, OFF_GATE)
    gates = _inproj(hb, w_in, layer, OFF_GATE, N_IN - OFF_GATE, gate=True)
    qt, kk, vt = _moba_prep(proj, rope_a)
    y_a = _moba_attn(qt, kk, vt)
    y_r = _retention(proj, rope_r, ret_tables, ret_gain[None, :])
    y_p = _pool(proj, w_pool, layer, pool_scale[None, :])
    merged = _merge(y_a, y_r, y_p, w_branch, layer, gates)
    return _outproj_ln(merged, w_out, layer, h, ln_g[None, :], ln_b[None, :], alpha)


def _moe_ln(h, wr_hi, wr_lo, b_router, w_gate, w_up, w_down, layer, ln_g, ln_b, alpha):
    e_pad, p_pad = _router(h, wr_hi, wr_lo, b_router)
    plan = _dispatch_plan(e_pad[:, :TOP_K], h.shape[0])
    ycomb = _experts(h, w_gate, w_up, w_down, layer, plan)
    return _combine_ln(h, ycomb, p_pad, ln_g[None, :], ln_b[None, :], alpha)


def kernel(x, w_in, ret_gain, w_pool, pool_scale, w_branch, w_out, ln1_g, ln1_b, w_r1, b_r1, w_r2, b_r2, w_e_gate, w_e_up, w_e_down, ln2_g, ln2_b):
    bsz, seq, d = x.shape
    depth = w_in.shape[0]
    assert bsz == 1 and d == D_MODEL and seq % 1024 == 0
    t = seq
    alpha = float((2 * depth) ** 0.25)

    rope_a = _rope_tables(t, ROPE_DIM, ROPE_THETA, ATT_HEAD_DIM)
    rope_r = _rope_tables(t, RET_QK_DIM, RET_ROPE_THETA, RET_QK_DIM)
    ret_tables = _retention_tables()

    w_router = jnp.concatenate([w_r1, w_r2], axis=2)
    w_router = jnp.pad(w_router, ((0, 0), (0, 0), (0, ROUTER_LANES - w_router.shape[2])))
    wr_hi = w_router.astype(BF16)
    wr_lo = (w_router - wr_hi.astype(F32)).astype(BF16)
    b_router = jnp.pad(jnp.concatenate([b_r1, b_r2], axis=1),
                       ((0, 0), (0, ROUTER_LANES - N_GROUPS - N_EXPERTS)))[:, None, :]

    w_pool_b, w_branch_b, w_out_b = w_pool.astype(BF16), w_branch.astype(BF16), w_out.astype(BF16)
    h = x.reshape(t, d)
    hb = h.astype(BF16)
    for l in range(depth):
        h = _token_mixer_ln(h, hb, w_in, ret_gain[l], w_pool_b, pool_scale[l], w_branch_b, w_out_b, l,
                            ln1_g[l], ln1_b[l], rope_a, rope_r, ret_tables, alpha)
        h, hb = _moe_ln(h, wr_hi[l], wr_lo[l], b_router[l], w_e_gate, w_e_up, w_e_down, l,
                        ln2_g[l], ln2_b[l], alpha)
    return h.reshape(bsz, seq, d)
```

```python
import functools

import jax
import jax.numpy as jnp
from jax import lax
from jax.experimental import pallas as pl
from jax.experimental.pallas import tpu as pltpu

D_MODEL = 2048

ATT_HEADS = 8
ATT_HEAD_DIM = 128
ATT_WIDTH = ATT_HEADS * ATT_HEAD_DIM
ROPE_DIM = ATT_HEAD_DIM // 4
ROPE_THETA = 500000.0
MOBA_BLOCK = 256
MOBA_TOPK = 3

RET_HEADS = 8
RET_QK_DIM = 64
RET_V_DIM = 128
RET_QK_WIDTH = RET_HEADS * RET_QK_DIM
RET_V_WIDTH = RET_HEADS * RET_V_DIM
RET_CHUNK = 256
RET_ROPE_THETA = 10000.0

POOL_WINDOWS = (2, 4, 8, 16)
POOL_GROUPS = 4
POOL_GROUP_DIM = 256
POOL_WIDTH = POOL_GROUPS * POOL_GROUP_DIM
POOL_HALO = 32

N_BRANCH = 3
BRANCH_WIDTH = 1024

OFF_QA = 0
OFF_KA = OFF_QA + ATT_WIDTH
OFF_VA = OFF_KA + ATT_WIDTH
OFF_QR = OFF_VA + ATT_WIDTH
OFF_KR = OFF_QR + RET_QK_WIDTH
OFF_VR = OFF_KR + RET_QK_WIDTH
OFF_GR = OFF_VR + RET_V_WIDTH
OFF_POOL = OFF_GR + RET_V_WIDTH
OFF_GATE = OFF_POOL + POOL_WIDTH
N_IN = OFF_GATE + N_BRANCH * D_MODEL

N_GROUPS = 4
EXPERTS_PER_GROUP = 8
N_EXPERTS = N_GROUPS * EXPERTS_PER_GROUP
TOP_K = 2
D_EXPERT = 512
MOE_BLOCK = 256
ROUTER_LANES = 128

LN_EPS = 1e-5
NEG = -1e30
TAKEN = -3e38
LOG2_E = 1.4426950408889634

LANES = 128
SUBLANES = 8
VMEM_LIMIT = 56 * 1024 * 1024

F32 = jnp.float32
BF16 = jnp.bfloat16


def _params(n_axes):
    return pltpu.CompilerParams(dimension_semantics=("arbitrary",) * n_axes,
                                vmem_limit_bytes=VMEM_LIMIT)


def _dot(a, b):
    return jnp.dot(a, b, preferred_element_type=F32)


def _dot_nt(a, b):
    return lax.dot_general(a, b, (((1,), (1,)), ((), ())), preferred_element_type=F32)


def _dot_tn(a, b):
    return lax.dot_general(a, b, (((0,), (0,)), ((), ())), preferred_element_type=F32)


def _split_bf16(a):
    hi = a.astype(BF16)
    lo = (a - hi.astype(F32)).astype(BF16)
    return hi, lo


def _dot_nt_f32ish(a, b):
    ah, al = _split_bf16(a)
    bh, bl = _split_bf16(b)
    return _dot_nt(ah, bh) + (_dot_nt(ah, bl) + _dot_nt(al, bh))


def _rotate(x, c, s1, s2, shift):
    return x * c + pltpu.roll(x, shift, 1) * s1 + pltpu.roll(x, LANES - shift, 1) * s2


def _sigmoid(x):
    return 0.5 * jnp.tanh(0.5 * x) + 0.5


def _layer_norm(v, g, b):
    mu = jnp.mean(v, axis=-1, keepdims=True)
    d = v - mu
    var = jnp.mean(d * d, axis=-1, keepdims=True)
    return d * lax.rsqrt(var + LN_EPS) * g + b


def _inproj_kernel(x_ref, w_ref, o_ref, wb_ref, *, gate):
    @pl.when(pl.program_id(1) == 0)
    def _():
        wb_ref[...] = w_ref[...].astype(BF16)

    y = _dot(x_ref[...], wb_ref[...])
    o_ref[...] = _sigmoid(y).astype(o_ref.dtype) if gate else y


def _inproj(xb, w, layer, col0, ncols, gate=False, tm=1024, tn=1024):
    t, d = xb.shape
    return pl.pallas_call(
        functools.partial(_inproj_kernel, gate=gate),
        grid=(ncols // tn, t // tm),
        in_specs=[pl.BlockSpec((tm, d), lambda j, i: (i, 0)),
                  pl.BlockSpec((None, d, tn), lambda j, i: (layer, 0, col0 // tn + j))],
        out_specs=pl.BlockSpec((tm, tn), lambda j, i: (i, j)),
        out_shape=jax.ShapeDtypeStruct((t, ncols), BF16 if gate else F32),
        scratch_shapes=[pltpu.VMEM((d, tn), BF16)],
        compiler_params=_params(2),
        name="inproj_gates" if gate else "inproj",
    )(xb, w)


ATT_KDIM = 2 * ATT_HEAD_DIM
ATT_VROWS = ATT_HEAD_DIM + 16
ATT_SLOTS = 4


def _moba_prep_kernel(q_ref, k_ref, v_ref, c_ref, s1_ref, s2_ref, qt_ref, ko_ref, vt_ref, km_ref):
    i = pl.program_id(0)
    nblk = km_ref.shape[0]

    @pl.when(i == 0)
    def _():
        km_ref[...] = jnp.zeros_like(km_ref)

    c, s1, s2 = c_ref[...], s1_ref[...], s2_ref[...]
    blk = lax.broadcasted_iota(jnp.int32, (nblk, MOBA_BLOCK), 0)
    blkf = blk.astype(F32)
    scale = ATT_HEAD_DIM ** -0.5 * LOG2_E
    lane = lax.broadcasted_iota(jnp.int32, (MOBA_BLOCK, ATT_KDIM - ATT_HEAD_DIM), 1)
    this_block = jnp.where(lane == i, 1.0, 0.0).astype(BF16)
    row = lax.broadcasted_iota(jnp.int32, (ATT_VROWS - ATT_HEAD_DIM, MOBA_BLOCK), 0)
    ones_row = jnp.where(row == 0, 1.0, 0.0).astype(BF16)
    pad_rows = ATT_KDIM - ATT_HEAD_DIM - nblk
    eye = jnp.where(lax.broadcasted_iota(jnp.int32, (ATT_HEAD_DIM, ATT_HEAD_DIM), 0)
                    == lax.broadcasted_iota(jnp.int32, (ATT_HEAD_DIM, ATT_HEAD_DIM), 1), 1.0, 0.0).astype(BF16)
    for h in range(ATT_HEADS):
        sl = slice(h * ATT_HEAD_DIM, (h + 1) * ATT_HEAD_DIM)
        qr = _rotate(q_ref[:, sl], c, s1, s2, ROPE_DIM // 2)
        kr = _rotate(k_ref[:, sl], c, s1, s2, ROPE_DIM // 2)
        ko_ref[h, 0, :, :ATT_HEAD_DIM] = kr.astype(BF16)
        ko_ref[h, 0, :, ATT_HEAD_DIM:] = this_block
        km_blk = lax.broadcasted_iota(jnp.int32, (nblk, ATT_HEAD_DIM), 0)
        km_ref[:, sl] = jnp.where(km_blk == i, jnp.mean(kr, axis=0, keepdims=True), km_ref[:, sl])
        qt_ref[h, 0, :ATT_HEAD_DIM, :] = _dot_nt(eye, (qr * scale).astype(BF16)).astype(BF16)
        vt_ref[h, 0, :ATT_HEAD_DIM, :] = _dot_nt(eye, v_ref[:, sl].astype(BF16)).astype(BF16)
        vt_ref[h, 0, ATT_HEAD_DIM:, :] = ones_row
        gate = _dot_nt_f32ish(km_ref[:, sl], qr)
        gate = jnp.where(blk < i, gate, NEG)
        for _ in range(MOBA_TOPK):
            top = jnp.max(gate, axis=0, keepdims=True)
            first = jnp.min(jnp.where(gate == top, blkf, float(nblk)), axis=0, keepdims=True)
            gate = jnp.where(blkf == first, TAKEN, gate)
        bias = jnp.where(gate == TAKEN, jnp.where(blk < i, 0.0, NEG), NEG)
        qt_ref[h, 0, ATT_HEAD_DIM:ATT_HEAD_DIM + nblk, :] = bias.astype(BF16)
        if pad_rows:
            qt_ref[h, 0, ATT_HEAD_DIM + nblk:, :] = jnp.zeros((pad_rows, MOBA_BLOCK), BF16)


def _moba_prep(proj, rope):
    t = proj.shape[0]
    nblk = t // MOBA_BLOCK
    assert ATT_HEAD_DIM + nblk <= ATT_KDIM and nblk % 16 == 0
    c, s1, s2 = rope
    tbl = pl.BlockSpec((MOBA_BLOCK, LANES), lambda i: (i, 0))
    hshape = (ATT_HEADS, nblk)
    return pl.pallas_call(
        _moba_prep_kernel,
        grid=(nblk,),
        in_specs=[pl.BlockSpec((MOBA_BLOCK, ATT_WIDTH), lambda i: (i, OFF_QA // ATT_WIDTH)),
                  pl.BlockSpec((MOBA_BLOCK, ATT_WIDTH), lambda i: (i, OFF_KA // ATT_WIDTH)),
                  pl.BlockSpec((MOBA_BLOCK, ATT_WIDTH), lambda i: (i, OFF_VA // ATT_WIDTH)),
                  tbl, tbl, tbl],
        out_specs=[pl.BlockSpec((ATT_HEADS, 1, ATT_KDIM, MOBA_BLOCK), lambda i: (0, i, 0, 0)),
                   pl.BlockSpec((ATT_HEADS, 1, MOBA_BLOCK, ATT_KDIM), lambda i: (0, i, 0, 0)),
                   pl.BlockSpec((ATT_HEADS, 1, ATT_VROWS, MOBA_BLOCK), lambda i: (0, i, 0, 0))],
        out_shape=[jax.ShapeDtypeStruct(hshape + (ATT_KDIM, MOBA_BLOCK), BF16),
                   jax.ShapeDtypeStruct(hshape + (MOBA_BLOCK, ATT_KDIM), BF16),
                   jax.ShapeDtypeStruct(hshape + (ATT_VROWS, MOBA_BLOCK), BF16)],
        scratch_shapes=[pltpu.VMEM((nblk, ATT_WIDTH), F32)],
        compiler_params=_params(1),
        name="moba_prep",
    )(proj, proj, proj, c, s1, s2)


def _moba_attn_kernel(qt_ref, k_ref, vt_ref, o_ref, s_ref, p_ref, a_ref, acc_ref):
    i = pl.program_id(1)
    heads = qt_ref.shape[0]
    nblk = k_ref.shape[1]
    shape = (MOBA_BLOCK, MOBA_BLOCK)
    causal = lax.broadcasted_iota(jnp.int32, shape, 0) <= lax.broadcasted_iota(jnp.int32, shape, 1)

    ms = []
    for h in range(heads):
        s = _dot(k_ref[h, i, :, :ATT_HEAD_DIM], qt_ref[h, 0, :ATT_HEAD_DIM, :])
        s = jnp.where(causal, s, NEG)
        m = jnp.max(s, axis=0, keepdims=True)
        ms.append(m)
        acc_ref[h] = _dot(vt_ref[h, i], jnp.exp2(s - m).astype(BF16))
        for slot in (2, 3):
            p_ref[slot, h] = jnp.zeros(shape, BF16)
            a_ref[slot, h] = jnp.ones((1, MOBA_BLOCK), F32)

    def scores2(j, slot):
        jc = jnp.minimum(j, nblk - 2)
        for h in range(heads):
            kk = k_ref[h, pl.ds(jc, 2)].reshape(2 * MOBA_BLOCK, ATT_KDIM)
            s2 = _dot(kk, qt_ref[h, 0])
            s_ref[slot, h] = s2[:MOBA_BLOCK]
            s_ref[slot + 1, h] = s2[MOBA_BLOCK:]

    def pv(j, slot):
        jc = jnp.clip(j, 0, nblk - 1)
        for h in range(heads):
            acc_ref[h] = a_ref[slot, h] * acc_ref[h] + _dot(vt_ref[h, jc], p_ref[slot, h])

    def softmax(slot, ms):
        out = []
        for h in range(heads):
            s = s_ref[slot, h]
            m_new = jnp.maximum(ms[h], jnp.max(s, axis=0, keepdims=True))
            a_ref[slot, h] = jnp.exp2(ms[h] - m_new)
            p_ref[slot, h] = jnp.exp2(s - m_new).astype(BF16)
            out.append(m_new)
        return out

    scores2(0, 0)

    def body(jj, ms):
        b = 4 * jj
        pv(b - 2, 2)
        pv(b - 1, 3)
        scores2(b + 2, 2)
        ms = softmax(0, ms)
        ms = softmax(1, ms)
        pv(b, 0)
        pv(b + 1, 1)
        scores2(b + 4, 0)
        ms = softmax(2, ms)
        return softmax(3, ms)

    trips = lax.shift_right_logical(i + 1, 2)
    ms = lax.fori_loop(0, trips, body, ms)
    b = 4 * trips
    pv(b - 2, 2)
    pv(b - 1, 3)

    @pl.when(i - b >= 1)
    def _():
        softmax(1, softmax(0, ms))
        pv(b, 0)
        pv(b + 1, 1)

    for h in range(heads):
        acc = acc_ref[h]
        out = acc[:ATT_HEAD_DIM] / acc[ATT_HEAD_DIM:ATT_HEAD_DIM + 1]
        o_ref[:, h * ATT_HEAD_DIM:(h + 1) * ATT_HEAD_DIM] = out.T.astype(o_ref.dtype)


def _moba_attn(qt, k, vt, heads=2):
    nblk = qt.shape[1]
    t = nblk * MOBA_BLOCK
    blk2 = (MOBA_BLOCK, MOBA_BLOCK)
    return pl.pallas_call(
        _moba_attn_kernel,
        grid=(ATT_HEADS // heads, nblk),
        in_specs=[pl.BlockSpec((heads, 1, ATT_KDIM, MOBA_BLOCK), lambda h, i: (h, i, 0, 0)),
                  pl.BlockSpec((heads, nblk, MOBA_BLOCK, ATT_KDIM), lambda h, i: (h, 0, 0, 0)),
                  pl.BlockSpec((heads, nblk, ATT_VROWS, MOBA_BLOCK), lambda h, i: (h, 0, 0, 0))],
        out_specs=pl.BlockSpec((MOBA_BLOCK, heads * ATT_HEAD_DIM), lambda h, i: (i, h)),
        out_shape=jax.ShapeDtypeStruct((t, ATT_WIDTH), BF16),
        scratch_shapes=[pltpu.VMEM((ATT_SLOTS, heads) + blk2, F32),
                        pltpu.VMEM((ATT_SLOTS, heads) + blk2, BF16),
                        pltpu.VMEM((ATT_SLOTS, heads, 1, MOBA_BLOCK), F32),
                        pltpu.VMEM((heads, ATT_VROWS, MOBA_BLOCK), F32)],
        compiler_params=_params(2),
        name="moba_attn",
    )(qt, k, vt)


def _retention_kernel(q_ref, k_ref, v_ref, g_ref, c_ref, s1_ref, s2_ref,
                      dec_ref, kdec_ref, qdec_ref, cdec_ref, gain_ref, o_ref, state_ref):
    @pl.when(pl.program_id(1) == 0)
    def _():
        state_ref[...] = jnp.zeros_like(state_ref)

    pairs = kdec_ref.shape[0]
    heads = 2 * pairs
    c, s1, s2 = c_ref[...], s1_ref[...], s2_ref[...]
    lane = lax.broadcasted_iota(jnp.int32, (RET_CHUNK, LANES), 1)

    def vslice(h):
        return slice(h * RET_V_DIM, (h + 1) * RET_V_DIM)

    qm, qdm, kb, kdt = [], [], [], []
    for p in range(pairs):
        psl = slice(p * LANES, (p + 1) * LANES)
        q = _rotate(q_ref[:, psl], c, s1, s2, RET_QK_DIM // 2)
        k = _rotate(k_ref[:, psl], c, s1, s2, RET_QK_DIM // 2) * (RET_QK_DIM ** -0.5)
        qd = q * qdec_ref[p]
        kd = k * kdec_ref[p]
        kb.append(k.astype(BF16))
        for hh in range(2):
            mine = (lane < RET_QK_DIM) if hh == 0 else (lane >= RET_QK_DIM)
            qm.append(jnp.where(mine, q, 0.0).astype(BF16))
            qdm.append(jnp.where(mine, qd, 0.0).astype(BF16))
            kdt.append(jnp.where(mine, kd, 0.0).astype(BF16))
    vb = [v_ref[:, vslice(h)].astype(BF16) for h in range(heads)]
    inner = [_dot_nt(qm[h], kb[h // 2]) for h in range(heads)]
    cross = [_dot(qdm[h], state_ref[h].astype(BF16)) for h in range(heads)]
    for h in range(heads):
        state_ref[h] = state_ref[h] * cdec_ref[h] + _dot_tn(kdt[h], vb[h])
    out = [_dot((inner[h] * dec_ref[h]).astype(BF16), vb[h]) + cross[h] for h in range(heads)]
    for h in range(heads):
        mu = jnp.mean(out[h], axis=-1, keepdims=True)
        d = out[h] - mu
        var = jnp.mean(d * d, axis=-1, keepdims=True)
        rn = d * lax.rsqrt(var + LN_EPS) * gain_ref[:, vslice(h)]
        g = g_ref[:, vslice(h)]
        o_ref[:, vslice(h)] = (g * _sigmoid(g) * rn).astype(o_ref.dtype)


def _retention(proj, rope, tables, gain, pairs=4):
    t = proj.shape[0]
    n = t // RET_CHUNK
    c, s1, s2 = rope
    dec, kdec, qdec, cdec = tables
    qk_w = pairs * LANES
    v_w = 2 * pairs * RET_V_DIM
    tbl = pl.BlockSpec((RET_CHUNK, LANES), lambda hp, ci: (ci, 0))
    return pl.pallas_call(
        _retention_kernel,
        grid=(RET_HEADS // (2 * pairs), n),
        in_specs=[pl.BlockSpec((RET_CHUNK, qk_w), lambda hp, ci: (ci, OFF_QR // qk_w + hp)),
                  pl.BlockSpec((RET_CHUNK, qk_w), lambda hp, ci: (ci, OFF_KR // qk_w + hp)),
                  pl.BlockSpec((RET_CHUNK, v_w), lambda hp, ci: (ci, OFF_VR // v_w + hp)),
                  pl.BlockSpec((RET_CHUNK, v_w), lambda hp, ci: (ci, OFF_GR // v_w + hp)),
                  tbl, tbl, tbl,
                  pl.BlockSpec((2 * pairs, RET_CHUNK, RET_CHUNK), lambda hp, ci: (hp, 0, 0)),
                  pl.BlockSpec((pairs, RET_CHUNK, LANES), lambda hp, ci: (hp, 0, 0)),
                  pl.BlockSpec((pairs, RET_CHUNK, LANES), lambda hp, ci: (hp, 0, 0)),
                  pl.BlockSpec((2 * pairs, 1, RET_V_DIM), lambda hp, ci: (hp, 0, 0)),
                  pl.BlockSpec((1, v_w), lambda hp, ci: (0, hp))],
        out_specs=pl.BlockSpec((RET_CHUNK, v_w), lambda hp, ci: (ci, hp)),
        out_shape=jax.ShapeDtypeStruct((t, RET_V_WIDTH), BF16),
        scratch_shapes=[pltpu.VMEM((2 * pairs, LANES, RET_V_DIM), F32)],
        compiler_params=_params(2),
        name="retention",
    )(proj, proj, proj, proj, c, s1, s2, dec, kdec, qdec, cdec, gain)


def _pool_kernel(cur_ref, prev_ref, w_ref, scale_ref, o_ref, buf_a, buf_b, *, tb):
    i = pl.program_id(0)
    g = pl.program_id(1)
    x = cur_ref[...]
    buf_a[0:POOL_HALO, :] = jnp.where(i > 0, prev_ref[...], 0.0)
    buf_a[POOL_HALO:, :] = x
    rows = tb + POOL_HALO
    pos = (i * tb + 1 + lax.broadcasted_iota(jnp.int32, (tb, 1), 0)).astype(F32)
    for gi, w in enumerate(POOL_WINDOWS):
        @pl.when(g == gi)
        def _(w=w):
            src, dst = buf_a, buf_b
            k, lo = 1, SUBLANES
            while k < w:
                dst[lo:rows, :] = src[lo:rows, :] + src[lo - k:rows - k, :]
                src, dst = dst, src
                k, lo = 2 * k, lo + SUBLANES
            tot = src[POOL_HALO:rows, :]
            d = tot / jnp.minimum(pos, float(w)) - x
            y = _dot(d.astype(BF16), w_ref[0])
            o_ref[...] = (y * scale_ref[...]).astype(o_ref.dtype)


def _pool(proj, w_pool, layer, scale, tb=2048):
    t = proj.shape[0]
    tb = min(tb, t)
    gd = POOL_GROUP_DIM
    assert SUBLANES * (max(POOL_WINDOWS).bit_length() - 1) <= POOL_HALO
    return pl.pallas_call(
        functools.partial(_pool_kernel, tb=tb),
        grid=(t // tb, POOL_GROUPS),
        in_specs=[pl.BlockSpec((tb, gd), lambda i, g: (i, OFF_POOL // gd + g)),
                  pl.BlockSpec((POOL_HALO, gd),
                               lambda i, g: (jnp.maximum(i * (tb // POOL_HALO) - 1, 0), OFF_POOL // gd + g)),
                  pl.BlockSpec((None, 1, gd, gd), lambda i, g: (layer, g, 0, 0)),
                  pl.BlockSpec((1, gd), lambda i, g: (0, g))],
        out_specs=pl.BlockSpec((tb, gd), lambda i, g: (i, g)),
        out_shape=jax.ShapeDtypeStruct((t, POOL_WIDTH), BF16),
        scratch_shapes=[pltpu.VMEM((tb + POOL_HALO, gd), F32), pltpu.VMEM((tb + POOL_HALO, gd), F32)],
        compiler_params=_params(2),
        name="pool",
    )(proj, proj, w_pool, scale)


def _merge_kernel(ya_ref, yr_ref, yp_ref, w_ref, ga_ref, gr_ref, gp_ref, o_ref):
    acc = ga_ref[...] * _dot(ya_ref[...], w_ref[0])
    acc = acc + gr_ref[...] * _dot(yr_ref[...], w_ref[1])
    acc = acc + gp_ref[...] * _dot(yp_ref[...], w_ref[2])
    o_ref[...] = acc.astype(o_ref.dtype)


def _merge(ya, yr, yp, w_branch, layer, gates, tm=1024, tn=1024):
    t = ya.shape[0]
    d = w_branch.shape[3]
    br = pl.BlockSpec((tm, BRANCH_WIDTH), lambda i, j: (i, 0))

    def gate(nb):
        return pl.BlockSpec((tm, tn), lambda i, j: (i, (nb * d) // tn + j))

    return pl.pallas_call(
        _merge_kernel,
        grid=(t // tm, d // tn),
        in_specs=[br, br, br,
                  pl.BlockSpec((None, N_BRANCH, BRANCH_WIDTH, tn), lambda i, j: (layer, 0, 0, j)),
                  gate(0), gate(1), gate(2)],
        out_specs=pl.BlockSpec((tm, tn), lambda i, j: (i, j)),
        out_shape=jax.ShapeDtypeStruct((t, d), BF16),
        compiler_params=_params(2),
        name="merge",
    )(ya, yr, yp, w_branch, gates, gates, gates)


def _outproj_ln_kernel(m_ref, w_ref, x_ref, g_ref, b_ref, o_ref, *, alpha):
    half = m_ref.shape[0] // 2
    for r in range(2):
        sl = slice(r * half, (r + 1) * half)
        y = _dot(m_ref[sl, :], w_ref[...])
        o_ref[sl, :] = _layer_norm(alpha * x_ref[sl, :] + y, g_ref[...], b_ref[...])


def _outproj_ln(merged, w_out, layer, x, g, b, alpha, tm=512):
    t, d = x.shape
    row = pl.BlockSpec((tm, d), lambda i: (i, 0))
    vec = pl.BlockSpec((1, d), lambda i: (0, 0))
    return pl.pallas_call(
        functools.partial(_outproj_ln_kernel, alpha=alpha),
        grid=(t // tm,),
        in_specs=[row, pl.BlockSpec((None, d, d), lambda i: (layer, 0, 0)), row, vec, vec],
        out_specs=row,
        out_shape=jax.ShapeDtypeStruct((t, d), F32),
        compiler_params=_params(1),
        name="outproj_ln",
    )(merged, w_out, x, g, b)


def _router_kernel(x_ref, wh_ref, wl_ref, b_ref, e_ref, p_ref):
    x = x_ref[...]
    xh, xl = _split_bf16(x)
    z = _dot(xh, wh_ref[...]) + (_dot(xh, wl_ref[...]) + _dot(xl, wh_ref[...])) + b_ref[...]
    lane = lax.broadcasted_iota(jnp.int32, z.shape, 1)
    big = jnp.int32(ROUTER_LANES)

    def masked_softmax(mask):
        zm = jnp.where(mask, z, NEG)
        e = jnp.where(mask, jnp.exp(zm - jnp.max(zm, axis=-1, keepdims=True)), 0.0)
        return e / jnp.sum(e, axis=-1, keepdims=True)

    def top1(p, mask):
        pm = jnp.where(mask, p, -1.0)
        top = jnp.max(pm, axis=-1, keepdims=True)
        idx = jnp.min(jnp.where(pm == top, lane, big), axis=-1, keepdims=True)
        return top, idx

    in_groups = lane < N_GROUPS
    p1 = masked_softmax(in_groups)
    g_top, g_idx = top1(p1, in_groups)
    lo = N_GROUPS + g_idx * EXPERTS_PER_GROUP
    in_group = (lane >= lo) & (lane < lo + EXPERTS_PER_GROUP)
    p2 = masked_softmax(in_group)
    e1, i1 = top1(p2, in_group)
    e2, i2 = top1(p2, in_group & (lane != i1))
    denom = e1 + e2
    w1 = g_top * e1 / denom
    w2 = g_top * e2 / denom
    e_ref[...] = jnp.where(lane == 0, i1 - N_GROUPS, jnp.where(lane == 1, i2 - N_GROUPS, 0))
    p_ref[...] = jnp.where(lane == 0, w1, jnp.where(lane == 1, w2, 0.0))


def _router(x, w_hi, w_lo, bias, tm=1024):
    t, d = x.shape
    row = pl.BlockSpec((tm, d), lambda i: (i, 0))
    wsp = pl.BlockSpec((d, ROUTER_LANES), lambda i: (0, 0))
    out = pl.BlockSpec((tm, ROUTER_LANES), lambda i: (i, 0))
    return pl.pallas_call(
        _router_kernel,
        grid=(t // tm,),
        in_specs=[row, wsp, wsp, pl.BlockSpec((1, ROUTER_LANES), lambda i: (0, 0))],
        out_specs=[out, out],
        out_shape=[jax.ShapeDtypeStruct((t, ROUTER_LANES), jnp.int32),
                   jax.ShapeDtypeStruct((t, ROUTER_LANES), F32)],
        compiler_params=_params(1),
        name="router",
    )(x, w_hi, w_lo, bias)


WAIT_CHUNKS = (64, 8, 1)


def _start_rows(n, copy):
    for r in range(MOE_BLOCK):
        @pl.when(r < n)
        def _(r=r):
            copy(r).start()


def _wait_rows(n, span):
    rem = n
    for c in WAIT_CHUNKS:
        cnt = lax.shift_right_logical(rem, c.bit_length() - 1)

        def one(_, carry, c=c):
            span(c).wait()
            return carry

        lax.fori_loop(0, cnt, one, 0)
        rem = rem & (c - 1)


def _experts_kernel(be_ref, base_ref, nvalid_ref, src_ref, dst_ref, nused_ref,
                    x_hbm, wg_ref, wu_ref, wd_ref, y_hbm,
                    xbuf, ybuf, wgb, wub, wdb, sem_in, sem_out):
    b = pl.program_id(0)
    n_used = nused_ref[0]
    slot = b & 1

    def start_gather(bb, sl):
        base = base_ref[bb]
        _start_rows(nvalid_ref[bb], lambda r: pltpu.make_async_copy(
            x_hbm.at[pl.ds(src_ref[base + r], 1), :], xbuf.at[sl, pl.ds(r, 1), :], sem_in.at[sl]))

    def wait_gather(bb, sl):
        _wait_rows(nvalid_ref[bb], lambda c: pltpu.make_async_copy(
            x_hbm.at[pl.ds(0, c), :], xbuf.at[sl, pl.ds(0, c), :], sem_in.at[sl]))

    def start_scatter(bb, sl):
        base = base_ref[bb]
        _start_rows(nvalid_ref[bb], lambda r: pltpu.make_async_copy(
            ybuf.at[sl, pl.ds(r, 1), :], y_hbm.at[pl.ds(dst_ref[base + r], 1), :], sem_out.at[sl]))

    def wait_scatter(bb, sl):
        _wait_rows(nvalid_ref[bb], lambda c: pltpu.make_async_copy(
            ybuf.at[sl, pl.ds(0, c), :], y_hbm.at[pl.ds(0, c), :], sem_out.at[sl]))

    @pl.when(b == 0)
    def _():
        xbuf[...] = jnp.zeros_like(xbuf)
        start_gather(0, 0)

    @pl.when(b < n_used)
    def _():
        @pl.when(b + 1 < n_used)
        def _():
            start_gather(b + 1, 1 - slot)

        new_expert = (b == 0) | (be_ref[b] != be_ref[jnp.maximum(b - 1, 0)])

        @pl.when(new_expert)
        def _():
            wgb[...] = wg_ref[...].astype(BF16)
            wub[...] = wu_ref[...].astype(BF16)
            wdb[...] = wd_ref[...].astype(BF16)

        wait_gather(b, slot)

        @pl.when(b >= 2)
        def _():
            wait_scatter(b - 2, slot)

        x = xbuf[slot].astype(BF16)
        gate = _dot(x, wgb[...])
        up = _dot(x, wub[...])
        hid = (gate * _sigmoid(gate) * up).astype(BF16)
        ybuf[slot] = _dot(hid, wdb[...])
        start_scatter(b, slot)

        @pl.when(b == n_used - 1)
        def _():
            @pl.when(b >= 1)
            def _():
                wait_scatter(b - 1, 1 - slot)

            wait_scatter(b, slot)


def _experts(x, wg, wu, wd, layer, plan):
    t, d = x.shape
    block_expert, base, nvalid, src, dst, n_used = plan
    n_blocks = block_expert.shape[0]

    def wspec(shape):
        return pl.BlockSpec((None, None) + shape, lambda b, be, *_: (layer, be[b], 0, 0))

    grid_spec = pltpu.PrefetchScalarGridSpec(
        num_scalar_prefetch=6,
        grid=(n_blocks,),
        in_specs=[pl.BlockSpec(memory_space=pl.ANY),
                  wspec((d, D_EXPERT)), wspec((d, D_EXPERT)), wspec((D_EXPERT, d))],
        out_specs=pl.BlockSpec(memory_space=pl.ANY),
        scratch_shapes=[pltpu.VMEM((2, MOE_BLOCK, d), F32),
                        pltpu.VMEM((2, MOE_BLOCK, d), F32),
                        pltpu.VMEM((d, D_EXPERT), BF16),
                        pltpu.VMEM((d, D_EXPERT), BF16),
                        pltpu.VMEM((D_EXPERT, d), BF16),
                        pltpu.SemaphoreType.DMA((2,)),
                        pltpu.SemaphoreType.DMA((2,))],
    )
    y = pl.pallas_call(
        _experts_kernel,
        grid_spec=grid_spec,
        out_shape=jax.ShapeDtypeStruct((TOP_K * t, d), F32),
        compiler_params=_params(1),
        name="experts",
    )(block_expert, base, nvalid, src, dst, n_used, x, wg, wu, wd)
    return y.reshape(TOP_K, t, d)


def _combine_ln_kernel(x_ref, y0_ref, y1_ref, p_ref, g_ref, b_ref, o_ref, ob_ref, *, alpha):
    p = p_ref[...]
    y = p[:, 0:1] * y0_ref[...] + p[:, 1:2] * y1_ref[...]
    out = _layer_norm(alpha * x_ref[...] + y, g_ref[...], b_ref[...])
    o_ref[...] = out
    ob_ref[...] = out.astype(BF16)


def _combine_ln(x, ycomb, probs, g, b, alpha, tm=512):
    t, d = x.shape
    row = pl.BlockSpec((tm, d), lambda i: (i, 0))
    vec = pl.BlockSpec((1, d), lambda i: (0, 0))
    return pl.pallas_call(
        functools.partial(_combine_ln_kernel, alpha=alpha),
        grid=(t // tm,),
        in_specs=[row,
                  pl.BlockSpec((None, tm, d), lambda i: (0, i, 0)),
                  pl.BlockSpec((None, tm, d), lambda i: (1, i, 0)),
                  pl.BlockSpec((tm, ROUTER_LANES), lambda i: (i, 0)), vec, vec],
        out_specs=[row, row],
        out_shape=[jax.ShapeDtypeStruct((t, d), F32), jax.ShapeDtypeStruct((t, d), BF16)],
        compiler_params=_params(1),
        name="combine_ln",
    )(x, ycomb, ycomb, probs, g, b)


def _rope_tables(t, rot_dim, theta, head_dim):
    half = rot_dim // 2
    inv_freq = 1.0 / (theta ** (jnp.arange(0, rot_dim, 2, dtype=F32) / rot_dim))
    ang = jnp.arange(t, dtype=F32)[:, None] * inv_freq[None, :]
    cos, sin = jnp.cos(ang), jnp.sin(ang)
    rest = head_dim - rot_dim
    ones, zeros = jnp.ones((t, rest), F32), jnp.zeros((t, rest), F32)
    zh = jnp.zeros((t, half), F32)
    c = jnp.concatenate([cos, cos, ones], axis=1)
    s1 = jnp.concatenate([zh, sin, zeros], axis=1)
    s2 = jnp.concatenate([-sin, zh, zeros], axis=1)
    rep = LANES // head_dim
    return tuple(jnp.tile(a, (1, rep)) for a in (c, s1, s2))


def _retention_tables():
    h, c = RET_HEADS, RET_CHUNK
    log_gamma = jnp.log1p(-jnp.exp2(-5.0 - jnp.arange(h, dtype=F32)))
    pos = jnp.arange(c, dtype=F32)
    diff = pos[:, None] - pos[None, :]
    decay = jnp.where(diff >= 0, jnp.exp(log_gamma[:, None, None] * jnp.maximum(diff, 0.0)), 0.0)
    k_decay = jnp.exp(log_gamma[:, None] * (c - 1.0 - pos)[None, :])
    q_decay = jnp.exp(log_gamma[:, None] * (pos + 1.0)[None, :])
    chunk_decay = jnp.exp(log_gamma * c)

    def pair_lanes(a):
        a = a.reshape(h // 2, 2, c).transpose(0, 2, 1)
        return jnp.repeat(a, RET_QK_DIM, axis=2)

    cdec = jnp.broadcast_to(chunk_decay[:, None, None], (h, 1, RET_V_DIM))
    return decay, pair_lanes(k_decay), pair_lanes(q_decay), cdec


def _dispatch_plan(expert, t):
    a = t * TOP_K
    flat_e = expert.reshape(a)
    order = jnp.argsort(flat_e).astype(jnp.int32)
    counts = jnp.sum(flat_e[None, :] == jnp.arange(N_EXPERTS, dtype=flat_e.dtype)[:, None],
                     axis=1, dtype=jnp.int32)
    start = jnp.cumsum(counts) - counts
    blocks = (counts + MOE_BLOCK - 1) // MOE_BLOCK
    blk_end = jnp.cumsum(blocks)
    n_blocks = -(-a // MOE_BLOCK) + N_EXPERTS
    b = jnp.arange(n_blocks, dtype=jnp.int32)
    block_expert = jnp.minimum(jnp.sum(blk_end[None, :] <= b[:, None], axis=1, dtype=jnp.int32),
                               N_EXPERTS - 1)
    first = (b - (blk_end - blocks)[block_expert]) * MOE_BLOCK
    base = start[block_expert] + first
    nvalid = jnp.clip(counts[block_expert] - first, 0, MOE_BLOCK)
    nvalid = jnp.where(b < blk_end[-1], nvalid, 0)
    n_used = blk_end[-1:].astype(jnp.int32)
    order = jnp.pad(order, (0, MOE_BLOCK))
    src = order // TOP_K
    dst = (order % TOP_K) * t + src
    return block_expert, base.astype(jnp.int32), nvalid.astype(jnp.int32), src, dst, n_used


def _token_mixer_ln(h, hb, w_in, ret_gain, w_pool, pool_scale, w_branch, w_out, layer, ln_g, ln_b,
                    rope_a, rope_r, ret_tables, alpha):
    proj = _inproj(hb, w_in, layer, 0, OFF_GATE)
    gates = _inproj(hb, w_in, layer, OFF_GATE, N_IN - OFF_GATE, gate=True)
    qt, kk, vt = _moba_prep(proj, rope_a)
    y_a = _moba_attn(qt, kk, vt)
    y_r = _retention(proj, rope_r, ret_tables, ret_gain[None, :])
    y_p = _pool(proj, w_pool, layer, pool_scale[None, :])
    merged = _merge(y_a, y_r, y_p, w_branch, layer, gates)
    return _outproj_ln(merged, w_out, layer, h, ln_g[None, :], ln_b[None, :], alpha)


def _moe_ln(h, wr_hi, wr_lo, b_router, w_gate, w_up, w_down, layer, ln_g, ln_b, alpha):
    e_pad, p_pad = _router(h, wr_hi, wr_lo, b_router)
    plan = _dispatch_plan(e_pad[:, :TOP_K], h.shape[0])
    ycomb = _experts(h, w_gate, w_up, w_down, layer, plan)
    return _combine_ln(h, ycomb, p_pad, ln_g[None, :], ln_b[None, :], alpha)


def kernel(x, w_in, ret_gain, w_pool, pool_scale, w_branch, w_out, ln1_g, ln1_b, w_r1, b_r1, w_r2, b_r2, w_e_gate, w_e_up, w_e_down, ln2_g, ln2_b):
    bsz, seq, d = x.shape
    depth = w_in.shape[0]
    assert bsz == 1 and d == D_MODEL and seq % 1024 == 0
    t = seq
    alpha = float((2 * depth) ** 0.25)

    rope_a = _rope_tables(t, ROPE_DIM, ROPE_THETA, ATT_HEAD_DIM)
    rope_r = _rope_tables(t, RET_QK_DIM, RET_ROPE_THETA, RET_QK_DIM)
    ret_tables = _retention_tables()

    w_router = jnp.concatenate([w_r1, w_r2], axis=2)
    w_router = jnp.pad(w_router, ((0, 0), (0, 0), (0, ROUTER_LANES - w_router.shape[2])))
    wr_hi = w_router.astype(BF16)
    wr_lo = (w_router - wr_hi.astype(F32)).astype(BF16)
    b_router = jnp.pad(jnp.concatenate([b_r1, b_r2], axis=1),
                       ((0, 0), (0, ROUTER_LANES - N_GROUPS - N_EXPERTS)))[:, None, :]

    w_pool_b, w_branch_b, w_out_b = w_pool.astype(BF16), w_branch.astype(BF16), w_out.astype(BF16)
    h = x.reshape(t, d)
    hb = h.astype(BF16)
    for l in range(depth):
        h = _token_mixer_ln(h, hb, w_in, ret_gain[l], w_pool_b, pool_scale[l], w_branch_b, w_out_b, l,
                            ln1_g[l], ln1_b[l], rope_a, rope_r, ret_tables, alpha)
        h, hb = _moe_ln(h, wr_hi[l], wr_lo[l], b_router[l], w_e_gate, w_e_up, w_e_down, l,
                        ln2_g[l], ln2_b[l], alpha)
    return h.reshape(bsz, seq, d)
```

```python
import functools

import jax
import jax.numpy as jnp
from jax import lax
from jax.experimental import pallas as pl
from jax.experimental.pallas import tpu as pltpu

D_MODEL = 2048

ATT_HEADS = 8
ATT_HEAD_DIM = 128
ATT_WIDTH = ATT_HEADS * ATT_HEAD_DIM
ROPE_DIM = ATT_HEAD_DIM // 4
ROPE_THETA = 500000.0
MOBA_BLOCK = 256
MOBA_TOPK = 3

RET_HEADS = 8
RET_QK_DIM = 64
RET_V_DIM = 128
RET_QK_WIDTH = RET_HEADS * RET_QK_DIM
RET_V_WIDTH = RET_HEADS * RET_V_DIM
RET_CHUNK = 256
RET_ROPE_THETA = 10000.0

POOL_WINDOWS = (2, 4, 8, 16)
POOL_GROUPS = 4
POOL_GROUP_DIM = 256
POOL_WIDTH = POOL_GROUPS * POOL_GROUP_DIM
POOL_HALO = 32

N_BRANCH = 3
BRANCH_WIDTH = 1024

OFF_QA = 0
OFF_KA = OFF_QA + ATT_WIDTH
OFF_VA = OFF_KA + ATT_WIDTH
OFF_QR = OFF_VA + ATT_WIDTH
OFF_KR = OFF_QR + RET_QK_WIDTH
OFF_VR = OFF_KR + RET_QK_WIDTH
OFF_GR = OFF_VR + RET_V_WIDTH
OFF_POOL = OFF_GR + RET_V_WIDTH
OFF_GATE = OFF_POOL + POOL_WIDTH
N_IN = OFF_GATE + N_BRANCH * D_MODEL

N_GROUPS = 4
EXPERTS_PER_GROUP = 8
N_EXPERTS = N_GROUPS * EXPERTS_PER_GROUP
TOP_K = 2
D_EXPERT = 512
MOE_BLOCK = 256
ROUTER_LANES = 128

LN_EPS = 1e-5
NEG = -1e30
TAKEN = -3e38
LOG2_E = 1.4426950408889634

LANES = 128
SUBLANES = 8
VMEM_LIMIT = 56 * 1024 * 1024

F32 = jnp.float32
BF16 = jnp.bfloat16


def _params(n_axes):
    return pltpu.CompilerParams(dimension_semantics=("arbitrary",) * n_axes,
                                vmem_limit_bytes=VMEM_LIMIT)


def _dot(a, b):
    return jnp.dot(a, b, preferred_element_type=F32)


def _dot_nt(a, b):
    return lax.dot_general(a, b, (((1,), (1,)), ((), ())), preferred_element_type=F32)


def _dot_tn(a, b):
    return lax.dot_general(a, b, (((0,), (0,)), ((), ())), preferred_element_type=F32)


def _split_bf16(a):
    hi = a.astype(BF16)
    lo = (a - hi.astype(F32)).astype(BF16)
    return hi, lo


def _dot_nt_f32ish(a, b):
    ah, al = _split_bf16(a)
    bh, bl = _split_bf16(b)
    return _dot_nt(ah, bh) + (_dot_nt(ah, bl) + _dot_nt(al, bh))


def _rotate(x, c, s1, s2, shift):
    return x * c + pltpu.roll(x, shift, 1) * s1 + pltpu.roll(x, LANES - shift, 1) * s2


def _sigmoid(x):
    return 0.5 * jnp.tanh(0.5 * x) + 0.5


def _layer_norm(v, g, b):
    mu = jnp.mean(v, axis=-1, keepdims=True)
    d = v - mu
    var = jnp.mean(d * d, axis=-1, keepdims=True)
    return d * lax.rsqrt(var + LN_EPS) * g + b


def _inproj_kernel(x_ref, w_ref, o_ref, wb_ref, *, gate):
    @pl.when(pl.program_id(1) == 0)
    def _():
        wb_ref[...] = w_ref[...].astype(BF16)

    y = _dot(x_ref[...], wb_ref[...])
    o_ref[...] = _sigmoid(y).astype(o_ref.dtype) if gate else y


def _inproj(xb, w, layer, col0, ncols, gate=False, tm=1024, tn=1024):
    t, d = xb.shape
    return pl.pallas_call(
        functools.partial(_inproj_kernel, gate=gate),
        grid=(ncols // tn, t // tm),
        in_specs=[pl.BlockSpec((tm, d), lambda j, i: (i, 0)),
                  pl.BlockSpec((None, d, tn), lambda j, i: (layer, 0, col0 // tn + j))],
        out_specs=pl.BlockSpec((tm, tn), lambda j, i: (i, j)),
        out_shape=jax.ShapeDtypeStruct((t, ncols), BF16 if gate else F32),
        scratch_shapes=[pltpu.VMEM((d, tn), BF16)],
        compiler_params=_params(2),
        name="inproj_gates" if gate else "inproj",
    )(xb, w)


ATT_KDIM = 2 * ATT_HEAD_DIM
ATT_VROWS = ATT_HEAD_DIM + 16
ATT_SLOTS = 4


def _moba_prep_kernel(q_ref, k_ref, v_ref, c_ref, s1_ref, s2_ref, qt_ref, ko_ref, vt_ref, km_ref):
    i = pl.program_id(0)
    nblk = km_ref.shape[0]

    @pl.when(i == 0)
    def _():
        km_ref[...] = jnp.zeros_like(km_ref)

    c, s1, s2 = c_ref[...], s1_ref[...], s2_ref[...]
    blk = lax.broadcasted_iota(jnp.int32, (nblk, MOBA_BLOCK), 0)
    blkf = blk.astype(F32)
    scale = ATT_HEAD_DIM ** -0.5 * LOG2_E
    lane = lax.broadcasted_iota(jnp.int32, (MOBA_BLOCK, ATT_KDIM - ATT_HEAD_DIM), 1)
    this_block = jnp.where(lane == i, 1.0, 0.0).astype(BF16)
    row = lax.broadcasted_iota(jnp.int32, (ATT_VROWS - ATT_HEAD_DIM, MOBA_BLOCK), 0)
    ones_row = jnp.where(row == 0, 1.0, 0.0).astype(BF16)
    pad_rows = ATT_KDIM - ATT_HEAD_DIM - nblk
    eye = jnp.where(lax.broadcasted_iota(jnp.int32, (ATT_HEAD_DIM, ATT_HEAD_DIM), 0)
                    == lax.broadcasted_iota(jnp.int32, (ATT_HEAD_DIM, ATT_HEAD_DIM), 1), 1.0, 0.0).astype(BF16)
    for h in range(ATT_HEADS):
        sl = slice(h * ATT_HEAD_DIM, (h + 1) * ATT_HEAD_DIM)
        qr = _rotate(q_ref[:, sl], c, s1, s2, ROPE_DIM // 2)
        kr = _rotate(k_ref[:, sl], c, s1, s2, ROPE_DIM // 2)
        ko_ref[h, 0, :, :ATT_HEAD_DIM] = kr.astype(BF16)
        ko_ref[h, 0, :, ATT_HEAD_DIM:] = this_block
        km_blk = lax.broadcasted_iota(jnp.int32, (nblk, ATT_HEAD_DIM), 0)
        km_ref[:, sl] = jnp.where(km_blk == i, jnp.mean(kr, axis=0, keepdims=True), km_ref[:, sl])
        qt_ref[h, 0, :ATT_HEAD_DIM, :] = _dot_nt(eye, (qr * scale).astype(BF16)).astype(BF16)
        vt_ref[h, 0, :ATT_HEAD_DIM, :] = _dot_nt(eye, v_ref[:, sl].astype(BF16)).astype(BF16)
        vt_ref[h, 0, ATT_HEAD_DIM:, :] = ones_row
        gate = _dot_nt_f32ish(km_ref[:, sl], qr)
        gate = jnp.where(blk < i, gate, NEG)
        for _ in range(MOBA_TOPK):
            top = jnp.max(gate, axis=0, keepdims=True)
            first = jnp.min(jnp.where(gate == top, blkf, float(nblk)), axis=0, keepdims=True)
            gate = jnp.where(blkf == first, TAKEN, gate)
        bias = jnp.where(gate == TAKEN, jnp.where(blk < i, 0.0, NEG), NEG)
        qt_ref[h, 0, ATT_HEAD_DIM:ATT_HEAD_DIM + nblk, :] = bias.astype(BF16)
        if pad_rows:
            qt_ref[h, 0, ATT_HEAD_DIM + nblk:, :] = jnp.zeros((pad_rows, MOBA_BLOCK), BF16)


def _moba_prep(proj, rope):
    t = proj.shape[0]
    nblk = t // MOBA_BLOCK
    assert ATT_HEAD_DIM + nblk <= ATT_KDIM and nblk % 16 == 0
    c, s1, s2 = rope
    tbl = pl.BlockSpec((MOBA_BLOCK, LANES), lambda i: (i, 0))
    hshape = (ATT_HEADS, nblk)
    return pl.pallas_call(
        _moba_prep_kernel,
        grid=(nblk,),
        in_specs=[pl.BlockSpec((MOBA_BLOCK, ATT_WIDTH), lambda i: (i, OFF_QA // ATT_WIDTH)),
                  pl.BlockSpec((MOBA_BLOCK, ATT_WIDTH), lambda i: (i, OFF_KA // ATT_WIDTH)),
                  pl.BlockSpec((MOBA_BLOCK, ATT_WIDTH), lambda i: (i, OFF_VA // ATT_WIDTH)),
                  tbl, tbl, tbl],
        out_specs=[pl.BlockSpec((ATT_HEADS, 1, ATT_KDIM, MOBA_BLOCK), lambda i: (0, i, 0, 0)),
                   pl.BlockSpec((ATT_HEADS, 1, MOBA_BLOCK, ATT_KDIM), lambda i: (0, i, 0, 0)),
                   pl.BlockSpec((ATT_HEADS, 1, ATT_VROWS, MOBA_BLOCK), lambda i: (0, i, 0, 0))],
        out_shape=[jax.ShapeDtypeStruct(hshape + (ATT_KDIM, MOBA_BLOCK), BF16),
                   jax.ShapeDtypeStruct(hshape + (MOBA_BLOCK, ATT_KDIM), BF16),
                   jax.ShapeDtypeStruct(hshape + (ATT_VROWS, MOBA_BLOCK), BF16)],
        scratch_shapes=[pltpu.VMEM((nblk, ATT_WIDTH), F32)],
        compiler_params=_params(1),
        name="moba_prep",
    )(proj, proj, proj, c, s1, s2)


def _moba_attn_kernel(qt_ref, k_ref, vt_ref, o_ref, s_ref, p_ref, a_ref, acc_ref):
    i = pl.program_id(1)
    heads = qt_ref.shape[0]
    nblk = k_ref.shape[1]
    shape = (MOBA_BLOCK, MOBA_BLOCK)
    causal = lax.broadcasted_iota(jnp.int32, shape, 0) <= lax.broadcasted_iota(jnp.int32, shape, 1)

    ms = []
    for h in range(heads):
        s = _dot(k_ref[h, i, :, :ATT_HEAD_DIM], qt_ref[h, 0, :ATT_HEAD_DIM, :])
        s_ref[0, h] = jnp.where(causal, s, NEG)
        s_ref[1, h] = _dot(k_ref[h, 0], qt_ref[h, 0])
        ms.append(jnp.full((1, MOBA_BLOCK), NEG, F32))
        acc_ref[h] = jnp.zeros(acc_ref.shape[1:], F32)
        for slot in (2, 3):
            p_ref[slot, h] = jnp.zeros(shape, BF16)
            a_ref[slot, h] = jnp.ones((1, MOBA_BLOCK), F32)

    def scores2(pos, slot):
        jc = jnp.minimum(pos - 1, nblk - 2)
        for h in range(heads):
            kk = k_ref[h, pl.ds(jc, 2)].reshape(2 * MOBA_BLOCK, ATT_KDIM)
            s2 = _dot(kk, qt_ref[h, 0])
            s_ref[slot, h] = s2[:MOBA_BLOCK]
            s_ref[slot + 1, h] = s2[MOBA_BLOCK:]

    def pv(pos, slot):
        jc = jnp.where(pos == 0, i, jnp.clip(pos - 1, 0, nblk - 1))
        for h in range(heads):
            acc_ref[h] = a_ref[slot, h] * acc_ref[h] + _dot(vt_ref[h, jc], p_ref[slot, h])

    def softmax(slot, ms):
        out = []
        for h in range(heads):
            s = s_ref[slot, h]
            m_new = jnp.maximum(ms[h], jnp.max(s, axis=0, keepdims=True))
            a_ref[slot, h] = jnp.exp2(ms[h] - m_new)
            p_ref[slot, h] = jnp.exp2(s - m_new).astype(BF16)
            out.append(m_new)
        return out

    def body(jj, ms):
        b = 4 * jj
        pv(b - 2, 2)
        pv(b - 1, 3)
        scores2(b + 2, 2)
        ms = softmax(0, ms)
        ms = softmax(1, ms)
        pv(b, 0)
        pv(b + 1, 1)
        scores2(b + 4, 0)
        ms = softmax(2, ms)
        return softmax(3, ms)

    npos = i + 1
    trips = lax.shift_right_logical(npos + 1, 2)
    ms = lax.fori_loop(0, trips, body, ms)
    b = 4 * trips
    pv(b - 2, 2)
    pv(b - 1, 3)

    @pl.when(npos - b >= 1)
    def _():
        softmax(1, softmax(0, ms))
        pv(b, 0)
        pv(b + 1, 1)

    for h in range(heads):
        acc = acc_ref[h]
        out = acc[:ATT_HEAD_DIM] / acc[ATT_HEAD_DIM:ATT_HEAD_DIM + 1]
        o_ref[:, h * ATT_HEAD_DIM:(h + 1) * ATT_HEAD_DIM] = out.T.astype(o_ref.dtype)


def _moba_attn(qt, k, vt, heads=2):
    nblk = qt.shape[1]
    t = nblk * MOBA_BLOCK
    blk2 = (MOBA_BLOCK, MOBA_BLOCK)
    return pl.pallas_call(
        _moba_attn_kernel,
        grid=(ATT_HEADS // heads, nblk),
        in_specs=[pl.BlockSpec((heads, 1, ATT_KDIM, MOBA_BLOCK), lambda h, i: (h, i, 0, 0)),
                  pl.BlockSpec((heads, nblk, MOBA_BLOCK, ATT_KDIM), lambda h, i: (h, 0, 0, 0)),
                  pl.BlockSpec((heads, nblk, ATT_VROWS, MOBA_BLOCK), lambda h, i: (h, 0, 0, 0))],
        out_specs=pl.BlockSpec((MOBA_BLOCK, heads * ATT_HEAD_DIM), lambda h, i: (i, h)),
        out_shape=jax.ShapeDtypeStruct((t, ATT_WIDTH), BF16),
        scratch_shapes=[pltpu.VMEM((ATT_SLOTS, heads) + blk2, F32),
                        pltpu.VMEM((ATT_SLOTS, heads) + blk2, BF16),
                        pltpu.VMEM((ATT_SLOTS, heads, 1, MOBA_BLOCK), F32),
                        pltpu.VMEM((heads, ATT_VROWS, MOBA_BLOCK), F32)],
        compiler_params=_params(2),
        name="moba_attn",
    )(qt, k, vt)


def _retention_kernel(q_ref, k_ref, v_ref, g_ref, c_ref, s1_ref, s2_ref,
                      dec_ref, kdec_ref, qdec_ref, cdec_ref, gain_ref, o_ref, state_ref):
    @pl.when(pl.program_id(1) == 0)
    def _():
        state_ref[...] = jnp.zeros_like(state_ref)

    pairs = kdec_ref.shape[0]
    heads = 2 * pairs
    c, s1, s2 = c_ref[...], s1_ref[...], s2_ref[...]
    lane = lax.broadcasted_iota(jnp.int32, (RET_CHUNK, LANES), 1)

    def vslice(h):
        return slice(h * RET_V_DIM, (h + 1) * RET_V_DIM)

    qm, qdm, kb, kdt = [], [], [], []
    for p in range(pairs):
        psl = slice(p * LANES, (p + 1) * LANES)
        q = _rotate(q_ref[:, psl], c, s1, s2, RET_QK_DIM // 2)
        k = _rotate(k_ref[:, psl], c, s1, s2, RET_QK_DIM // 2) * (RET_QK_DIM ** -0.5)
        qd = q * qdec_ref[p]
        kd = k * kdec_ref[p]
        kb.append(k.astype(BF16))
        for hh in range(2):
            mine = (lane < RET_QK_DIM) if hh == 0 else (lane >= RET_QK_DIM)
            qm.append(jnp.where(mine, q, 0.0).astype(BF16))
            qdm.append(jnp.where(mine, qd, 0.0).astype(BF16))
            kdt.append(jnp.where(mine, kd, 0.0).astype(BF16))
    vb = [v_ref[:, vslice(h)].astype(BF16) for h in range(heads)]
    inner = [_dot_nt(qm[h], kb[h // 2]) for h in range(heads)]
    cross = [_dot(qdm[h], state_ref[h].astype(BF16)) for h in range(heads)]
    for h in range(heads):
        state_ref[h] = state_ref[h] * cdec_ref[h] + _dot_tn(kdt[h], vb[h])
    out = [_dot((inner[h] * dec_ref[h]).astype(BF16), vb[h]) + cross[h] for h in range(heads)]
    for h in range(heads):
        mu = jnp.mean(out[h], axis=-1, keepdims=True)
        d = out[h] - mu
        var = jnp.mean(d * d, axis=-1, keepdims=True)
        rn = d * lax.rsqrt(var + LN_EPS) * gain_ref[:, vslice(h)]
        g = g_ref[:, vslice(h)]
        o_ref[:, vslice(h)] = (g * _sigmoid(g) * rn).astype(o_ref.dtype)


def _retention(proj, rope, tables, gain, pairs=4):
    t = proj.shape[0]
    n = t // RET_CHUNK
    c, s1, s2 = rope
    dec, kdec, qdec, cdec = tables
    qk_w = pairs * LANES
    v_w = 2 * pairs * RET_V_DIM
    tbl = pl.BlockSpec((RET_CHUNK, LANES), lambda hp, ci: (ci, 0))
    return pl.pallas_call(
        _retention_kernel,
        grid=(RET_HEADS // (2 * pairs), n),
        in_specs=[pl.BlockSpec((RET_CHUNK, qk_w), lambda hp, ci: (ci, OFF_QR // qk_w + hp)),
                  pl.BlockSpec((RET_CHUNK, qk_w), lambda hp, ci: (ci, OFF_KR // qk_w + hp)),
                  pl.BlockSpec((RET_CHUNK, v_w), lambda hp, ci: (ci, OFF_VR // v_w + hp)),
                  pl.BlockSpec((RET_CHUNK, v_w), lambda hp, ci: (ci, OFF_GR // v_w + hp)),
                  tbl, tbl, tbl,
                  pl.BlockSpec((2 * pairs, RET_CHUNK, RET_CHUNK), lambda hp, ci: (hp, 0, 0)),
                  pl.BlockSpec((pairs, RET_CHUNK, LANES), lambda hp, ci: (hp, 0, 0)),
                  pl.BlockSpec((pairs, RET_CHUNK, LANES), lambda hp, ci: (hp, 0, 0)),
                  pl.BlockSpec((2 * pairs, 1, RET_V_DIM), lambda hp, ci: (hp, 0, 0)),
                  pl.BlockSpec((1, v_w), lambda hp, ci: (0, hp))],
        out_specs=pl.BlockSpec((RET_CHUNK, v_w), lambda hp, ci: (ci, hp)),
        out_shape=jax.ShapeDtypeStruct((t, RET_V_WIDTH), BF16),
        scratch_shapes=[pltpu.VMEM((2 * pairs, LANES, RET_V_DIM), F32)],
        compiler_params=_params(2),
        name="retention",
    )(proj, proj, proj, proj, c, s1, s2, dec, kdec, qdec, cdec, gain)


def _pool_kernel(cur_ref, prev_ref, w_ref, scale_ref, o_ref, buf_a, buf_b, *, tb):
    i = pl.program_id(0)
    g = pl.program_id(1)
    x = cur_ref[...]
    buf_a[0:POOL_HALO, :] = jnp.where(i > 0, prev_ref[...], 0.0)
    buf_a[POOL_HALO:, :] = x
    rows = tb + POOL_HALO
    pos = (i * tb + 1 + lax.broadcasted_iota(jnp.int32, (tb, 1), 0)).astype(F32)
    for gi, w in enumerate(POOL_WINDOWS):
        @pl.when(g == gi)
        def _(w=w):
            src, dst = buf_a, buf_b
            k, lo = 1, SUBLANES
            while k < w:
                dst[lo:rows, :] = src[lo:rows, :] + src[lo - k:rows - k, :]
                src, dst = dst, src
                k, lo = 2 * k, lo + SUBLANES
            tot = src[POOL_HALO:rows, :]
            d = tot / jnp.minimum(pos, float(w)) - x
            y = _dot(d.astype(BF16), w_ref[0])
            o_ref[...] = (y * scale_ref[...]).astype(o_ref.dtype)


def _pool(proj, w_pool, layer, scale, tb=2048):
    t = proj.shape[0]
    tb = min(tb, t)
    gd = POOL_GROUP_DIM
    assert SUBLANES * (max(POOL_WINDOWS).bit_length() - 1) <= POOL_HALO
    return pl.pallas_call(
        functools.partial(_pool_kernel, tb=tb),
        grid=(t // tb, POOL_GROUPS),
        in_specs=[pl.BlockSpec((tb, gd), lambda i, g: (i, OFF_POOL // gd + g)),
                  pl.BlockSpec((POOL_HALO, gd),
                               lambda i, g: (jnp.maximum(i * (tb // POOL_HALO) - 1, 0), OFF_POOL // gd + g)),
                  pl.BlockSpec((None, 1, gd, gd), lambda i, g: (layer, g, 0, 0)),
                  pl.BlockSpec((1, gd), lambda i, g: (0, g))],
        out_specs=pl.BlockSpec((tb, gd), lambda i, g: (i, g)),
        out_shape=jax.ShapeDtypeStruct((t, POOL_WIDTH), BF16),
        scratch_shapes=[pltpu.VMEM((tb + POOL_HALO, gd), F32), pltpu.VMEM((tb + POOL_HALO, gd), F32)],
        compiler_params=_params(2),
        name="pool",
    )(proj, proj, w_pool, scale)


def _merge_kernel(ya_ref, yr_ref, yp_ref, w_ref, ga_ref, gr_ref, gp_ref, o_ref):
    acc = ga_ref[...] * _dot(ya_ref[...], w_ref[0])
    acc = acc + gr_ref[...] * _dot(yr_ref[...], w_ref[1])
    acc = acc + gp_ref[...] * _dot(yp_ref[...], w_ref[2])
    o_ref[...] = acc.astype(o_ref.dtype)


def _merge(ya, yr, yp, w_branch, layer, gates, tm=1024, tn=1024):
    t = ya.shape[0]
    d = w_branch.shape[3]
    br = pl.BlockSpec((tm, BRANCH_WIDTH), lambda i, j: (i, 0))

    def gate(nb):
        return pl.BlockSpec((tm, tn), lambda i, j: (i, (nb * d) // tn + j))

    return pl.pallas_call(
        _merge_kernel,
        grid=(t // tm, d // tn),
        in_specs=[br, br, br,
                  pl.BlockSpec((None, N_BRANCH, BRANCH_WIDTH, tn), lambda i, j: (layer, 0, 0, j)),
                  gate(0), gate(1), gate(2)],
        out_specs=pl.BlockSpec((tm, tn), lambda i, j: (i, j)),
        out_shape=jax.ShapeDtypeStruct((t, d), BF16),
        compiler_params=_params(2),
        name="merge",
    )(ya, yr, yp, w_branch, gates, gates, gates)


def _outproj_ln_kernel(m_ref, w_ref, x_ref, g_ref, b_ref, o_ref, *, alpha):
    half = m_ref.shape[0] // 2
    for r in range(2):
        sl = slice(r * half, (r + 1) * half)
        y = _dot(m_ref[sl, :], w_ref[...])
        o_ref[sl, :] = _layer_norm(alpha * x_ref[sl, :] + y, g_ref[...], b_ref[...])


def _outproj_ln(merged, w_out, layer, x, g, b, alpha, tm=512):
    t, d = x.shape
    row = pl.BlockSpec((tm, d), lambda i: (i, 0))
    vec = pl.BlockSpec((1, d), lambda i: (0, 0))
    return pl.pallas_call(
        functools.partial(_outproj_ln_kernel, alpha=alpha),
        grid=(t // tm,),
        in_specs=[row, pl.BlockSpec((None, d, d), lambda i: (layer, 0, 0)), row, vec, vec],
        out_specs=row,
        out_shape=jax.ShapeDtypeStruct((t, d), F32),
        compiler_params=_params(1),
        name="outproj_ln",
    )(merged, w_out, x, g, b)


def _router_kernel(x_ref, wh_ref, wl_ref, b_ref, e_ref, p_ref):
    x = x_ref[...]
    xh, xl = _split_bf16(x)
    z = _dot(xh, wh_ref[...]) + (_dot(xh, wl_ref[...]) + _dot(xl, wh_ref[...])) + b_ref[...]
    lane = lax.broadcasted_iota(jnp.int32, z.shape, 1)
    big = jnp.int32(ROUTER_LANES)

    def masked_softmax(mask):
        zm = jnp.where(mask, z, NEG)
        e = jnp.where(mask, jnp.exp(zm - jnp.max(zm, axis=-1, keepdims=True)), 0.0)
        return e / jnp.sum(e, axis=-1, keepdims=True)

    def top1(p, mask):
        pm = jnp.where(mask, p, -1.0)
        top = jnp.max(pm, axis=-1, keepdims=True)
        idx = jnp.min(jnp.where(pm == top, lane, big), axis=-1, keepdims=True)
        return top, idx

    in_groups = lane < N_GROUPS
    p1 = masked_softmax(in_groups)
    g_top, g_idx = top1(p1, in_groups)
    lo = N_GROUPS + g_idx * EXPERTS_PER_GROUP
    in_group = (lane >= lo) & (lane < lo + EXPERTS_PER_GROUP)
    p2 = masked_softmax(in_group)
    e1, i1 = top1(p2, in_group)
    e2, i2 = top1(p2, in_group & (lane != i1))
    denom = e1 + e2
    w1 = g_top * e1 / denom
    w2 = g_top * e2 / denom
    e_ref[...] = jnp.where(lane == 0, i1 - N_GROUPS, jnp.where(lane == 1, i2 - N_GROUPS, 0))
    p_ref[...] = jnp.where(lane == 0, w1, jnp.where(lane == 1, w2, 0.0))


def _router(x, w_hi, w_lo, bias, tm=1024):
    t, d = x.shape
    row = pl.BlockSpec((tm, d), lambda i: (i, 0))
    wsp = pl.BlockSpec((d, ROUTER_LANES), lambda i: (0, 0))
    out = pl.BlockSpec((tm, ROUTER_LANES), lambda i: (i, 0))
    return pl.pallas_call(
        _router_kernel,
        grid=(t // tm,),
        in_specs=[row, wsp, wsp, pl.BlockSpec((1, ROUTER_LANES), lambda i: (0, 0))],
        out_specs=[out, out],
        out_shape=[jax.ShapeDtypeStruct((t, ROUTER_LANES), jnp.int32),
                   jax.ShapeDtypeStruct((t, ROUTER_LANES), F32)],
        compiler_params=_params(1),
        name="router",
    )(x, w_hi, w_lo, bias)


WAIT_CHUNKS = (64, 8, 1)


def _start_rows(n, copy):
    for r in range(MOE_BLOCK):
        @pl.when(r < n)
        def _(r=r):
            copy(r).start()


def _wait_rows(n, span):
    rem = n
    for c in WAIT_CHUNKS:
        cnt = lax.shift_right_logical(rem, c.bit_length() - 1)

        def one(_, carry, c=c):
            span(c).wait()
            return carry

        lax.fori_loop(0, cnt, one, 0)
        rem = rem & (c - 1)


def _experts_kernel(be_ref, base_ref, nvalid_ref, src_ref, dst_ref, nused_ref,
                    x_hbm, wg_ref, wu_ref, wd_ref, y_hbm,
                    xbuf, ybuf, wgb, wub, wdb, sem_in, sem_out):
    b = pl.program_id(0)
    n_used = nused_ref[0]
    slot = b & 1

    def start_gather(bb, sl):
        base = base_ref[bb]
        _start_rows(nvalid_ref[bb], lambda r: pltpu.make_async_copy(
            x_hbm.at[pl.ds(src_ref[base + r], 1), :], xbuf.at[sl, pl.ds(r, 1), :], sem_in.at[sl]))

    def wait_gather(bb, sl):
        _wait_rows(nvalid_ref[bb], lambda c: pltpu.make_async_copy(
            x_hbm.at[pl.ds(0, c), :], xbuf.at[sl, pl.ds(0, c), :], sem_in.at[sl]))

    def start_scatter(bb, sl):
        base = base_ref[bb]
        _start_rows(nvalid_ref[bb], lambda r: pltpu.make_async_copy(
            ybuf.at[sl, pl.ds(r, 1), :], y_hbm.at[pl.ds(dst_ref[base + r], 1), :], sem_out.at[sl]))

    def wait_scatter(bb, sl):
        _wait_rows(nvalid_ref[bb], lambda c: pltpu.make_async_copy(
            ybuf.at[sl, pl.ds(0, c), :], y_hbm.at[pl.ds(0, c), :], sem_out.at[sl]))

    @pl.when(b == 0)
    def _():
        xbuf[...] = jnp.zeros_like(xbuf)
        start_gather(0, 0)

    @pl.when(b < n_used)
    def _():
        @pl.when(b + 1 < n_used)
        def _():
            start_gather(b + 1, 1 - slot)

        new_expert = (b == 0) | (be_ref[b] != be_ref[jnp.maximum(b - 1, 0)])

        @pl.when(new_expert)
        def _():
            wgb[...] = wg_ref[...].astype(BF16)
            wub[...] = wu_ref[...].astype(BF16)
            wdb[...] = wd_ref[...].astype(BF16)

        wait_gather(b, slot)

        @pl.when(b >= 2)
        def _():
            wait_scatter(b - 2, slot)

        x = xbuf[slot].astype(BF16)
        gate = _dot(x, wgb[...])
        up = _dot(x, wub[...])
        hid = (gate * _sigmoid(gate) * up).astype(BF16)
        ybuf[slot] = _dot(hid, wdb[...])
        start_scatter(b, slot)

        @pl.when(b == n_used - 1)
        def _():
            @pl.when(b >= 1)
            def _():
                wait_scatter(b - 1, 1 - slot)

            wait_scatter(b, slot)


def _experts(x, wg, wu, wd, layer, plan):
    t, d = x.shape
    block_expert, base, nvalid, src, dst, n_used = plan
    n_blocks = block_expert.shape[0]

    def wspec(shape):
        return pl.BlockSpec((None, None) + shape, lambda b, be, *_: (layer, be[b], 0, 0))

    grid_spec = pltpu.PrefetchScalarGridSpec(
        num_scalar_prefetch=6,
        grid=(n_blocks,),
        in_specs=[pl.BlockSpec(memory_space=pl.ANY),
                  wspec((d, D_EXPERT)), wspec((d, D_EXPERT)), wspec((D_EXPERT, d))],
        out_specs=pl.BlockSpec(memory_space=pl.ANY),
        scratch_shapes=[pltpu.VMEM((2, MOE_BLOCK, d), F32),
                        pltpu.VMEM((2, MOE_BLOCK, d), F32),
                        pltpu.VMEM((d, D_EXPERT), BF16),
                        pltpu.VMEM((d, D_EXPERT), BF16),
                        pltpu.VMEM((D_EXPERT, d), BF16),
                        pltpu.SemaphoreType.DMA((2,)),
                        pltpu.SemaphoreType.DMA((2,))],
    )
    y = pl.pallas_call(
        _experts_kernel,
        grid_spec=grid_spec,
        out_shape=jax.ShapeDtypeStruct((TOP_K * t, d), F32),
        compiler_params=_params(1),
        name="experts",
    )(block_expert, base, nvalid, src, dst, n_used, x, wg, wu, wd)
    return y.reshape(TOP_K, t, d)


def _combine_ln_kernel(x_ref, y0_ref, y1_ref, p_ref, g_ref, b_ref, o_ref, ob_ref, *, alpha):
    p = p_ref[...]
    y = p[:, 0:1] * y0_ref[...] + p[:, 1:2] * y1_ref[...]
    out = _layer_norm(alpha * x_ref[...] + y, g_ref[...], b_ref[...])
    o_ref[...] = out
    ob_ref[...] = out.astype(BF16)


def _combine_ln(x, ycomb, probs, g, b, alpha, tm=512):
    t, d = x.shape
    row = pl.BlockSpec((tm, d), lambda i: (i, 0))
    vec = pl.BlockSpec((1, d), lambda i: (0, 0))
    return pl.pallas_call(
        functools.partial(_combine_ln_kernel, alpha=alpha),
        grid=(t // tm,),
        in_specs=[row,
                  pl.BlockSpec((None, tm, d), lambda i: (0, i, 0)),
                  pl.BlockSpec((None, tm, d), lambda i: (1, i, 0)),
                  pl.BlockSpec((tm, ROUTER_LANES), lambda i: (i, 0)), vec, vec],
        out_specs=[row, row],
        out_shape=[jax.ShapeDtypeStruct((t, d), F32), jax.ShapeDtypeStruct((t, d), BF16)],
        compiler_params=_params(1),
        name="combine_ln",
    )(x, ycomb, ycomb, probs, g, b)


def _rope_tables(t, rot_dim, theta, head_dim):
    half = rot_dim // 2
    inv_freq = 1.0 / (theta ** (jnp.arange(0, rot_dim, 2, dtype=F32) / rot_dim))
    ang = jnp.arange(t, dtype=F32)[:, None] * inv_freq[None, :]
    cos, sin = jnp.cos(ang), jnp.sin(ang)
    rest = head_dim - rot_dim
    ones, zeros = jnp.ones((t, rest), F32), jnp.zeros((t, rest), F32)
    zh = jnp.zeros((t, half), F32)
    c = jnp.concatenate([cos, cos, ones], axis=1)
    s1 = jnp.concatenate([zh, sin, zeros], axis=1)
    s2 = jnp.concatenate([-sin, zh, zeros], axis=1)
    rep = LANES // head_dim
    return tuple(jnp.tile(a, (1, rep)) for a in (c, s1, s2))


def _retention_tables():
    h, c = RET_HEADS, RET_CHUNK
    log_gamma = jnp.log1p(-jnp.exp2(-5.0 - jnp.arange(h, dtype=F32)))
    pos = jnp.arange(c, dtype=F32)
    diff = pos[:, None] - pos[None, :]
    decay = jnp.where(diff >= 0, jnp.exp(log_gamma[:, None, None] * jnp.maximum(diff, 0.0)), 0.0)
    k_decay = jnp.exp(log_gamma[:, None] * (c - 1.0 - pos)[None, :])
    q_decay = jnp.exp(log_gamma[:, None] * (pos + 1.0)[None, :])
    chunk_decay = jnp.exp(log_gamma * c)

    def pair_lanes(a):
        a = a.reshape(h // 2, 2, c).transpose(0, 2, 1)
        return jnp.repeat(a, RET_QK_DIM, axis=2)

    cdec = jnp.broadcast_to(chunk_decay[:, None, None], (h, 1, RET_V_DIM))
    return decay, pair_lanes(k_decay), pair_lanes(q_decay), cdec


def _dispatch_plan(expert, t):
    a = t * TOP_K
    flat_e = expert.reshape(a)
    order = jnp.argsort(flat_e).astype(jnp.int32)
    counts = jnp.sum(flat_e[None, :] == jnp.arange(N_EXPERTS, dtype=flat_e.dtype)[:, None],
                     axis=1, dtype=jnp.int32)
    start = jnp.cumsum(counts) - counts
    blocks = (counts + MOE_BLOCK - 1) // MOE_BLOCK
    blk_end = jnp.cumsum(blocks)
    n_blocks = -(-a // MOE_BLOCK) + N_EXPERTS
    b = jnp.arange(n_blocks, dtype=jnp.int32)
    block_expert = jnp.minimum(jnp.sum(blk_end[None, :] <= b[:, None], axis=1, dtype=jnp.int32),
                               N_EXPERTS - 1)
    first = (b - (blk_end - blocks)[block_expert]) * MOE_BLOCK
    base = start[block_expert] + first
    nvalid = jnp.clip(counts[block_expert] - first, 0, MOE_BLOCK)
    nvalid = jnp.where(b < blk_end[-1], nvalid, 0)
    n_used = blk_end[-1:].astype(jnp.int32)
    order = jnp.pad(order, (0, MOE_BLOCK))
    src = order // TOP_K
    dst = (order % TOP_K) * t + src
    return block_expert, base.astype(jnp.int32), nvalid.astype(jnp.int32), src, dst, n_used


def _token_mixer_ln(h, hb, w_in, ret_gain, w_pool, pool_scale, w_branch, w_out, layer, ln_g, ln_b,
                    rope_a, rope_r, ret_tables, alpha):
    proj = _inproj(hb, w_in, layer, 0, OFF_GATE)
    gates = _inproj(hb, w_in, layer, OFF_GATE, N_IN - OFF_GATE, gate=True)
    qt, kk, vt = _moba_prep(proj, rope_a)
    y_a = _moba_attn(qt, kk, vt)
    y_r = _retention(proj, rope_r, ret_tables, ret_gain[None, :])
    y_p = _pool(proj, w_pool, layer, pool_scale[None, :])
    merged = _merge(y_a, y_r, y_p, w_branch, layer, gates)
    return _outproj_ln(merged, w_out, layer, h, ln_g[None, :], ln_b[None, :], alpha)


def _moe_ln(h, wr_hi, wr_lo, b_router, w_gate, w_up, w_down, layer, ln_g, ln_b, alpha):
    e_pad, p_pad = _router(h, wr_hi, wr_lo, b_router)
    plan = _dispatch_plan(e_pad[:, :TOP_K], h.shape[0])
    ycomb = _experts(h, w_gate, w_up, w_down, layer, plan)
    return _combine_ln(h, ycomb, p_pad, ln_g[None, :], ln_b[None, :], alpha)


def kernel(x, w_in, ret_gain, w_pool, pool_scale, w_branch, w_out, ln1_g, ln1_b, w_r1, b_r1, w_r2, b_r2, w_e_gate, w_e_up, w_e_down, ln2_g, ln2_b):
    bsz, seq, d = x.shape
    depth = w_in.shape[0]
    assert bsz == 1 and d == D_MODEL and seq % 1024 == 0
    t = seq
    alpha = float((2 * depth) ** 0.25)

    rope_a = _rope_tables(t, ROPE_DIM, ROPE_THETA, ATT_HEAD_DIM)
    rope_r = _rope_tables(t, RET_QK_DIM, RET_ROPE_THETA, RET_QK_DIM)
    ret_tables = _retention_tables()

    w_router = jnp.concatenate([w_r1, w_r2], axis=2)
    w_router = jnp.pad(w_router, ((0, 0), (0, 0), (0, ROUTER_LANES - w_router.shape[2])))
    wr_hi = w_router.astype(BF16)
    wr_lo = (w_router - wr_hi.astype(F32)).astype(BF16)
    b_router = jnp.pad(jnp.concatenate([b_r1, b_r2], axis=1),
                       ((0, 0), (0, ROUTER_LANES - N_GROUPS - N_EXPERTS)))[:, None, :]

    w_pool_b, w_branch_b, w_out_b = w_pool.astype(BF16), w_branch.astype(BF16), w_out.astype(BF16)
    h = x.reshape(t, d)
    hb = h.astype(BF16)
    for l in range(depth):
        h = _token_mixer_ln(h, hb, w_in, ret_gain[l], w_pool_b, pool_scale[l], w_branch_b, w_out_b, l,
                            ln1_g[l], ln1_b[l], rope_a, rope_r, ret_tables, alpha)
        h, hb = _moe_ln(h, wr_hi[l], wr_lo[l], b_router[l], w_e_gate, w_e_up, w_e_down, l,
                        ln2_g[l], ln2_b[l], alpha)
    return h.reshape(bsz, seq, d)
```

```python
import functools

import jax
import jax.numpy as jnp
from jax import lax
from jax.experimental import pallas as pl
from jax.experimental.pallas import tpu as pltpu

D_MODEL = 2048

ATT_HEADS = 8
ATT_HEAD_DIM = 128
ATT_WIDTH = ATT_HEADS * ATT_HEAD_DIM
ROPE_DIM = ATT_HEAD_DIM // 4
ROPE_THETA = 500000.0
MOBA_BLOCK = 256
MOBA_TOPK = 3

RET_HEADS = 8
RET_QK_DIM = 64
RET_V_DIM = 128
RET_QK_WIDTH = RET_HEADS * RET_QK_DIM
RET_V_WIDTH = RET_HEADS * RET_V_DIM
RET_CHUNK = 256
RET_ROPE_THETA = 10000.0

POOL_WINDOWS = (2, 4, 8, 16)
POOL_GROUPS = 4
POOL_GROUP_DIM = 256
POOL_WIDTH = POOL_GROUPS * POOL_GROUP_DIM
POOL_HALO = 32

N_BRANCH = 3
BRANCH_WIDTH = 1024

OFF_QA = 0
OFF_KA = OFF_QA + ATT_WIDTH
OFF_VA = OFF_KA + ATT_WIDTH
OFF_QR = OFF_VA + ATT_WIDTH
OFF_KR = OFF_QR + RET_QK_WIDTH
OFF_VR = OFF_KR + RET_QK_WIDTH
OFF_GR = OFF_VR + RET_V_WIDTH
OFF_POOL = OFF_GR + RET_V_WIDTH
OFF_GATE = OFF_POOL + POOL_WIDTH
N_IN = OFF_GATE + N_BRANCH * D_MODEL

N_GROUPS = 4
EXPERTS_PER_GROUP = 8
N_EXPERTS = N_GROUPS * EXPERTS_PER_GROUP
TOP_K = 2
D_EXPERT = 512
MOE_BLOCK = 256
ROUTER_LANES = 128

LN_EPS = 1e-5
NEG = -1e30
TAKEN = -3e38
LOG2_E = 1.4426950408889634

LANES = 128
SUBLANES = 8
VMEM_LIMIT = 56 * 1024 * 1024

F32 = jnp.float32
BF16 = jnp.bfloat16


def _params(n_axes):
    return pltpu.CompilerParams(dimension_semantics=("arbitrary",) * n_axes,
                                vmem_limit_bytes=VMEM_LIMIT)


def _dot(a, b):
    return jnp.dot(a, b, preferred_element_type=F32)


def _dot_nt(a, b):
    return lax.dot_general(a, b, (((1,), (1,)), ((), ())), preferred_element_type=F32)


def _dot_tn(a, b):
    return lax.dot_general(a, b, (((0,), (0,)), ((), ())), preferred_element_type=F32)


def _split_bf16(a):
    hi = a.astype(BF16)
    lo = (a - hi.astype(F32)).astype(BF16)
    return hi, lo


def _dot_nt_f32ish(a, b):
    ah, al = _split_bf16(a)
    bh, bl = _split_bf16(b)
    return _dot_nt(ah, bh) + (_dot_nt(ah, bl) + _dot_nt(al, bh))


def _rotate(x, c, s1, s2, shift):
    return x * c + pltpu.roll(x, shift, 1) * s1 + pltpu.roll(x, LANES - shift, 1) * s2


def _sigmoid(x):
    return 0.5 * jnp.tanh(0.5 * x) + 0.5


def _layer_norm(v, g, b):
    mu = jnp.mean(v, axis=-1, keepdims=True)
    d = v - mu
    var = jnp.mean(d * d, axis=-1, keepdims=True)
    return d * lax.rsqrt(var + LN_EPS) * g + b


def _inproj_kernel(x_ref, w_ref, o_ref, wb_ref, *, gate):
    @pl.when(pl.program_id(1) == 0)
    def _():
        wb_ref[...] = w_ref[...].astype(BF16)

    y = _dot(x_ref[...], wb_ref[...])
    o_ref[...] = _sigmoid(y).astype(o_ref.dtype) if gate else y


def _inproj(xb, w, layer, col0, ncols, gate=False, tm=1024, tn=1024):
    t, d = xb.shape
    return pl.pallas_call(
        functools.partial(_inproj_kernel, gate=gate),
        grid=(ncols // tn, t // tm),
        in_specs=[pl.BlockSpec((tm, d), lambda j, i: (i, 0)),
                  pl.BlockSpec((None, d, tn), lambda j, i: (layer, 0, col0 // tn + j))],
        out_specs=pl.BlockSpec((tm, tn), lambda j, i: (i, j)),
        out_shape=jax.ShapeDtypeStruct((t, ncols), BF16 if gate else F32),
        scratch_shapes=[pltpu.VMEM((d, tn), BF16)],
        compiler_params=_params(2),
        name="inproj_gates" if gate else "inproj",
    )(xb, w)


ATT_KDIM = 2 * ATT_HEAD_DIM
ATT_VROWS = ATT_HEAD_DIM + 16
ATT_SLOTS = 4


def _moba_prep_kernel(q_ref, k_ref, v_ref, c_ref, s1_ref, s2_ref, qt_ref, ko_ref, vt_ref, km_ref):
    i = pl.program_id(0)
    nblk = km_ref.shape[0]

    @pl.when(i == 0)
    def _():
        km_ref[...] = jnp.zeros_like(km_ref)

    c, s1, s2 = c_ref[...], s1_ref[...], s2_ref[...]
    blk = lax.broadcasted_iota(jnp.int32, (nblk, MOBA_BLOCK), 0)
    blkf = blk.astype(F32)
    scale = ATT_HEAD_DIM ** -0.5 * LOG2_E
    lane = lax.broadcasted_iota(jnp.int32, (MOBA_BLOCK, ATT_KDIM - ATT_HEAD_DIM), 1)
    this_block = jnp.where(lane == i, 1.0, 0.0).astype(BF16)
    row = lax.broadcasted_iota(jnp.int32, (ATT_VROWS - ATT_HEAD_DIM, MOBA_BLOCK), 0)
    ones_row = jnp.where(row == 0, 1.0, 0.0).astype(BF16)
    pad_rows = ATT_KDIM - ATT_HEAD_DIM - nblk
    eye = jnp.where(lax.broadcasted_iota(jnp.int32, (ATT_HEAD_DIM, ATT_HEAD_DIM), 0)
                    == lax.broadcasted_iota(jnp.int32, (ATT_HEAD_DIM, ATT_HEAD_DIM), 1), 1.0, 0.0).astype(BF16)
    for h in range(ATT_HEADS):
        sl = slice(h * ATT_HEAD_DIM, (h + 1) * ATT_HEAD_DIM)
        qr = _rotate(q_ref[:, sl], c, s1, s2, ROPE_DIM // 2)
        kr = _rotate(k_ref[:, sl], c, s1, s2, ROPE_DIM // 2)
        ko_ref[h, 0, :, :ATT_HEAD_DIM] = kr.astype(BF16)
        ko_ref[h, 0, :, ATT_HEAD_DIM:] = this_block
        km_blk = lax.broadcasted_iota(jnp.int32, (nblk, ATT_HEAD_DIM), 0)
        km_ref[:, sl] = jnp.where(km_blk == i, jnp.mean(kr, axis=0, keepdims=True), km_ref[:, sl])
        qt_ref[h, 0, :ATT_HEAD_DIM, :] = _dot_nt(eye, (qr * scale).astype(BF16)).astype(BF16)
        vt_ref[h, 0, :ATT_HEAD_DIM, :] = _dot_nt(eye, v_ref[:, sl].astype(BF16)).astype(BF16)
        vt_ref[h, 0, ATT_HEAD_DIM:, :] = ones_row
        gate = _dot_nt_f32ish(km_ref[:, sl], qr)
        gate = jnp.where(blk < i, gate, NEG)
        for _ in range(MOBA_TOPK):
            top = jnp.max(gate, axis=0, keepdims=True)
            first = jnp.min(jnp.where(gate == top, blkf, float(nblk)), axis=0, keepdims=True)
            gate = jnp.where(blkf == first, TAKEN, gate)
        bias = jnp.where(gate == TAKEN, jnp.where(blk < i, 0.0, NEG), NEG)
        qt_ref[h, 0, ATT_HEAD_DIM:ATT_HEAD_DIM + nblk, :] = bias.astype(BF16)
        if pad_rows:
            qt_ref[h, 0, ATT_HEAD_DIM + nblk:, :] = jnp.zeros((pad_rows, MOBA_BLOCK), BF16)


def _moba_prep(proj, rope):
    t = proj.shape[0]
    nblk = t // MOBA_BLOCK
    assert ATT_HEAD_DIM + nblk <= ATT_KDIM and nblk % 16 == 0
    c, s1, s2 = rope
    tbl = pl.BlockSpec((MOBA_BLOCK, LANES), lambda i: (i, 0))
    hshape = (ATT_HEADS, nblk)
    return pl.pallas_call(
        _moba_prep_kernel,
        grid=(nblk,),
        in_specs=[pl.BlockSpec((MOBA_BLOCK, ATT_WIDTH), lambda i: (i, OFF_QA // ATT_WIDTH)),
                  pl.BlockSpec((MOBA_BLOCK, ATT_WIDTH), lambda i: (i, OFF_KA // ATT_WIDTH)),
                  pl.BlockSpec((MOBA_BLOCK, ATT_WIDTH), lambda i: (i, OFF_VA // ATT_WIDTH)),
                  tbl, tbl, tbl],
        out_specs=[pl.BlockSpec((ATT_HEADS, 1, ATT_KDIM, MOBA_BLOCK), lambda i: (0, i, 0, 0)),
                   pl.BlockSpec((ATT_HEADS, 1, MOBA_BLOCK, ATT_KDIM), lambda i: (0, i, 0, 0)),
                   pl.BlockSpec((ATT_HEADS, 1, ATT_VROWS, MOBA_BLOCK), lambda i: (0, i, 0, 0))],
        out_shape=[jax.ShapeDtypeStruct(hshape + (ATT_KDIM, MOBA_BLOCK), BF16),
                   jax.ShapeDtypeStruct(hshape + (MOBA_BLOCK, ATT_KDIM), BF16),
                   jax.ShapeDtypeStruct(hshape + (ATT_VROWS, MOBA_BLOCK), BF16)],
        scratch_shapes=[pltpu.VMEM((nblk, ATT_WIDTH), F32)],
        compiler_params=_params(1),
        name="moba_prep",
    )(proj, proj, proj, c, s1, s2)


def _moba_attn_kernel(qt_ref, k_ref, vt_ref, o_ref, s_ref, p_ref, a_ref, acc_ref):
    heads = qt_ref.shape[0]
    nblk = k_ref.shape[1]
    shape = (MOBA_BLOCK, MOBA_BLOCK)
    causal = lax.broadcasted_iota(jnp.int32, shape, 0) <= lax.broadcasted_iota(jnp.int32, shape, 1)
    for qb in range(qt_ref.shape[1]):
        _moba_attn_block(qt_ref, k_ref, vt_ref, o_ref, s_ref, p_ref, a_ref, acc_ref, qb, causal)


def _moba_attn_block(qt_ref, k_ref, vt_ref, o_ref, s_ref, p_ref, a_ref, acc_ref, qb, causal):
    i = qt_ref.shape[1] * pl.program_id(1) + qb
    heads = qt_ref.shape[0]
    nblk = k_ref.shape[1]
    shape = (MOBA_BLOCK, MOBA_BLOCK)

    ms = []
    for h in range(heads):
        s = _dot(k_ref[h, i, :, :ATT_HEAD_DIM], qt_ref[h, qb, :ATT_HEAD_DIM, :])
        s_ref[0, h] = jnp.where(causal, s, NEG)
        s_ref[1, h] = _dot(k_ref[h, 0], qt_ref[h, qb])
        ms.append(jnp.full((1, MOBA_BLOCK), NEG, F32))
        acc_ref[h] = jnp.zeros(acc_ref.shape[1:], F32)
        for slot in (2, 3):
            p_ref[slot, h] = jnp.zeros(shape, BF16)
            a_ref[slot, h] = jnp.ones((1, MOBA_BLOCK), F32)

    def scores2(pos, slot):
        jc = jnp.minimum(pos - 1, nblk - 2)
        for h in range(heads):
            kk = k_ref[h, pl.ds(jc, 2)].reshape(2 * MOBA_BLOCK, ATT_KDIM)
            s2 = _dot(kk, qt_ref[h, qb])
            s_ref[slot, h] = s2[:MOBA_BLOCK]
            s_ref[slot + 1, h] = s2[MOBA_BLOCK:]

    def pv(pos, slot):
        jc = jnp.where(pos == 0, i, jnp.clip(pos - 1, 0, nblk - 1))
        for h in range(heads):
            acc_ref[h] = a_ref[slot, h] * acc_ref[h] + _dot(vt_ref[h, jc], p_ref[slot, h])

    def softmax(slot, ms):
        out = []
        for h in range(heads):
            s = s_ref[slot, h]
            m_new = jnp.maximum(ms[h], jnp.max(s, axis=0, keepdims=True))
            a_ref[slot, h] = jnp.exp2(ms[h] - m_new)
            p_ref[slot, h] = jnp.exp2(s - m_new).astype(BF16)
            out.append(m_new)
        return out

    def body(jj, ms):
        b = 4 * jj
        pv(b - 2, 2)
        pv(b - 1, 3)
        scores2(b + 2, 2)
        ms = softmax(0, ms)
        ms = softmax(1, ms)
        pv(b, 0)
        pv(b + 1, 1)
        scores2(b + 4, 0)
        ms = softmax(2, ms)
        return softmax(3, ms)

    npos = i + 1
    trips = lax.shift_right_logical(npos + 1, 2)
    ms = lax.fori_loop(0, trips, body, ms)
    b = 4 * trips
    pv(b - 2, 2)
    pv(b - 1, 3)

    @pl.when(npos - b >= 1)
    def _():
        softmax(1, softmax(0, ms))
        pv(b, 0)
        pv(b + 1, 1)

    for h in range(heads):
        acc = acc_ref[h]
        out = acc[:ATT_HEAD_DIM] / acc[ATT_HEAD_DIM:ATT_HEAD_DIM + 1]
        o_ref[qb * MOBA_BLOCK:(qb + 1) * MOBA_BLOCK, h * ATT_HEAD_DIM:(h + 1) * ATT_HEAD_DIM] = out.T.astype(o_ref.dtype)


def _moba_attn(qt, k, vt, heads=2, qblocks=2):
    nblk = qt.shape[1]
    t = nblk * MOBA_BLOCK
    blk2 = (MOBA_BLOCK, MOBA_BLOCK)
    return pl.pallas_call(
        _moba_attn_kernel,
        grid=(ATT_HEADS // heads, nblk // qblocks),
        in_specs=[pl.BlockSpec((heads, qblocks, ATT_KDIM, MOBA_BLOCK), lambda h, i: (h, i, 0, 0)),
                  pl.BlockSpec((heads, nblk, MOBA_BLOCK, ATT_KDIM), lambda h, i: (h, 0, 0, 0)),
                  pl.BlockSpec((heads, nblk, ATT_VROWS, MOBA_BLOCK), lambda h, i: (h, 0, 0, 0))],
        out_specs=pl.BlockSpec((qblocks * MOBA_BLOCK, heads * ATT_HEAD_DIM), lambda h, i: (i, h)),
        out_shape=jax.ShapeDtypeStruct((t, ATT_WIDTH), BF16),
        scratch_shapes=[pltpu.VMEM((ATT_SLOTS, heads) + blk2, F32),
                        pltpu.VMEM((ATT_SLOTS, heads) + blk2, BF16),
                        pltpu.VMEM((ATT_SLOTS, heads, 1, MOBA_BLOCK), F32),
                        pltpu.VMEM((heads, ATT_VROWS, MOBA_BLOCK), F32)],
        compiler_params=_params(2),
        name="moba_attn",
    )(qt, k, vt)


def _retention_kernel(q_ref, k_ref, v_ref, g_ref, c_ref, s1_ref, s2_ref,
                      dec_ref, kdec_ref, qdec_ref, cdec_ref, gain_ref, o_ref, state_ref):
    @pl.when(pl.program_id(1) == 0)
    def _():
        state_ref[...] = jnp.zeros_like(state_ref)

    pairs = kdec_ref.shape[0]
    heads = 2 * pairs
    c, s1, s2 = c_ref[...], s1_ref[...], s2_ref[...]
    lane = lax.broadcasted_iota(jnp.int32, (RET_CHUNK, LANES), 1)

    def vslice(h):
        return slice(h * RET_V_DIM, (h + 1) * RET_V_DIM)

    qm, qdm, kb, kdt = [], [], [], []
    for p in range(pairs):
        psl = slice(p * LANES, (p + 1) * LANES)
        q = _rotate(q_ref[:, psl], c, s1, s2, RET_QK_DIM // 2)
        k = _rotate(k_ref[:, psl], c, s1, s2, RET_QK_DIM // 2) * (RET_QK_DIM ** -0.5)
        qd = q * qdec_ref[p]
        kd = k * kdec_ref[p]
        kb.append(k.astype(BF16))
        for hh in range(2):
            mine = (lane < RET_QK_DIM) if hh == 0 else (lane >= RET_QK_DIM)
            qm.append(jnp.where(mine, q, 0.0).astype(BF16))
            qdm.append(jnp.where(mine, qd, 0.0).astype(BF16))
            kdt.append(jnp.where(mine, kd, 0.0).astype(BF16))
    vb = [v_ref[:, vslice(h)].astype(BF16) for h in range(heads)]
    inner = [_dot_nt(qm[h], kb[h // 2]) for h in range(heads)]
    cross = [_dot(qdm[h], state_ref[h].astype(BF16)) for h in range(heads)]
    for h in range(heads):
        state_ref[h] = state_ref[h] * cdec_ref[h] + _dot_tn(kdt[h], vb[h])
    out = [_dot((inner[h] * dec_ref[h]).astype(BF16), vb[h]) + cross[h] for h in range(heads)]
    for h in range(heads):
        mu = jnp.mean(out[h], axis=-1, keepdims=True)
        d = out[h] - mu
        var = jnp.mean(d * d, axis=-1, keepdims=True)
        rn = d * lax.rsqrt(var + LN_EPS) * gain_ref[:, vslice(h)]
        g = g_ref[:, vslice(h)]
        o_ref[:, vslice(h)] = (g * _sigmoid(g) * rn).astype(o_ref.dtype)


def _retention(proj, rope, tables, gain, pairs=4):
    t = proj.shape[0]
    n = t // RET_CHUNK
    c, s1, s2 = rope
    dec, kdec, qdec, cdec = tables
    qk_w = pairs * LANES
    v_w = 2 * pairs * RET_V_DIM
    tbl = pl.BlockSpec((RET_CHUNK, LANES), lambda hp, ci: (ci, 0))
    return pl.pallas_call(
        _retention_kernel,
        grid=(RET_HEADS // (2 * pairs), n),
        in_specs=[pl.BlockSpec((RET_CHUNK, qk_w), lambda hp, ci: (ci, OFF_QR // qk_w + hp)),
                  pl.BlockSpec((RET_CHUNK, qk_w), lambda hp, ci: (ci, OFF_KR // qk_w + hp)),
                  pl.BlockSpec((RET_CHUNK, v_w), lambda hp, ci: (ci, OFF_VR // v_w + hp)),
                  pl.BlockSpec((RET_CHUNK, v_w), lambda hp, ci: (ci, OFF_GR // v_w + hp)),
                  tbl, tbl, tbl,
                  pl.BlockSpec((2 * pairs, RET_CHUNK, RET_CHUNK), lambda hp, ci: (hp, 0, 0)),
                  pl.BlockSpec((pairs, RET_CHUNK, LANES), lambda hp, ci: (hp, 0, 0)),
                  pl.BlockSpec((pairs, RET_CHUNK, LANES), lambda hp, ci: (hp, 0, 0)),
                  pl.BlockSpec((2 * pairs, 1, RET_V_DIM), lambda hp, ci: (hp, 0, 0)),
                  pl.BlockSpec((1, v_w), lambda hp, ci: (0, hp))],
        out_specs=pl.BlockSpec((RET_CHUNK, v_w), lambda hp, ci: (ci, hp)),
        out_shape=jax.ShapeDtypeStruct((t, RET_V_WIDTH), BF16),
        scratch_shapes=[pltpu.VMEM((2 * pairs, LANES, RET_V_DIM), F32)],
        compiler_params=_params(2),
        name="retention",
    )(proj, proj, proj, proj, c, s1, s2, dec, kdec, qdec, cdec, gain)


def _pool_kernel(cur_ref, prev_ref, w_ref, scale_ref, o_ref, buf_a, buf_b, *, tb):
    i = pl.program_id(0)
    g = pl.program_id(1)
    x = cur_ref[...]
    buf_a[0:POOL_HALO, :] = jnp.where(i > 0, prev_ref[...], 0.0)
    buf_a[POOL_HALO:, :] = x
    rows = tb + POOL_HALO
    pos = (i * tb + 1 + lax.broadcasted_iota(jnp.int32, (tb, 1), 0)).astype(F32)
    for gi, w in enumerate(POOL_WINDOWS):
        @pl.when(g == gi)
        def _(w=w):
            src, dst = buf_a, buf_b
            k, lo = 1, SUBLANES
            while k < w:
                dst[lo:rows, :] = src[lo:rows, :] + src[lo - k:rows - k, :]
                src, dst = dst, src
                k, lo = 2 * k, lo + SUBLANES
            tot = src[POOL_HALO:rows, :]
            d = tot / jnp.minimum(pos, float(w)) - x
            y = _dot(d.astype(BF16), w_ref[0])
            o_ref[...] = (y * scale_ref[...]).astype(o_ref.dtype)


def _pool(proj, w_pool, layer, scale, tb=2048):
    t = proj.shape[0]
    tb = min(tb, t)
    gd = POOL_GROUP_DIM
    assert SUBLANES * (max(POOL_WINDOWS).bit_length() - 1) <= POOL_HALO
    return pl.pallas_call(
        functools.partial(_pool_kernel, tb=tb),
        grid=(t // tb, POOL_GROUPS),
        in_specs=[pl.BlockSpec((tb, gd), lambda i, g: (i, OFF_POOL // gd + g)),
                  pl.BlockSpec((POOL_HALO, gd),
                               lambda i, g: (jnp.maximum(i * (tb // POOL_HALO) - 1, 0), OFF_POOL // gd + g)),
                  pl.BlockSpec((None, 1, gd, gd), lambda i, g: (layer, g, 0, 0)),
                  pl.BlockSpec((1, gd), lambda i, g: (0, g))],
        out_specs=pl.BlockSpec((tb, gd), lambda i, g: (i, g)),
        out_shape=jax.ShapeDtypeStruct((t, POOL_WIDTH), BF16),
        scratch_shapes=[pltpu.VMEM((tb + POOL_HALO, gd), F32), pltpu.VMEM((tb + POOL_HALO, gd), F32)],
        compiler_params=_params(2),
        name="pool",
    )(proj, proj, w_pool, scale)


def _merge_kernel(ya_ref, yr_ref, yp_ref, w_ref, ga_ref, gr_ref, gp_ref, o_ref):
    acc = ga_ref[...] * _dot(ya_ref[...], w_ref[0])
    acc = acc + gr_ref[...] * _dot(yr_ref[...], w_ref[1])
    acc = acc + gp_ref[...] * _dot(yp_ref[...], w_ref[2])
    o_ref[...] = acc.astype(o_ref.dtype)


def _merge(ya, yr, yp, w_branch, layer, gates, tm=1024, tn=1024):
    t = ya.shape[0]
    d = w_branch.shape[3]
    br = pl.BlockSpec((tm, BRANCH_WIDTH), lambda i, j: (i, 0))

    def gate(nb):
        return pl.BlockSpec((tm, tn), lambda i, j: (i, (nb * d) // tn + j))

    return pl.pallas_call(
        _merge_kernel,
        grid=(t // tm, d // tn),
        in_specs=[br, br, br,
                  pl.BlockSpec((None, N_BRANCH, BRANCH_WIDTH, tn), lambda i, j: (layer, 0, 0, j)),
                  gate(0), gate(1), gate(2)],
        out_specs=pl.BlockSpec((tm, tn), lambda i, j: (i, j)),
        out_shape=jax.ShapeDtypeStruct((t, d), BF16),
        compiler_params=_params(2),
        name="merge",
    )(ya, yr, yp, w_branch, gates, gates, gates)


def _outproj_ln_kernel(m_ref, w_ref, x_ref, g_ref, b_ref, o_ref, *, alpha):
    half = m_ref.shape[0] // 2
    for r in range(2):
        sl = slice(r * half, (r + 1) * half)
        y = _dot(m_ref[sl, :], w_ref[...])
        o_ref[sl, :] = _layer_norm(alpha * x_ref[sl, :] + y, g_ref[...], b_ref[...])


def _outproj_ln(merged, w_out, layer, x, g, b, alpha, tm=512):
    t, d = x.shape
    row = pl.BlockSpec((tm, d), lambda i: (i, 0))
    vec = pl.BlockSpec((1, d), lambda i: (0, 0))
    return pl.pallas_call(
        functools.partial(_outproj_ln_kernel, alpha=alpha),
        grid=(t // tm,),
        in_specs=[row, pl.BlockSpec((None, d, d), lambda i: (layer, 0, 0)), row, vec, vec],
        out_specs=row,
        out_shape=jax.ShapeDtypeStruct((t, d), F32),
        compiler_params=_params(1),
        name="outproj_ln",
    )(merged, w_out, x, g, b)


def _router_kernel(x_ref, wh_ref, wl_ref, b_ref, e_ref, p_ref):
    x = x_ref[...]
    xh, xl = _split_bf16(x)
    z = _dot(xh, wh_ref[...]) + (_dot(xh, wl_ref[...]) + _dot(xl, wh_ref[...])) + b_ref[...]
    lane = lax.broadcasted_iota(jnp.int32, z.shape, 1)
    big = jnp.int32(ROUTER_LANES)

    def masked_softmax(mask):
        zm = jnp.where(mask, z, NEG)
        e = jnp.where(mask, jnp.exp(zm - jnp.max(zm, axis=-1, keepdims=True)), 0.0)
        return e / jnp.sum(e, axis=-1, keepdims=True)

    def top1(p, mask):
        pm = jnp.where(mask, p, -1.0)
        top = jnp.max(pm, axis=-1, keepdims=True)
        idx = jnp.min(jnp.where(pm == top, lane, big), axis=-1, keepdims=True)
        return top, idx

    in_groups = lane < N_GROUPS
    p1 = masked_softmax(in_groups)
    g_top, g_idx = top1(p1, in_groups)
    lo = N_GROUPS + g_idx * EXPERTS_PER_GROUP
    in_group = (lane >= lo) & (lane < lo + EXPERTS_PER_GROUP)
    p2 = masked_softmax(in_group)
    e1, i1 = top1(p2, in_group)
    e2, i2 = top1(p2, in_group & (lane != i1))
    denom = e1 + e2
    w1 = g_top * e1 / denom
    w2 = g_top * e2 / denom
    e_ref[...] = jnp.where(lane == 0, i1 - N_GROUPS, jnp.where(lane == 1, i2 - N_GROUPS, 0))
    p_ref[...] = jnp.where(lane == 0, w1, jnp.where(lane == 1, w2, 0.0))


def _router(x, w_hi, w_lo, bias, tm=1024):
    t, d = x.shape
    row = pl.BlockSpec((tm, d), lambda i: (i, 0))
    wsp = pl.BlockSpec((d, ROUTER_LANES), lambda i: (0, 0))
    out = pl.BlockSpec((tm, ROUTER_LANES), lambda i: (i, 0))
    return pl.pallas_call(
        _router_kernel,
        grid=(t // tm,),
        in_specs=[row, wsp, wsp, pl.BlockSpec((1, ROUTER_LANES), lambda i: (0, 0))],
        out_specs=[out, out],
        out_shape=[jax.ShapeDtypeStruct((t, ROUTER_LANES), jnp.int32),
                   jax.ShapeDtypeStruct((t, ROUTER_LANES), F32)],
        compiler_params=_params(1),
        name="router",
    )(x, w_hi, w_lo, bias)


WAIT_CHUNKS = (64, 8, 1)


def _start_rows(n, copy):
    for r in range(MOE_BLOCK):
        @pl.when(r < n)
        def _(r=r):
            copy(r).start()


def _wait_rows(n, span):
    rem = n
    for c in WAIT_CHUNKS:
        cnt = lax.shift_right_logical(rem, c.bit_length() - 1)

        def one(_, carry, c=c):
            span(c).wait()
            return carry

        lax.fori_loop(0, cnt, one, 0)
        rem = rem & (c - 1)


def _experts_kernel(be_ref, base_ref, nvalid_ref, src_ref, dst_ref, nused_ref,
                    x_hbm, wg_ref, wu_ref, wd_ref, y_hbm,
                    xbuf, ybuf, wgb, wub, wdb, sem_in, sem_out):
    b = pl.program_id(0)
    n_used = nused_ref[0]
    slot = b & 1

    def start_gather(bb, sl):
        base = base_ref[bb]
        _start_rows(nvalid_ref[bb], lambda r: pltpu.make_async_copy(
            x_hbm.at[pl.ds(src_ref[base + r], 1), :], xbuf.at[sl, pl.ds(r, 1), :], sem_in.at[sl]))

    def wait_gather(bb, sl):
        _wait_rows(nvalid_ref[bb], lambda c: pltpu.make_async_copy(
            x_hbm.at[pl.ds(0, c), :], xbuf.at[sl, pl.ds(0, c), :], sem_in.at[sl]))

    def start_scatter(bb, sl):
        base = base_ref[bb]
        _start_rows(nvalid_ref[bb], lambda r: pltpu.make_async_copy(
            ybuf.at[sl, pl.ds(r, 1), :], y_hbm.at[pl.ds(dst_ref[base + r], 1), :], sem_out.at[sl]))

    def wait_scatter(bb, sl):
        _wait_rows(nvalid_ref[bb], lambda c: pltpu.make_async_copy(
            ybuf.at[sl, pl.ds(0, c), :], y_hbm.at[pl.ds(0, c), :], sem_out.at[sl]))

    @pl.when(b == 0)
    def _():
        xbuf[...] = jnp.zeros_like(xbuf)
        start_gather(0, 0)

    @pl.when(b < n_used)
    def _():
        @pl.when(b + 1 < n_used)
        def _():
            start_gather(b + 1, 1 - slot)

        new_expert = (b == 0) | (be_ref[b] != be_ref[jnp.maximum(b - 1, 0)])

        @pl.when(new_expert)
        def _():
            wgb[...] = wg_ref[...].astype(BF16)
            wub[...] = wu_ref[...].astype(BF16)
            wdb[...] = wd_ref[...].astype(BF16)

        wait_gather(b, slot)

        @pl.when(b >= 2)
        def _():
            wait_scatter(b - 2, slot)

        x = xbuf[slot].astype(BF16)
        gate = _dot(x, wgb[...])
        up = _dot(x, wub[...])
        hid = (gate * _sigmoid(gate) * up).astype(BF16)
        ybuf[slot] = _dot(hid, wdb[...])
        start_scatter(b, slot)

        @pl.when(b == n_used - 1)
        def _():
            @pl.when(b >= 1)
            def _():
                wait_scatter(b - 1, 1 - slot)

            wait_scatter(b, slot)


def _experts(x, wg, wu, wd, layer, plan):
    t, d = x.shape
    block_expert, base, nvalid, src, dst, n_used = plan
    n_blocks = block_expert.shape[0]

    def wspec(shape):
        return pl.BlockSpec((None, None) + shape, lambda b, be, *_: (layer, be[b], 0, 0))

    grid_spec = pltpu.PrefetchScalarGridSpec(
        num_scalar_prefetch=6,
        grid=(n_blocks,),
        in_specs=[pl.BlockSpec(memory_space=pl.ANY),
                  wspec((d, D_EXPERT)), wspec((d, D_EXPERT)), wspec((D_EXPERT, d))],
        out_specs=pl.BlockSpec(memory_space=pl.ANY),
        scratch_shapes=[pltpu.VMEM((2, MOE_BLOCK, d), F32),
                        pltpu.VMEM((2, MOE_BLOCK, d), F32),
                        pltpu.VMEM((d, D_EXPERT), BF16),
                        pltpu.VMEM((d, D_EXPERT), BF16),
                        pltpu.VMEM((D_EXPERT, d), BF16),
                        pltpu.SemaphoreType.DMA((2,)),
                        pltpu.SemaphoreType.DMA((2,))],
    )
    y = pl.pallas_call(
        _experts_kernel,
        grid_spec=grid_spec,
        out_shape=jax.ShapeDtypeStruct((TOP_K * t, d), F32),
        compiler_params=_params(1),
        name="experts",
    )(block_expert, base, nvalid, src, dst, n_used, x, wg, wu, wd)
    return y.reshape(TOP_K, t, d)


def _combine_ln_kernel(x_ref, y0_ref, y1_ref, p_ref, g_ref, b_ref, o_ref, ob_ref, *, alpha):
    p = p_ref[...]
    y = p[:, 0:1] * y0_ref[...] + p[:, 1:2] * y1_ref[...]
    out = _layer_norm(alpha * x_ref[...] + y, g_ref[...], b_ref[...])
    o_ref[...] = out
    ob_ref[...] = out.astype(BF16)


def _combine_ln(x, ycomb, probs, g, b, alpha, tm=512):
    t, d = x.shape
    row = pl.BlockSpec((tm, d), lambda i: (i, 0))
    vec = pl.BlockSpec((1, d), lambda i: (0, 0))
    return pl.pallas_call(
        functools.partial(_combine_ln_kernel, alpha=alpha),
        grid=(t // tm,),
        in_specs=[row,
                  pl.BlockSpec((None, tm, d), lambda i: (0, i, 0)),
                  pl.BlockSpec((None, tm, d), lambda i: (1, i, 0)),
                  pl.BlockSpec((tm, ROUTER_LANES), lambda i: (i, 0)), vec, vec],
        out_specs=[row, row],
        out_shape=[jax.ShapeDtypeStruct((t, d), F32), jax.ShapeDtypeStruct((t, d), BF16)],
        compiler_params=_params(1),
        name="combine_ln",
    )(x, ycomb, ycomb, probs, g, b)


def _rope_tables(t, rot_dim, theta, head_dim):
    half = rot_dim // 2
    inv_freq = 1.0 / (theta ** (jnp.arange(0, rot_dim, 2, dtype=F32) / rot_dim))
    ang = jnp.arange(t, dtype=F32)[:, None] * inv_freq[None, :]
    cos, sin = jnp.cos(ang), jnp.sin(ang)
    rest = head_dim - rot_dim
    ones, zeros = jnp.ones((t, rest), F32), jnp.zeros((t, rest), F32)
    zh = jnp.zeros((t, half), F32)
    c = jnp.concatenate([cos, cos, ones], axis=1)
    s1 = jnp.concatenate([zh, sin, zeros], axis=1)
    s2 = jnp.concatenate([-sin, zh, zeros], axis=1)
    rep = LANES // head_dim
    return tuple(jnp.tile(a, (1, rep)) for a in (c, s1, s2))


def _retention_tables():
    h, c = RET_HEADS, RET_CHUNK
    log_gamma = jnp.log1p(-jnp.exp2(-5.0 - jnp.arange(h, dtype=F32)))
    pos = jnp.arange(c, dtype=F32)
    diff = pos[:, None] - pos[None, :]
    decay = jnp.where(diff >= 0, jnp.exp(log_gamma[:, None, None] * jnp.maximum(diff, 0.0)), 0.0)
    k_decay = jnp.exp(log_gamma[:, None] * (c - 1.0 - pos)[None, :])
    q_decay = jnp.exp(log_gamma[:, None] * (pos + 1.0)[None, :])
    chunk_decay = jnp.exp(log_gamma * c)

    def pair_lanes(a):
        a = a.reshape(h // 2, 2, c).transpose(0, 2, 1)
        return jnp.repeat(a, RET_QK_DIM, axis=2)

    cdec = jnp.broadcast_to(chunk_decay[:, None, None], (h, 1, RET_V_DIM))
    return decay, pair_lanes(k_decay), pair_lanes(q_decay), cdec


def _dispatch_plan(expert, t):
    a = t * TOP_K
    flat_e = expert.reshape(a)
    order = jnp.argsort(flat_e).astype(jnp.int32)
    counts = jnp.sum(flat_e[None, :] == jnp.arange(N_EXPERTS, dtype=flat_e.dtype)[:, None],
                     axis=1, dtype=jnp.int32)
    start = jnp.cumsum(counts) - counts
    blocks = (counts + MOE_BLOCK - 1) // MOE_BLOCK
    blk_end = jnp.cumsum(blocks)
    n_blocks = -(-a // MOE_BLOCK) + N_EXPERTS
    b = jnp.arange(n_blocks, dtype=jnp.int32)
    block_expert = jnp.minimum(jnp.sum(blk_end[None, :] <= b[:, None], axis=1, dtype=jnp.int32),
                               N_EXPERTS - 1)
    first = (b - (blk_end - blocks)[block_expert]) * MOE_BLOCK
    base = start[block_expert] + first
    nvalid = jnp.clip(counts[block_expert] - first, 0, MOE_BLOCK)
    nvalid = jnp.where(b < blk_end[-1], nvalid, 0)
    n_used = blk_end[-1:].astype(jnp.int32)
    order = jnp.pad(order, (0, MOE_BLOCK))
    src = order // TOP_K
    dst = (order % TOP_K) * t + src
    return block_expert, base.astype(jnp.int32), nvalid.astype(jnp.int32), src, dst, n_used


def _token_mixer_ln(h, hb, w_in, ret_gain, w_pool, pool_scale, w_branch, w_out, layer, ln_g, ln_b,
                    rope_a, rope_r, ret_tables, alpha):
    proj = _inproj(hb, w_in, layer, 0, OFF_GATE)
    gates = _inproj(hb, w_in, layer, OFF_GATE, N_IN - OFF_GATE, gate=True)
    qt, kk, vt = _moba_prep(proj, rope_a)
    y_a = _moba_attn(qt, kk, vt)
    y_r = _retention(proj, rope_r, ret_tables, ret_gain[None, :])
    y_p = _pool(proj, w_pool, layer, pool_scale[None, :])
    merged = _merge(y_a, y_r, y_p, w_branch, layer, gates)
    return _outproj_ln(merged, w_out, layer, h, ln_g[None, :], ln_b[None, :], alpha)


def _moe_ln(h, wr_hi, wr_lo, b_router, w_gate, w_up, w_down, layer, ln_g, ln_b, alpha):
    e_pad, p_pad = _router(h, wr_hi, wr_lo, b_router)
    plan = _dispatch_plan(e_pad[:, :TOP_K], h.shape[0])
    ycomb = _experts(h, w_gate, w_up, w_down, layer, plan)
    return _combine_ln(h, ycomb, p_pad, ln_g[None, :], ln_b[None, :], alpha)


def kernel(x, w_in, ret_gain, w_pool, pool_scale, w_branch, w_out, ln1_g, ln1_b, w_r1, b_r1, w_r2, b_r2, w_e_gate, w_e_up, w_e_down, ln2_g, ln2_b):
    bsz, seq, d = x.shape
    depth = w_in.shape[0]
    assert bsz == 1 and d == D_MODEL and seq % 1024 == 0
    t = seq
    alpha = float((2 * depth) ** 0.25)

    rope_a = _rope_tables(t, ROPE_DIM, ROPE_THETA, ATT_HEAD_DIM)
    rope_r = _rope_tables(t, RET_QK_DIM, RET_ROPE_THETA, RET_QK_DIM)
    ret_tables = _retention_tables()

    w_router = jnp.concatenate([w_r1, w_r2], axis=2)
    w_router = jnp.pad(w_router, ((0, 0), (0, 0), (0, ROUTER_LANES - w_router.shape[2])))
    wr_hi = w_router.astype(BF16)
    wr_lo = (w_router - wr_hi.astype(F32)).astype(BF16)
    b_router = jnp.pad(jnp.concatenate([b_r1, b_r2], axis=1),
                       ((0, 0), (0, ROUTER_LANES - N_GROUPS - N_EXPERTS)))[:, None, :]

    w_pool_b, w_branch_b, w_out_b = w_pool.astype(BF16), w_branch.astype(BF16), w_out.astype(BF16)
    h = x.reshape(t, d)
    hb = h.astype(BF16)
    for l in range(depth):
        h = _token_mixer_ln(h, hb, w_in, ret_gain[l], w_pool_b, pool_scale[l], w_branch_b, w_out_b, l,
                            ln1_g[l], ln1_b[l], rope_a, rope_r, ret_tables, alpha)
        h, hb = _moe_ln(h, wr_hi[l], wr_lo[l], b_router[l], w_e_gate, w_e_up, w_e_down, l,
                        ln2_g[l], ln2_b[l], alpha)
    return h.reshape(bsz, seq, d)
```

```python
import functools

import jax
import jax.numpy as jnp
from jax import lax
from jax.experimental import pallas as pl
from jax.experimental.pallas import tpu as pltpu

D_MODEL = 2048

ATT_HEADS = 8
ATT_HEAD_DIM = 128
ATT_WIDTH = ATT_HEADS * ATT_HEAD_DIM
ROPE_DIM = ATT_HEAD_DIM // 4
ROPE_THETA = 500000.0
MOBA_BLOCK = 256
MOBA_TOPK = 3

RET_HEADS = 8
RET_QK_DIM = 64
RET_V_DIM = 128
RET_QK_WIDTH = RET_HEADS * RET_QK_DIM
RET_V_WIDTH = RET_HEADS * RET_V_DIM
RET_CHUNK = 256
RET_ROPE_THETA = 10000.0

POOL_WINDOWS = (2, 4, 8, 16)
POOL_GROUPS = 4
POOL_GROUP_DIM = 256
POOL_WIDTH = POOL_GROUPS * POOL_GROUP_DIM
POOL_HALO = 32

N_BRANCH = 3
BRANCH_WIDTH = 1024

OFF_QA = 0
OFF_KA = OFF_QA + ATT_WIDTH
OFF_VA = OFF_KA + ATT_WIDTH
OFF_QR = OFF_VA + ATT_WIDTH
OFF_KR = OFF_QR + RET_QK_WIDTH
OFF_VR = OFF_KR + RET_QK_WIDTH
OFF_GR = OFF_VR + RET_V_WIDTH
OFF_POOL = OFF_GR + RET_V_WIDTH
OFF_GATE = OFF_POOL + POOL_WIDTH
N_IN = OFF_GATE + N_BRANCH * D_MODEL

N_GROUPS = 4
EXPERTS_PER_GROUP = 8
N_EXPERTS = N_GROUPS * EXPERTS_PER_GROUP
TOP_K = 2
D_EXPERT = 512
MOE_BLOCK = 256
ROUTER_LANES = 128

LN_EPS = 1e-5
NEG = -1e30
TAKEN = -3e38
LOG2_E = 1.4426950408889634

LANES = 128
SUBLANES = 8
VMEM_LIMIT = 56 * 1024 * 1024

F32 = jnp.float32
BF16 = jnp.bfloat16


def _params(n_axes):
    return pltpu.CompilerParams(dimension_semantics=("arbitrary",) * n_axes,
                                vmem_limit_bytes=VMEM_LIMIT)


def _dot(a, b):
    return jnp.dot(a, b, preferred_element_type=F32)


def _dot_nt(a, b):
    return lax.dot_general(a, b, (((1,), (1,)), ((), ())), preferred_element_type=F32)


def _dot_tn(a, b):
    return lax.dot_general(a, b, (((0,), (0,)), ((), ())), preferred_element_type=F32)


def _split_bf16(a):
    hi = a.astype(BF16)
    lo = (a - hi.astype(F32)).astype(BF16)
    return hi, lo


def _dot_nt_f32ish(a, b):
    ah, al = _split_bf16(a)
    bh, bl = _split_bf16(b)
    return _dot_nt(ah, bh) + (_dot_nt(ah, bl) + _dot_nt(al, bh))


def _rotate(x, c, s1, s2, shift):
    return x * c + pltpu.roll(x, shift, 1) * s1 + pltpu.roll(x, LANES - shift, 1) * s2


def _sigmoid(x):
    return 0.5 * jnp.tanh(0.5 * x) + 0.5


def _layer_norm(v, g, b):
    mu = jnp.mean(v, axis=-1, keepdims=True)
    d = v - mu
    var = jnp.mean(d * d, axis=-1, keepdims=True)
    return d * lax.rsqrt(var + LN_EPS) * g + b


def _inproj_kernel(x_ref, w_ref, o_ref, wb_ref, *, gate):
    @pl.when(pl.program_id(1) == 0)
    def _():
        wb_ref[...] = w_ref[...].astype(BF16)

    y = _dot(x_ref[...], wb_ref[...])
    o_ref[...] = _sigmoid(y).astype(o_ref.dtype) if gate else y


def _inproj(xb, w, layer, col0, ncols, gate=False, tm=1024, tn=1024):
    t, d = xb.shape
    return pl.pallas_call(
        functools.partial(_inproj_kernel, gate=gate),
        grid=(ncols // tn, t // tm),
        in_specs=[pl.BlockSpec((tm, d), lambda j, i: (i, 0)),
                  pl.BlockSpec((None, d, tn), lambda j, i: (layer, 0, col0 // tn + j))],
        out_specs=pl.BlockSpec((tm, tn), lambda j, i: (i, j)),
        out_shape=jax.ShapeDtypeStruct((t, ncols), BF16 if gate else F32),
        scratch_shapes=[pltpu.VMEM((d, tn), BF16)],
        compiler_params=_params(2),
        name="inproj_gates" if gate else "inproj",
    )(xb, w)


ATT_KDIM = 2 * ATT_HEAD_DIM
ATT_VROWS = ATT_HEAD_DIM + 16
ATT_SLOTS = 4


def _moba_prep_kernel(q_ref, k_ref, v_ref, c_ref, s1_ref, s2_ref, qt_ref, ko_ref, vt_ref, km_ref):
    i = pl.program_id(0)
    nblk = km_ref.shape[0]

    @pl.when(i == 0)
    def _():
        km_ref[...] = jnp.zeros_like(km_ref)

    c, s1, s2 = c_ref[...], s1_ref[...], s2_ref[...]
    blk = lax.broadcasted_iota(jnp.int32, (nblk, MOBA_BLOCK), 0)
    blkf = blk.astype(F32)
    scale = ATT_HEAD_DIM ** -0.5 * LOG2_E
    lane = lax.broadcasted_iota(jnp.int32, (MOBA_BLOCK, ATT_KDIM - ATT_HEAD_DIM), 1)
    this_block = jnp.where(lane == i, 1.0, 0.0).astype(BF16)
    row = lax.broadcasted_iota(jnp.int32, (ATT_VROWS - ATT_HEAD_DIM, MOBA_BLOCK), 0)
    ones_row = jnp.where(row == 0, 1.0, 0.0).astype(BF16)
    pad_rows = ATT_KDIM - ATT_HEAD_DIM - nblk
    eye = jnp.where(lax.broadcasted_iota(jnp.int32, (ATT_HEAD_DIM, ATT_HEAD_DIM), 0)
                    == lax.broadcasted_iota(jnp.int32, (ATT_HEAD_DIM, ATT_HEAD_DIM), 1), 1.0, 0.0).astype(BF16)
    for h in range(ATT_HEADS):
        sl = slice(h * ATT_HEAD_DIM, (h + 1) * ATT_HEAD_DIM)
        qr = _rotate(q_ref[:, sl], c, s1, s2, ROPE_DIM // 2)
        kr = _rotate(k_ref[:, sl], c, s1, s2, ROPE_DIM // 2)
        ko_ref[h, 0, :, :ATT_HEAD_DIM] = kr.astype(BF16)
        ko_ref[h, 0, :, ATT_HEAD_DIM:] = this_block
        km_blk = lax.broadcasted_iota(jnp.int32, (nblk, ATT_HEAD_DIM), 0)
        km_ref[:, sl] = jnp.where(km_blk == i, jnp.mean(kr, axis=0, keepdims=True), km_ref[:, sl])
        qt_ref[h, 0, :ATT_HEAD_DIM, :] = _dot_nt(eye, (qr * scale).astype(BF16)).astype(BF16)
        vt_ref[h, 0, :ATT_HEAD_DIM, :] = _dot_nt(eye, v_ref[:, sl].astype(BF16)).astype(BF16)
        vt_ref[h, 0, ATT_HEAD_DIM:, :] = ones_row
        gate = _dot_nt_f32ish(km_ref[:, sl], qr)
        gate = jnp.where(blk < i, gate, NEG)
        for _ in range(MOBA_TOPK):
            top = jnp.max(gate, axis=0, keepdims=True)
            first = jnp.min(jnp.where(gate == top, blkf, float(nblk)), axis=0, keepdims=True)
            gate = jnp.where(blkf == first, TAKEN, gate)
        bias = jnp.where(gate == TAKEN, jnp.where(blk < i, 0.0, NEG), NEG)
        qt_ref[h, 0, ATT_HEAD_DIM:ATT_HEAD_DIM + nblk, :] = bias.astype(BF16)
        if pad_rows:
            qt_ref[h, 0, ATT_HEAD_DIM + nblk:, :] = jnp.zeros((pad_rows, MOBA_BLOCK), BF16)


def _moba_prep(proj, rope):
    t = proj.shape[0]
    nblk = t // MOBA_BLOCK
    assert ATT_HEAD_DIM + nblk <= ATT_KDIM and nblk % 16 == 0
    c, s1, s2 = rope
    tbl = pl.BlockSpec((MOBA_BLOCK, LANES), lambda i: (i, 0))
    hshape = (ATT_HEADS, nblk)
    return pl.pallas_call(
        _moba_prep_kernel,
        grid=(nblk,),
        in_specs=[pl.BlockSpec((MOBA_BLOCK, ATT_WIDTH), lambda i: (i, OFF_QA // ATT_WIDTH)),
                  pl.BlockSpec((MOBA_BLOCK, ATT_WIDTH), lambda i: (i, OFF_KA // ATT_WIDTH)),
                  pl.BlockSpec((MOBA_BLOCK, ATT_WIDTH), lambda i: (i, OFF_VA // ATT_WIDTH)),
                  tbl, tbl, tbl],
        out_specs=[pl.BlockSpec((ATT_HEADS, 1, ATT_KDIM, MOBA_BLOCK), lambda i: (0, i, 0, 0)),
                   pl.BlockSpec((ATT_HEADS, 1, MOBA_BLOCK, ATT_KDIM), lambda i: (0, i, 0, 0)),
                   pl.BlockSpec((ATT_HEADS, 1, ATT_VROWS, MOBA_BLOCK), lambda i: (0, i, 0, 0))],
        out_shape=[jax.ShapeDtypeStruct(hshape + (ATT_KDIM, MOBA_BLOCK), BF16),
                   jax.ShapeDtypeStruct(hshape + (MOBA_BLOCK, ATT_KDIM), BF16),
                   jax.ShapeDtypeStruct(hshape + (ATT_VROWS, MOBA_BLOCK), BF16)],
        scratch_shapes=[pltpu.VMEM((nblk, ATT_WIDTH), F32)],
        compiler_params=_params(1),
        name="moba_prep",
    )(proj, proj, proj, c, s1, s2)


def _moba_attn_kernel(qt_ref, k_ref, vt_ref, o_ref, s_ref, p_ref, a_ref, acc_ref):
    heads = qt_ref.shape[0]
    nblk = k_ref.shape[1]
    shape = (MOBA_BLOCK, MOBA_BLOCK)
    causal = lax.broadcasted_iota(jnp.int32, shape, 0) <= lax.broadcasted_iota(jnp.int32, shape, 1)
    for qb in range(qt_ref.shape[1]):
        _moba_attn_block(qt_ref, k_ref, vt_ref, o_ref, s_ref, p_ref, a_ref, acc_ref, qb, causal)


def _moba_attn_block(qt_ref, k_ref, vt_ref, o_ref, s_ref, p_ref, a_ref, acc_ref, qb, causal):
    i = qt_ref.shape[1] * pl.program_id(1) + qb
    heads = qt_ref.shape[0]
    nblk = k_ref.shape[1]
    shape = (MOBA_BLOCK, MOBA_BLOCK)

    ms = []
    for h in range(heads):
        s = _dot(k_ref[h, i, :, :ATT_HEAD_DIM], qt_ref[h, qb, :ATT_HEAD_DIM, :])
        s_ref[0, h] = jnp.where(causal, s, NEG)
        s_ref[1, h] = _dot(k_ref[h, 0], qt_ref[h, qb])
        ms.append(jnp.full((1, MOBA_BLOCK), NEG, F32))
        acc_ref[h] = jnp.zeros(acc_ref.shape[1:], F32)
        for slot in (2, 3):
            p_ref[slot, h] = jnp.zeros(shape, BF16)
            a_ref[slot, h] = jnp.ones((1, MOBA_BLOCK), F32)

    def scores2(pos, slot):
        jc = jnp.minimum(pos - 1, nblk - 2)
        for h in range(heads):
            kk = k_ref[h, pl.ds(jc, 2)].reshape(2 * MOBA_BLOCK, ATT_KDIM)
            s2 = _dot(kk, qt_ref[h, qb])
            s_ref[slot, h] = s2[:MOBA_BLOCK]
            s_ref[slot + 1, h] = s2[MOBA_BLOCK:]

    def pv(pos, slot):
        jc = jnp.where(pos == 0, i, jnp.clip(pos - 1, 0, nblk - 1))
        for h in range(heads):
            acc_ref[h] = a_ref[slot, h] * acc_ref[h] + _dot(vt_ref[h, jc], p_ref[slot, h])

    def softmax(slot, ms):
        out = []
        for h in range(heads):
            s = s_ref[slot, h]
            m_new = jnp.maximum(ms[h], jnp.max(s, axis=0, keepdims=True))
            a_ref[slot, h] = jnp.exp2(ms[h] - m_new)
            p_ref[slot, h] = jnp.exp2(s - m_new).astype(BF16)
            out.append(m_new)
        return out

    def body(jj, ms):
        b = 4 * jj
        pv(b - 2, 2)
        pv(b - 1, 3)
        scores2(b + 2, 2)
        ms = softmax(0, ms)
        ms = softmax(1, ms)
        pv(b, 0)
        pv(b + 1, 1)
        scores2(b + 4, 0)
        ms = softmax(2, ms)
        return softmax(3, ms)

    npos = i + 1
    trips = lax.shift_right_logical(npos + 1, 2)
    ms = lax.fori_loop(0, trips, body, ms)
    b = 4 * trips
    pv(b - 2, 2)
    pv(b - 1, 3)

    @pl.when(npos - b >= 1)
    def _():
        softmax(1, softmax(0, ms))
        pv(b, 0)
        pv(b + 1, 1)

    for h in range(heads):
        acc = acc_ref[h]
        out = acc[:ATT_HEAD_DIM] / acc[ATT_HEAD_DIM:ATT_HEAD_DIM + 1]
        o_ref[qb * MOBA_BLOCK:(qb + 1) * MOBA_BLOCK, h * ATT_HEAD_DIM:(h + 1) * ATT_HEAD_DIM] = out.T.astype(o_ref.dtype)


def _moba_attn(qt, k, vt, heads=2, qblocks=4):
    nblk = qt.shape[1]
    t = nblk * MOBA_BLOCK
    blk2 = (MOBA_BLOCK, MOBA_BLOCK)
    return pl.pallas_call(
        _moba_attn_kernel,
        grid=(ATT_HEADS // heads, nblk // qblocks),
        in_specs=[pl.BlockSpec((heads, qblocks, ATT_KDIM, MOBA_BLOCK), lambda h, i: (h, i, 0, 0)),
                  pl.BlockSpec((heads, nblk, MOBA_BLOCK, ATT_KDIM), lambda h, i: (h, 0, 0, 0)),
                  pl.BlockSpec((heads, nblk, ATT_VROWS, MOBA_BLOCK), lambda h, i: (h, 0, 0, 0))],
        out_specs=pl.BlockSpec((qblocks * MOBA_BLOCK, heads * ATT_HEAD_DIM), lambda h, i: (i, h)),
        out_shape=jax.ShapeDtypeStruct((t, ATT_WIDTH), BF16),
        scratch_shapes=[pltpu.VMEM((ATT_SLOTS, heads) + blk2, F32),
                        pltpu.VMEM((ATT_SLOTS, heads) + blk2, BF16),
                        pltpu.VMEM((ATT_SLOTS, heads, 1, MOBA_BLOCK), F32),
                        pltpu.VMEM((heads, ATT_VROWS, MOBA_BLOCK), F32)],
        compiler_params=_params(2),
        name="moba_attn",
    )(qt, k, vt)


def _retention_kernel(q_ref, k_ref, v_ref, g_ref, c_ref, s1_ref, s2_ref,
                      dec_ref, kdec_ref, qdec_ref, cdec_ref, gain_ref, o_ref, state_ref):
    @pl.when(pl.program_id(1) == 0)
    def _():
        state_ref[...] = jnp.zeros_like(state_ref)

    pairs = kdec_ref.shape[0]
    heads = 2 * pairs
    c, s1, s2 = c_ref[...], s1_ref[...], s2_ref[...]
    lane = lax.broadcasted_iota(jnp.int32, (RET_CHUNK, LANES), 1)

    def vslice(h):
        return slice(h * RET_V_DIM, (h + 1) * RET_V_DIM)

    qm, qdm, kb, kdt = [], [], [], []
    for p in range(pairs):
        psl = slice(p * LANES, (p + 1) * LANES)
        q = _rotate(q_ref[:, psl], c, s1, s2, RET_QK_DIM // 2)
        k = _rotate(k_ref[:, psl], c, s1, s2, RET_QK_DIM // 2) * (RET_QK_DIM ** -0.5)
        qd = q * qdec_ref[p]
        kd = k * kdec_ref[p]
        kb.append(k.astype(BF16))
        for hh in range(2):
            mine = (lane < RET_QK_DIM) if hh == 0 else (lane >= RET_QK_DIM)
            qm.append(jnp.where(mine, q, 0.0).astype(BF16))
            qdm.append(jnp.where(mine, qd, 0.0).astype(BF16))
            kdt.append(jnp.where(mine, kd, 0.0).astype(BF16))
    vb = [v_ref[:, vslice(h)].astype(BF16) for h in range(heads)]
    inner = [_dot_nt(qm[h], kb[h // 2]) for h in range(heads)]
    cross = [_dot(qdm[h], state_ref[h].astype(BF16)) for h in range(heads)]
    for h in range(heads):
        state_ref[h] = state_ref[h] * cdec_ref[h] + _dot_tn(kdt[h], vb[h])
    out = [_dot((inner[h] * dec_ref[h]).astype(BF16), vb[h]) + cross[h] for h in range(heads)]
    for h in range(heads):
        mu = jnp.mean(out[h], axis=-1, keepdims=True)
        d = out[h] - mu
        var = jnp.mean(d * d, axis=-1, keepdims=True)
        rn = d * lax.rsqrt(var + LN_EPS) * gain_ref[:, vslice(h)]
        g = g_ref[:, vslice(h)]
        o_ref[:, vslice(h)] = (g * _sigmoid(g) * rn).astype(o_ref.dtype)


def _retention(proj, rope, tables, gain, pairs=4):
    t = proj.shape[0]
    n = t // RET_CHUNK
    c, s1, s2 = rope
    dec, kdec, qdec, cdec = tables
    qk_w = pairs * LANES
    v_w = 2 * pairs * RET_V_DIM
    tbl = pl.BlockSpec((RET_CHUNK, LANES), lambda hp, ci: (ci, 0))
    return pl.pallas_call(
        _retention_kernel,
        grid=(RET_HEADS // (2 * pairs), n),
        in_specs=[pl.BlockSpec((RET_CHUNK, qk_w), lambda hp, ci: (ci, OFF_QR // qk_w + hp)),
                  pl.BlockSpec((RET_CHUNK, qk_w), lambda hp, ci: (ci, OFF_KR // qk_w + hp)),
                  pl.BlockSpec((RET_CHUNK, v_w), lambda hp, ci: (ci, OFF_VR // v_w + hp)),
                  pl.BlockSpec((RET_CHUNK, v_w), lambda hp, ci: (ci, OFF_GR // v_w + hp)),
                  tbl, tbl, tbl,
                  pl.BlockSpec((2 * pairs, RET_CHUNK, RET_CHUNK), lambda hp, ci: (hp, 0, 0)),
                  pl.BlockSpec((pairs, RET_CHUNK, LANES), lambda hp, ci: (hp, 0, 0)),
                  pl.BlockSpec((pairs, RET_CHUNK, LANES), lambda hp, ci: (hp, 0, 0)),
                  pl.BlockSpec((2 * pairs, 1, RET_V_DIM), lambda hp, ci: (hp, 0, 0)),
                  pl.BlockSpec((1, v_w), lambda hp, ci: (0, hp))],
        out_specs=pl.BlockSpec((RET_CHUNK, v_w), lambda hp, ci: (ci, hp)),
        out_shape=jax.ShapeDtypeStruct((t, RET_V_WIDTH), BF16),
        scratch_shapes=[pltpu.VMEM((2 * pairs, LANES, RET_V_DIM), F32)],
        compiler_params=_params(2),
        name="retention",
    )(proj, proj, proj, proj, c, s1, s2, dec, kdec, qdec, cdec, gain)


def _pool_kernel(cur_ref, prev_ref, w_ref, scale_ref, o_ref, buf_a, buf_b, *, tb):
    i = pl.program_id(0)
    g = pl.program_id(1)
    x = cur_ref[...]
    buf_a[0:POOL_HALO, :] = jnp.where(i > 0, prev_ref[...], 0.0)
    buf_a[POOL_HALO:, :] = x
    rows = tb + POOL_HALO
    pos = (i * tb + 1 + lax.broadcasted_iota(jnp.int32, (tb, 1), 0)).astype(F32)
    for gi, w in enumerate(POOL_WINDOWS):
        @pl.when(g == gi)
        def _(w=w):
            src, dst = buf_a, buf_b
            k, lo = 1, SUBLANES
            while k < w:
                dst[lo:rows, :] = src[lo:rows, :] + src[lo - k:rows - k, :]
                src, dst = dst, src
                k, lo = 2 * k, lo + SUBLANES
            tot = src[POOL_HALO:rows, :]
            d = tot / jnp.minimum(pos, float(w)) - x
            y = _dot(d.astype(BF16), w_ref[0])
            o_ref[...] = (y * scale_ref[...]).astype(o_ref.dtype)


def _pool(proj, w_pool, layer, scale, tb=2048):
    t = proj.shape[0]
    tb = min(tb, t)
    gd = POOL_GROUP_DIM
    assert SUBLANES * (max(POOL_WINDOWS).bit_length() - 1) <= POOL_HALO
    return pl.pallas_call(
        functools.partial(_pool_kernel, tb=tb),
        grid=(t // tb, POOL_GROUPS),
        in_specs=[pl.BlockSpec((tb, gd), lambda i, g: (i, OFF_POOL // gd + g)),
                  pl.BlockSpec((POOL_HALO, gd),
                               lambda i, g: (jnp.maximum(i * (tb // POOL_HALO) - 1, 0), OFF_POOL // gd + g)),
                  pl.BlockSpec((None, 1, gd, gd), lambda i, g: (layer, g, 0, 0)),
                  pl.BlockSpec((1, gd), lambda i, g: (0, g))],
        out_specs=pl.BlockSpec((tb, gd), lambda i, g: (i, g)),
        out_shape=jax.ShapeDtypeStruct((t, POOL_WIDTH), BF16),
        scratch_shapes=[pltpu.VMEM((tb + POOL_HALO, gd), F32), pltpu.VMEM((tb + POOL_HALO, gd), F32)],
        compiler_params=_params(2),
        name="pool",
    )(proj, proj, w_pool, scale)


def _merge_kernel(ya_ref, yr_ref, yp_ref, w_ref, ga_ref, gr_ref, gp_ref, o_ref):
    acc = ga_ref[...] * _dot(ya_ref[...], w_ref[0])
    acc = acc + gr_ref[...] * _dot(yr_ref[...], w_ref[1])
    acc = acc + gp_ref[...] * _dot(yp_ref[...], w_ref[2])
    o_ref[...] = acc.astype(o_ref.dtype)


def _merge(ya, yr, yp, w_branch, layer, gates, tm=1024, tn=1024):
    t = ya.shape[0]
    d = w_branch.shape[3]
    br = pl.BlockSpec((tm, BRANCH_WIDTH), lambda i, j: (i, 0))

    def gate(nb):
        return pl.BlockSpec((tm, tn), lambda i, j: (i, (nb * d) // tn + j))

    return pl.pallas_call(
        _merge_kernel,
        grid=(t // tm, d // tn),
        in_specs=[br, br, br,
                  pl.BlockSpec((None, N_BRANCH, BRANCH_WIDTH, tn), lambda i, j: (layer, 0, 0, j)),
                  gate(0), gate(1), gate(2)],
        out_specs=pl.BlockSpec((tm, tn), lambda i, j: (i, j)),
        out_shape=jax.ShapeDtypeStruct((t, d), BF16),
        compiler_params=_params(2),
        name="merge",
    )(ya, yr, yp, w_branch, gates, gates, gates)


def _outproj_ln_kernel(m_ref, w_ref, x_ref, g_ref, b_ref, o_ref, *, alpha):
    half = m_ref.shape[0] // 2
    for r in range(2):
        sl = slice(r * half, (r + 1) * half)
        y = _dot(m_ref[sl, :], w_ref[...])
        o_ref[sl, :] = _layer_norm(alpha * x_ref[sl, :] + y, g_ref[...], b_ref[...])


def _outproj_ln(merged, w_out, layer, x, g, b, alpha, tm=512):
    t, d = x.shape
    row = pl.BlockSpec((tm, d), lambda i: (i, 0))
    vec = pl.BlockSpec((1, d), lambda i: (0, 0))
    return pl.pallas_call(
        functools.partial(_outproj_ln_kernel, alpha=alpha),
        grid=(t // tm,),
        in_specs=[row, pl.BlockSpec((None, d, d), lambda i: (layer, 0, 0)), row, vec, vec],
        out_specs=row,
        out_shape=jax.ShapeDtypeStruct((t, d), F32),
        compiler_params=_params(1),
        name="outproj_ln",
    )(merged, w_out, x, g, b)


def _router_kernel(x_ref, wh_ref, wl_ref, b_ref, e_ref, p_ref):
    x = x_ref[...]
    xh, xl = _split_bf16(x)
    z = _dot(xh, wh_ref[...]) + (_dot(xh, wl_ref[...]) + _dot(xl, wh_ref[...])) + b_ref[...]
    lane = lax.broadcasted_iota(jnp.int32, z.shape, 1)
    big = jnp.int32(ROUTER_LANES)

    def masked_softmax(mask):
        zm = jnp.where(mask, z, NEG)
        e = jnp.where(mask, jnp.exp(zm - jnp.max(zm, axis=-1, keepdims=True)), 0.0)
        return e / jnp.sum(e, axis=-1, keepdims=True)

    def top1(p, mask):
        pm = jnp.where(mask, p, -1.0)
        top = jnp.max(pm, axis=-1, keepdims=True)
        idx = jnp.min(jnp.where(pm == top, lane, big), axis=-1, keepdims=True)
        return top, idx

    in_groups = lane < N_GROUPS
    p1 = masked_softmax(in_groups)
    g_top, g_idx = top1(p1, in_groups)
    lo = N_GROUPS + g_idx * EXPERTS_PER_GROUP
    in_group = (lane >= lo) & (lane < lo + EXPERTS_PER_GROUP)
    p2 = masked_softmax(in_group)
    e1, i1 = top1(p2, in_group)
    e2, i2 = top1(p2, in_group & (lane != i1))
    denom = e1 + e2
    w1 = g_top * e1 / denom
    w2 = g_top * e2 / denom
    e_ref[...] = jnp.where(lane == 0, i1 - N_GROUPS, jnp.where(lane == 1, i2 - N_GROUPS, 0))
    p_ref[...] = jnp.where(lane == 0, w1, jnp.where(lane == 1, w2, 0.0))


def _router(x, w_hi, w_lo, bias, tm=1024):
    t, d = x.shape
    row = pl.BlockSpec((tm, d), lambda i: (i, 0))
    wsp = pl.BlockSpec((d, ROUTER_LANES), lambda i: (0, 0))
    out = pl.BlockSpec((tm, ROUTER_LANES), lambda i: (i, 0))
    return pl.pallas_call(
        _router_kernel,
        grid=(t // tm,),
        in_specs=[row, wsp, wsp, pl.BlockSpec((1, ROUTER_LANES), lambda i: (0, 0))],
        out_specs=[out, out],
        out_shape=[jax.ShapeDtypeStruct((t, ROUTER_LANES), jnp.int32),
                   jax.ShapeDtypeStruct((t, ROUTER_LANES), F32)],
        compiler_params=_params(1),
        name="router",
    )(x, w_hi, w_lo, bias)


WAIT_CHUNKS = (64, 8, 1)


def _start_rows(n, copy):
    for r in range(MOE_BLOCK):
        @pl.when(r < n)
        def _(r=r):
            copy(r).start()


def _wait_rows(n, span):
    rem = n
    for c in WAIT_CHUNKS:
        cnt = lax.shift_right_logical(rem, c.bit_length() - 1)

        def one(_, carry, c=c):
            span(c).wait()
            return carry

        lax.fori_loop(0, cnt, one, 0)
        rem = rem & (c - 1)


def _experts_kernel(be_ref, base_ref, nvalid_ref, src_ref, dst_ref, nused_ref,
                    x_hbm, wg_ref, wu_ref, wd_ref, y_hbm,
                    xbuf, ybuf, wgb, wub, wdb, sem_in, sem_out):
    b = pl.program_id(0)
    n_used = nused_ref[0]
    slot = b & 1

    def start_gather(bb, sl):
        base = base_ref[bb]
        _start_rows(nvalid_ref[bb], lambda r: pltpu.make_async_copy(
            x_hbm.at[pl.ds(src_ref[base + r], 1), :], xbuf.at[sl, pl.ds(r, 1), :], sem_in.at[sl]))

    def wait_gather(bb, sl):
        _wait_rows(nvalid_ref[bb], lambda c: pltpu.make_async_copy(
            x_hbm.at[pl.ds(0, c), :], xbuf.at[sl, pl.ds(0, c), :], sem_in.at[sl]))

    def start_scatter(bb, sl):
        base = base_ref[bb]
        _start_rows(nvalid_ref[bb], lambda r: pltpu.make_async_copy(
            ybuf.at[sl, pl.ds(r, 1), :], y_hbm.at[pl.ds(dst_ref[base + r], 1), :], sem_out.at[sl]))

    def wait_scatter(bb, sl):
        _wait_rows(nvalid_ref[bb], lambda c: pltpu.make_async_copy(
            ybuf.at[sl, pl.ds(0, c), :], y_hbm.at[pl.ds(0, c), :], sem_out.at[sl]))

    @pl.when(b == 0)
    def _():
        xbuf[...] = jnp.zeros_like(xbuf)
        start_gather(0, 0)

    @pl.when(b < n_used)
    def _():
        @pl.when(b + 1 < n_used)
        def _():
            start_gather(b + 1, 1 - slot)

        new_expert = (b == 0) | (be_ref[b] != be_ref[jnp.maximum(b - 1, 0)])

        @pl.when(new_expert)
        def _():
            wgb[...] = wg_ref[...].astype(BF16)
            wub[...] = wu_ref[...].astype(BF16)
            wdb[...] = wd_ref[...].astype(BF16)

        wait_gather(b, slot)

        @pl.when(b >= 2)
        def _():
            wait_scatter(b - 2, slot)

        x = xbuf[slot].astype(BF16)
        gate = _dot(x, wgb[...])
        up = _dot(x, wub[...])
        hid = (gate * _sigmoid(gate) * up).astype(BF16)
        ybuf[slot] = _dot(hid, wdb[...])
        start_scatter(b, slot)

        @pl.when(b == n_used - 1)
        def _():
            @pl.when(b >= 1)
            def _():
                wait_scatter(b - 1, 1 - slot)

            wait_scatter(b, slot)


def _experts(x, wg, wu, wd, layer, plan):
    t, d = x.shape
    block_expert, base, nvalid, src, dst, n_used = plan
    n_blocks = block_expert.shape[0]

    def wspec(shape):
        return pl.BlockSpec((None, None) + shape, lambda b, be, *_: (layer, be[b], 0, 0))

    grid_spec = pltpu.PrefetchScalarGridSpec(
        num_scalar_prefetch=6,
        grid=(n_blocks,),
        in_specs=[pl.BlockSpec(memory_space=pl.ANY),
                  wspec((d, D_EXPERT)), wspec((d, D_EXPERT)), wspec((D_EXPERT, d))],
        out_specs=pl.BlockSpec(memory_space=pl.ANY),
        scratch_shapes=[pltpu.VMEM((2, MOE_BLOCK, d), F32),
                        pltpu.VMEM((2, MOE_BLOCK, d), F32),
                        pltpu.VMEM((d, D_EXPERT), BF16),
                        pltpu.VMEM((d, D_EXPERT), BF16),
                        pltpu.VMEM((D_EXPERT, d), BF16),
                        pltpu.SemaphoreType.DMA((2,)),
                        pltpu.SemaphoreType.DMA((2,))],
    )
    y = pl.pallas_call(
        _experts_kernel,
        grid_spec=grid_spec,
        out_shape=jax.ShapeDtypeStruct((TOP_K * t, d), F32),
        compiler_params=_params(1),
        name="experts",
    )(block_expert, base, nvalid, src, dst, n_used, x, wg, wu, wd)
    return y.reshape(TOP_K, t, d)


def _combine_ln_kernel(x_ref, y0_ref, y1_ref, p_ref, g_ref, b_ref, o_ref, ob_ref, *, alpha):
    p = p_ref[...]
    y = p[:, 0:1] * y0_ref[...] + p[:, 1:2] * y1_ref[...]
    out = _layer_norm(alpha * x_ref[...] + y, g_ref[...], b_ref[...])
    o_ref[...] = out
    ob_ref[...] = out.astype(BF16)


def _combine_ln(x, ycomb, probs, g, b, alpha, tm=512):
    t, d = x.shape
    row = pl.BlockSpec((tm, d), lambda i: (i, 0))
    vec = pl.BlockSpec((1, d), lambda i: (0, 0))
    return pl.pallas_call(
        functools.partial(_combine_ln_kernel, alpha=alpha),
        grid=(t // tm,),
        in_specs=[row,
                  pl.BlockSpec((None, tm, d), lambda i: (0, i, 0)),
                  pl.BlockSpec((None, tm, d), lambda i: (1, i, 0)),
                  pl.BlockSpec((tm, ROUTER_LANES), lambda i: (i, 0)), vec, vec],
        out_specs=[row, row],
        out_shape=[jax.ShapeDtypeStruct((t, d), F32), jax.ShapeDtypeStruct((t, d), BF16)],
        compiler_params=_params(1),
        name="combine_ln",
    )(x, ycomb, ycomb, probs, g, b)


def _rope_tables(t, rot_dim, theta, head_dim):
    half = rot_dim // 2
    inv_freq = 1.0 / (theta ** (jnp.arange(0, rot_dim, 2, dtype=F32) / rot_dim))
    ang = jnp.arange(t, dtype=F32)[:, None] * inv_freq[None, :]
    cos, sin = jnp.cos(ang), jnp.sin(ang)
    rest = head_dim - rot_dim
    ones, zeros = jnp.ones((t, rest), F32), jnp.zeros((t, rest), F32)
    zh = jnp.zeros((t, half), F32)
    c = jnp.concatenate([cos, cos, ones], axis=1)
    s1 = jnp.concatenate([zh, sin, zeros], axis=1)
    s2 = jnp.concatenate([-sin, zh, zeros], axis=1)
    rep = LANES // head_dim
    return tuple(jnp.tile(a, (1, rep)) for a in (c, s1, s2))


def _retention_tables():
    h, c = RET_HEADS, RET_CHUNK
    log_gamma = jnp.log1p(-jnp.exp2(-5.0 - jnp.arange(h, dtype=F32)))
    pos = jnp.arange(c, dtype=F32)
    diff = pos[:, None] - pos[None, :]
    decay = jnp.where(diff >= 0, jnp.exp(log_gamma[:, None, None] * jnp.maximum(diff, 0.0)), 0.0)
    k_decay = jnp.exp(log_gamma[:, None] * (c - 1.0 - pos)[None, :])
    q_decay = jnp.exp(log_gamma[:, None] * (pos + 1.0)[None, :])
    chunk_decay = jnp.exp(log_gamma * c)

    def pair_lanes(a):
        a = a.reshape(h // 2, 2, c).transpose(0, 2, 1)
        return jnp.repeat(a, RET_QK_DIM, axis=2)

    cdec = jnp.broadcast_to(chunk_decay[:, None, None], (h, 1, RET_V_DIM))
    return decay, pair_lanes(k_decay), pair_lanes(q_decay), cdec


def _dispatch_plan(expert, t):
    a = t * TOP_K
    flat_e = expert.reshape(a)
    order = jnp.argsort(flat_e).astype(jnp.int32)
    counts = jnp.sum(flat_e[None, :] == jnp.arange(N_EXPERTS, dtype=flat_e.dtype)[:, None],
                     axis=1, dtype=jnp.int32)
    start = jnp.cumsum(counts) - counts
    blocks = (counts + MOE_BLOCK - 1) // MOE_BLOCK
    blk_end = jnp.cumsum(blocks)
    n_blocks = -(-a // MOE_BLOCK) + N_EXPERTS
    b = jnp.arange(n_blocks, dtype=jnp.int32)
    block_expert = jnp.minimum(jnp.sum(blk_end[None, :] <= b[:, None], axis=1, dtype=jnp.int32),
                               N_EXPERTS - 1)
    first = (b - (blk_end - blocks)[block_expert]) * MOE_BLOCK
    base = start[block_expert] + first
    nvalid = jnp.clip(counts[block_expert] - first, 0, MOE_BLOCK)
    nvalid = jnp.where(b < blk_end[-1], nvalid, 0)
    n_used = blk_end[-1:].astype(jnp.int32)
    order = jnp.pad(order, (0, MOE_BLOCK))
    src = order // TOP_K
    dst = (order % TOP_K) * t + src
    return block_expert, base.astype(jnp.int32), nvalid.astype(jnp.int32), src, dst, n_used


def _token_mixer_ln(h, hb, w_in, ret_gain, w_pool, pool_scale, w_branch, w_out, layer, ln_g, ln_b,
                    rope_a, rope_r, ret_tables, alpha):
    proj = _inproj(hb, w_in, layer, 0, OFF_GATE)
    gates = _inproj(hb, w_in, layer, OFF_GATE, N_IN - OFF_GATE, gate=True)
    qt, kk, vt = _moba_prep(proj, rope_a)
    y_a = _moba_attn(qt, kk, vt)
    y_r = _retention(proj, rope_r, ret_tables, ret_gain[None, :])
    y_p = _pool(proj, w_pool, layer, pool_scale[None, :])
    merged = _merge(y_a, y_r, y_p, w_branch, layer, gates)
    return _outproj_ln(merged, w_out, layer, h, ln_g[None, :], ln_b[None, :], alpha)


def _moe_ln(h, wr_hi, wr_lo, b_router, w_gate, w_up, w_down, layer, ln_g, ln_b, alpha):
    e_pad, p_pad = _router(h, wr_hi, wr_lo, b_router)
    plan = _dispatch_plan(e_pad[:, :TOP_K], h.shape[0])
    ycomb = _experts(h, w_gate, w_up, w_down, layer, plan)
    return _combine_ln(h, ycomb, p_pad, ln_g[None, :], ln_b[None, :], alpha)


def kernel(x, w_in, ret_gain, w_pool, pool_scale, w_branch, w_out, ln1_g, ln1_b, w_r1, b_r1, w_r2, b_r2, w_e_gate, w_e_up, w_e_down, ln2_g, ln2_b):
    bsz, seq, d = x.shape
    depth = w_in.shape[0]
    assert bsz == 1 and d == D_MODEL and seq % 1024 == 0
    t = seq
    alpha = float((2 * depth) ** 0.25)

    rope_a = _rope_tables(t, ROPE_DIM, ROPE_THETA, ATT_HEAD_DIM)
    rope_r = _rope_tables(t, RET_QK_DIM, RET_ROPE_THETA, RET_QK_DIM)
    ret_tables = _retention_tables()

    w_router = jnp.concatenate([w_r1, w_r2], axis=2)
    w_router = jnp.pad(w_router, ((0, 0), (0, 0), (0, ROUTER_LANES - w_router.shape[2])))
    wr_hi = w_router.astype(BF16)
    wr_lo = (w_router - wr_hi.astype(F32)).astype(BF16)
    b_router = jnp.pad(jnp.concatenate([b_r1, b_r2], axis=1),
                       ((0, 0), (0, ROUTER_LANES - N_GROUPS - N_EXPERTS)))[:, None, :]

    w_pool_b, w_branch_b, w_out_b = w_pool.astype(BF16), w_branch.astype(BF16), w_out.astype(BF16)
    h = x.reshape(t, d)
    hb = h.astype(BF16)
    for l in range(depth):
        h = _token_mixer_ln(h, hb, w_in, ret_gain[l], w_pool_b, pool_scale[l], w_branch_b, w_out_b, l,
                            ln1_g[l], ln1_b[l], rope_a, rope_r, ret_tables, alpha)
        h, hb = _moe_ln(h, wr_hi[l], wr_lo[l], b_router[l], w_e_gate, w_e_up, w_e_down, l,
                        ln2_g[l], ln2_b[l], alpha)
    return h.reshape(bsz, seq, d)
```

```python
import functools

import jax
import jax.numpy as jnp
from jax import lax
from jax.experimental import pallas as pl
from jax.experimental.pallas import tpu as pltpu

D_MODEL = 2048

ATT_HEADS = 8
ATT_HEAD_DIM = 128
ATT_WIDTH = ATT_HEADS * ATT_HEAD_DIM
ROPE_DIM = ATT_HEAD_DIM // 4
ROPE_THETA = 500000.0
MOBA_BLOCK = 256
MOBA_TOPK = 3

RET_HEADS = 8
RET_QK_DIM = 64
RET_V_DIM = 128
RET_QK_WIDTH = RET_HEADS * RET_QK_DIM
RET_V_WIDTH = RET_HEADS * RET_V_DIM
RET_CHUNK = 256
RET_ROPE_THETA = 10000.0

POOL_WINDOWS = (2, 4, 8, 16)
POOL_GROUPS = 4
POOL_GROUP_DIM = 256
POOL_WIDTH = POOL_GROUPS * POOL_GROUP_DIM
POOL_HALO = 32

N_BRANCH = 3
BRANCH_WIDTH = 1024

OFF_QA = 0
OFF_KA = OFF_QA + ATT_WIDTH
OFF_VA = OFF_KA + ATT_WIDTH
OFF_QR = OFF_VA + ATT_WIDTH
OFF_KR = OFF_QR + RET_QK_WIDTH
OFF_VR = OFF_KR + RET_QK_WIDTH
OFF_GR = OFF_VR + RET_V_WIDTH
OFF_POOL = OFF_GR + RET_V_WIDTH
OFF_GATE = OFF_POOL + POOL_WIDTH
N_IN = OFF_GATE + N_BRANCH * D_MODEL

N_GROUPS = 4
EXPERTS_PER_GROUP = 8
N_EXPERTS = N_GROUPS * EXPERTS_PER_GROUP
TOP_K = 2
D_EXPERT = 512
MOE_BLOCK = 256
ROUTER_LANES = 128

LN_EPS = 1e-5
NEG = -1e30
TAKEN = -3e38
LOG2_E = 1.4426950408889634

LANES = 128
SUBLANES = 8
VMEM_LIMIT = 56 * 1024 * 1024

F32 = jnp.float32
BF16 = jnp.bfloat16


def _params(n_axes):
    return pltpu.CompilerParams(dimension_semantics=("arbitrary",) * n_axes,
                                vmem_limit_bytes=VMEM_LIMIT)


def _dot(a, b):
    return jnp.dot(a, b, preferred_element_type=F32)


def _dot_nt(a, b):
    return lax.dot_general(a, b, (((1,), (1,)), ((), ())), preferred_element_type=F32)


def _dot_tn(a, b):
    return lax.dot_general(a, b, (((0,), (0,)), ((), ())), preferred_element_type=F32)


def _split_bf16(a):
    hi = a.astype(BF16)
    lo = (a - hi.astype(F32)).astype(BF16)
    return hi, lo


def _dot_nt_f32ish(a, b):
    ah, al = _split_bf16(a)
    bh, bl = _split_bf16(b)
    return _dot_nt(ah, bh) + (_dot_nt(ah, bl) + _dot_nt(al, bh))


def _rotate(x, c, s1, s2, shift):
    return x * c + pltpu.roll(x, shift, 1) * s1 + pltpu.roll(x, LANES - shift, 1) * s2


def _sigmoid(x):
    return 0.5 * jnp.tanh(0.5 * x) + 0.5


def _layer_norm(v, g, b):
    mu = jnp.mean(v, axis=-1, keepdims=True)
    d = v - mu
    var = jnp.mean(d * d, axis=-1, keepdims=True)
    return d * lax.rsqrt(var + LN_EPS) * g + b


def _inproj_kernel(x_ref, w_ref, o_ref, wb_ref, *, gate):
    @pl.when(pl.program_id(1) == 0)
    def _():
        wb_ref[...] = w_ref[...].astype(BF16)

    y = _dot(x_ref[...], wb_ref[...])
    o_ref[...] = _sigmoid(y).astype(o_ref.dtype) if gate else y


def _inproj(xb, w, layer, col0, ncols, gate=False, tm=1024, tn=1024):
    t, d = xb.shape
    return pl.pallas_call(
        functools.partial(_inproj_kernel, gate=gate),
        grid=(ncols // tn, t // tm),
        in_specs=[pl.BlockSpec((tm, d), lambda j, i: (i, 0)),
                  pl.BlockSpec((None, d, tn), lambda j, i: (layer, 0, col0 // tn + j))],
        out_specs=pl.BlockSpec((tm, tn), lambda j, i: (i, j)),
        out_shape=jax.ShapeDtypeStruct((t, ncols), BF16 if gate else F32),
        scratch_shapes=[pltpu.VMEM((d, tn), BF16)],
        compiler_params=_params(2),
        name="inproj_gates" if gate else "inproj",
    )(xb, w)


ATT_KDIM = 2 * ATT_HEAD_DIM
ATT_VROWS = ATT_HEAD_DIM + 16
ATT_SLOTS = 4


def _moba_prep_kernel(q_ref, k_ref, v_ref, c_ref, s1_ref, s2_ref, qt_ref, ko_ref, vt_ref, km_ref):
    i = pl.program_id(0)
    nblk = km_ref.shape[0]

    @pl.when(i == 0)
    def _():
        km_ref[...] = jnp.zeros_like(km_ref)

    c, s1, s2 = c_ref[...], s1_ref[...], s2_ref[...]
    blk = lax.broadcasted_iota(jnp.int32, (nblk, MOBA_BLOCK), 0)
    blkf = blk.astype(F32)
    scale = ATT_HEAD_DIM ** -0.5 * LOG2_E
    lane = lax.broadcasted_iota(jnp.int32, (MOBA_BLOCK, ATT_KDIM - ATT_HEAD_DIM), 1)
    this_block = jnp.where(lane == i, 1.0, 0.0).astype(BF16)
    row = lax.broadcasted_iota(jnp.int32, (ATT_VROWS - ATT_HEAD_DIM, MOBA_BLOCK), 0)
    ones_row = jnp.where(row == 0, 1.0, 0.0).astype(BF16)
    pad_rows = ATT_KDIM - ATT_HEAD_DIM - nblk
    eye = jnp.where(lax.broadcasted_iota(jnp.int32, (ATT_HEAD_DIM, ATT_HEAD_DIM), 0)
                    == lax.broadcasted_iota(jnp.int32, (ATT_HEAD_DIM, ATT_HEAD_DIM), 1), 1.0, 0.0).astype(BF16)
    for h in range(ATT_HEADS):
        sl = slice(h * ATT_HEAD_DIM, (h + 1) * ATT_HEAD_DIM)
        qr = _rotate(q_ref[:, sl], c, s1, s2, ROPE_DIM // 2)
        kr = _rotate(k_ref[:, sl], c, s1, s2, ROPE_DIM // 2)
        ko_ref[h, 0, :, :ATT_HEAD_DIM] = kr.astype(BF16)
        ko_ref[h, 0, :, ATT_HEAD_DIM:] = this_block
        km_blk = lax.broadcasted_iota(jnp.int32, (nblk, ATT_HEAD_DIM), 0)
        km_ref[:, sl] = jnp.where(km_blk == i, jnp.mean(kr, axis=0, keepdims=True), km_ref[:, sl])
        qt_ref[h, 0, :ATT_HEAD_DIM, :] = _dot_nt(eye, (qr * scale).astype(BF16)).astype(BF16)
        vt_ref[h, 0, :ATT_HEAD_DIM, :] = _dot_nt(eye, v_ref[:, sl].astype(BF16)).astype(BF16)
        vt_ref[h, 0, ATT_HEAD_DIM:, :] = ones_row
        gate = _dot_nt_f32ish(km_ref[:, sl], qr)
        gate = jnp.where(blk < i, gate, NEG)
        for _ in range(MOBA_TOPK):
            top = jnp.max(gate, axis=0, keepdims=True)
            first = jnp.min(jnp.where(gate == top, blkf, float(nblk)), axis=0, keepdims=True)
            gate = jnp.where(blkf == first, TAKEN, gate)
        bias = jnp.where(gate == TAKEN, jnp.where(blk < i, 0.0, NEG), NEG)
        qt_ref[h, 0, ATT_HEAD_DIM:ATT_HEAD_DIM + nblk, :] = bias.astype(BF16)
        if pad_rows:
            qt_ref[h, 0, ATT_HEAD_DIM + nblk:, :] = jnp.zeros((pad_rows, MOBA_BLOCK), BF16)


def _moba_prep(proj, rope):
    t = proj.shape[0]
    nblk = t // MOBA_BLOCK
    assert ATT_HEAD_DIM + nblk <= ATT_KDIM and nblk % 16 == 0
    c, s1, s2 = rope
    tbl = pl.BlockSpec((MOBA_BLOCK, LANES), lambda i: (i, 0))
    hshape = (ATT_HEADS, nblk)
    return pl.pallas_call(
        _moba_prep_kernel,
        grid=(nblk,),
        in_specs=[pl.BlockSpec((MOBA_BLOCK, ATT_WIDTH), lambda i: (i, OFF_QA // ATT_WIDTH)),
                  pl.BlockSpec((MOBA_BLOCK, ATT_WIDTH), lambda i: (i, OFF_KA // ATT_WIDTH)),
                  pl.BlockSpec((MOBA_BLOCK, ATT_WIDTH), lambda i: (i, OFF_VA // ATT_WIDTH)),
                  tbl, tbl, tbl],
        out_specs=[pl.BlockSpec((ATT_HEADS, 1, ATT_KDIM, MOBA_BLOCK), lambda i: (0, i, 0, 0)),
                   pl.BlockSpec((ATT_HEADS, 1, MOBA_BLOCK, ATT_KDIM), lambda i: (0, i, 0, 0)),
                   pl.BlockSpec((ATT_HEADS, 1, ATT_VROWS, MOBA_BLOCK), lambda i: (0, i, 0, 0))],
        out_shape=[jax.ShapeDtypeStruct(hshape + (ATT_KDIM, MOBA_BLOCK), BF16),
                   jax.ShapeDtypeStruct(hshape + (MOBA_BLOCK, ATT_KDIM), BF16),
                   jax.ShapeDtypeStruct(hshape + (ATT_VROWS, MOBA_BLOCK), BF16)],
        scratch_shapes=[pltpu.VMEM((nblk, ATT_WIDTH), F32)],
        compiler_params=_params(1),
        name="moba_prep",
    )(proj, proj, proj, c, s1, s2)


def _moba_attn_kernel(qt_ref, k_ref, vt_ref, o_ref, s_ref, p_ref, a_ref, acc_ref):
    heads = qt_ref.shape[0]
    nblk = k_ref.shape[1]
    shape = (MOBA_BLOCK, MOBA_BLOCK)
    causal = lax.broadcasted_iota(jnp.int32, shape, 0) <= lax.broadcasted_iota(jnp.int32, shape, 1)
    for qb in range(qt_ref.shape[1]):
        _moba_attn_block(qt_ref, k_ref, vt_ref, o_ref, s_ref, p_ref, a_ref, acc_ref, qb, causal)


def _moba_attn_block(qt_ref, k_ref, vt_ref, o_ref, s_ref, p_ref, a_ref, acc_ref, qb, causal):
    i = qt_ref.shape[1] * pl.program_id(1) + qb
    heads = qt_ref.shape[0]
    nblk = k_ref.shape[1]
    shape = (MOBA_BLOCK, MOBA_BLOCK)

    ms = []
    for h in range(heads):
        s = _dot(k_ref[h, i, :, :ATT_HEAD_DIM], qt_ref[h, qb, :ATT_HEAD_DIM, :])
        s_ref[0, h] = jnp.where(causal, s, NEG)
        s_ref[1, h] = _dot(k_ref[h, 0], qt_ref[h, qb])
        ms.append(jnp.full((1, MOBA_BLOCK), NEG, F32))
        acc_ref[h] = jnp.zeros(acc_ref.shape[1:], F32)
        for slot in (2, 3):
            p_ref[slot, h] = jnp.zeros(shape, BF16)
            a_ref[slot, h] = jnp.ones((1, MOBA_BLOCK), F32)

    def scores2(pos, slot):
        jc = jnp.minimum(pos - 1, nblk - 2)
        for h in range(heads):
            kk = k_ref[h, pl.ds(jc, 2)].reshape(2 * MOBA_BLOCK, ATT_KDIM)
            s2 = _dot(kk, qt_ref[h, qb])
            s_ref[slot, h] = s2[:MOBA_BLOCK]
            s_ref[slot + 1, h] = s2[MOBA_BLOCK:]

    def pv(pos, slot):
        jc = jnp.where(pos == 0, i, jnp.clip(pos - 1, 0, nblk - 1))
        for h in range(heads):
            acc_ref[h] = a_ref[slot, h] * acc_ref[h] + _dot(vt_ref[h, jc], p_ref[slot, h])

    def softmax(slot, ms):
        out = []
        for h in range(heads):
            s = s_ref[slot, h]
            m_new = jnp.maximum(ms[h], jnp.max(s, axis=0, keepdims=True))
            a_ref[slot, h] = jnp.exp2(ms[h] - m_new)
            p_ref[slot, h] = jnp.exp2(s - m_new).astype(BF16)
            out.append(m_new)
        return out

    def body(jj, ms):
        b = 4 * jj
        pv(b - 2, 2)
        pv(b - 1, 3)
        scores2(b + 2, 2)
        ms = softmax(0, ms)
        ms = softmax(1, ms)
        pv(b, 0)
        pv(b + 1, 1)
        scores2(b + 4, 0)
        ms = softmax(2, ms)
        return softmax(3, ms)

    npos = i + 1
    trips = lax.shift_right_logical(npos + 1, 2)
    ms = lax.fori_loop(0, trips, body, ms)
    b = 4 * trips
    pv(b - 2, 2)
    pv(b - 1, 3)

    @pl.when(npos - b >= 1)
    def _():
        softmax(1, softmax(0, ms))
        pv(b, 0)
        pv(b + 1, 1)

    for h in range(heads):
        acc = acc_ref[h]
        out = acc[:ATT_HEAD_DIM] / acc[ATT_HEAD_DIM:ATT_HEAD_DIM + 1]
        o_ref[qb * MOBA_BLOCK:(qb + 1) * MOBA_BLOCK, h * ATT_HEAD_DIM:(h + 1) * ATT_HEAD_DIM] = out.T.astype(o_ref.dtype)


def _moba_attn(qt, k, vt, heads=2, qblocks=8):
    nblk = qt.shape[1]
    t = nblk * MOBA_BLOCK
    blk2 = (MOBA_BLOCK, MOBA_BLOCK)
    return pl.pallas_call(
        _moba_attn_kernel,
        grid=(ATT_HEADS // heads, nblk // qblocks),
        in_specs=[pl.BlockSpec((heads, qblocks, ATT_KDIM, MOBA_BLOCK), lambda h, i: (h, i, 0, 0)),
                  pl.BlockSpec((heads, nblk, MOBA_BLOCK, ATT_KDIM), lambda h, i: (h, 0, 0, 0)),
                  pl.BlockSpec((heads, nblk, ATT_VROWS, MOBA_BLOCK), lambda h, i: (h, 0, 0, 0))],
        out_specs=pl.BlockSpec((qblocks * MOBA_BLOCK, heads * ATT_HEAD_DIM), lambda h, i: (i, h)),
        out_shape=jax.ShapeDtypeStruct((t, ATT_WIDTH), BF16),
        scratch_shapes=[pltpu.VMEM((ATT_SLOTS, heads) + blk2, F32),
                        pltpu.VMEM((ATT_SLOTS, heads) + blk2, BF16),
                        pltpu.VMEM((ATT_SLOTS, heads, 1, MOBA_BLOCK), F32),
                        pltpu.VMEM((heads, ATT_VROWS, MOBA_BLOCK), F32)],
        compiler_params=_params(2),
        name="moba_attn",
    )(qt, k, vt)


def _retention_kernel(q_ref, k_ref, v_ref, g_ref, c_ref, s1_ref, s2_ref,
                      dec_ref, kdec_ref, qdec_ref, cdec_ref, gain_ref, o_ref, state_ref):
    @pl.when(pl.program_id(1) == 0)
    def _():
        state_ref[...] = jnp.zeros_like(state_ref)

    pairs = kdec_ref.shape[0]
    heads = 2 * pairs
    c, s1, s2 = c_ref[...], s1_ref[...], s2_ref[...]
    lane = lax.broadcasted_iota(jnp.int32, (RET_CHUNK, LANES), 1)

    def vslice(h):
        return slice(h * RET_V_DIM, (h + 1) * RET_V_DIM)

    qm, qdm, kb, kdt = [], [], [], []
    for p in range(pairs):
        psl = slice(p * LANES, (p + 1) * LANES)
        q = _rotate(q_ref[:, psl], c, s1, s2, RET_QK_DIM // 2)
        k = _rotate(k_ref[:, psl], c, s1, s2, RET_QK_DIM // 2) * (RET_QK_DIM ** -0.5)
        qd = q * qdec_ref[p]
        kd = k * kdec_ref[p]
        kb.append(k.astype(BF16))
        for hh in range(2):
            mine = (lane < RET_QK_DIM) if hh == 0 else (lane >= RET_QK_DIM)
            qm.append(jnp.where(mine, q, 0.0).astype(BF16))
            qdm.append(jnp.where(mine, qd, 0.0).astype(BF16))
            kdt.append(jnp.where(mine, kd, 0.0).astype(BF16))
    vb = [v_ref[:, vslice(h)].astype(BF16) for h in range(heads)]
    inner = [_dot_nt(qm[h], kb[h // 2]) for h in range(heads)]
    cross = [_dot(qdm[h], state_ref[h].astype(BF16)) for h in range(heads)]
    for h in range(heads):
        state_ref[h] = state_ref[h] * cdec_ref[h] + _dot_tn(kdt[h], vb[h])
    out = [_dot((inner[h] * dec_ref[h]).astype(BF16), vb[h]) + cross[h] for h in range(heads)]
    for h in range(heads):
        mu = jnp.mean(out[h], axis=-1, keepdims=True)
        d = out[h] - mu
        var = jnp.mean(d * d, axis=-1, keepdims=True)
        rn = d * lax.rsqrt(var + LN_EPS) * gain_ref[:, vslice(h)]
        g = g_ref[:, vslice(h)]
        o_ref[:, vslice(h)] = (g * _sigmoid(g) * rn).astype(o_ref.dtype)


def _retention(proj, rope, tables, gain, pairs=4):
    t = proj.shape[0]
    n = t // RET_CHUNK
    c, s1, s2 = rope
    dec, kdec, qdec, cdec = tables
    qk_w = pairs * LANES
    v_w = 2 * pairs * RET_V_DIM
    tbl = pl.BlockSpec((RET_CHUNK, LANES), lambda hp, ci: (ci, 0))
    return pl.pallas_call(
        _retention_kernel,
        grid=(RET_HEADS // (2 * pairs), n),
        in_specs=[pl.BlockSpec((RET_CHUNK, qk_w), lambda hp, ci: (ci, OFF_QR // qk_w + hp)),
                  pl.BlockSpec((RET_CHUNK, qk_w), lambda hp, ci: (ci, OFF_KR // qk_w + hp)),
                  pl.BlockSpec((RET_CHUNK, v_w), lambda hp, ci: (ci, OFF_VR // v_w + hp)),
                  pl.BlockSpec((RET_CHUNK, v_w), lambda hp, ci: (ci, OFF_GR // v_w + hp)),
                  tbl, tbl, tbl,
                  pl.BlockSpec((2 * pairs, RET_CHUNK, RET_CHUNK), lambda hp, ci: (hp, 0, 0)),
                  pl.BlockSpec((pairs, RET_CHUNK, LANES), lambda hp, ci: (hp, 0, 0)),
                  pl.BlockSpec((pairs, RET_CHUNK, LANES), lambda hp, ci: (hp, 0, 0)),
                  pl.BlockSpec((2 * pairs, 1, RET_V_DIM), lambda hp, ci: (hp, 0, 0)),
                  pl.BlockSpec((1, v_w), lambda hp, ci: (0, hp))],
        out_specs=pl.BlockSpec((RET_CHUNK, v_w), lambda hp, ci: (ci, hp)),
        out_shape=jax.ShapeDtypeStruct((t, RET_V_WIDTH), BF16),
        scratch_shapes=[pltpu.VMEM((2 * pairs, LANES, RET_V_DIM), F32)],
        compiler_params=_params(2),
        name="retention",
    )(proj, proj, proj, proj, c, s1, s2, dec, kdec, qdec, cdec, gain)


def _pool_kernel(cur_ref, prev_ref, w_ref, scale_ref, o_ref, buf_a, buf_b, *, tb):
    i = pl.program_id(0)
    g = pl.program_id(1)
    x = cur_ref[...]
    buf_a[0:POOL_HALO, :] = jnp.where(i > 0, prev_ref[...], 0.0)
    buf_a[POOL_HALO:, :] = x
    rows = tb + POOL_HALO
    pos = (i * tb + 1 + lax.broadcasted_iota(jnp.int32, (tb, 1), 0)).astype(F32)
    for gi, w in enumerate(POOL_WINDOWS):
        @pl.when(g == gi)
        def _(w=w):
            src, dst = buf_a, buf_b
            k, lo = 1, SUBLANES
            while k < w:
                dst[lo:rows, :] = src[lo:rows, :] + src[lo - k:rows - k, :]
                src, dst = dst, src
                k, lo = 2 * k, lo + SUBLANES
            tot = src[POOL_HALO:rows, :]
            d = tot / jnp.minimum(pos, float(w)) - x
            y = _dot(d.astype(BF16), w_ref[0])
            o_ref[...] = (y * scale_ref[...]).astype(o_ref.dtype)


def _pool(proj, w_pool, layer, scale, tb=4096):
    t = proj.shape[0]
    tb = min(tb, t)
    gd = POOL_GROUP_DIM
    assert SUBLANES * (max(POOL_WINDOWS).bit_length() - 1) <= POOL_HALO
    return pl.pallas_call(
        functools.partial(_pool_kernel, tb=tb),
        grid=(t // tb, POOL_GROUPS),
        in_specs=[pl.BlockSpec((tb, gd), lambda i, g: (i, OFF_POOL // gd + g)),
                  pl.BlockSpec((POOL_HALO, gd),
                               lambda i, g: (jnp.maximum(i * (tb // POOL_HALO) - 1, 0), OFF_POOL // gd + g)),
                  pl.BlockSpec((None, 1, gd, gd), lambda i, g: (layer, g, 0, 0)),
                  pl.BlockSpec((1, gd), lambda i, g: (0, g))],
        out_specs=pl.BlockSpec((tb, gd), lambda i, g: (i, g)),
        out_shape=jax.ShapeDtypeStruct((t, POOL_WIDTH), BF16),
        scratch_shapes=[pltpu.VMEM((tb + POOL_HALO, gd), F32), pltpu.VMEM((tb + POOL_HALO, gd), F32)],
        compiler_params=_params(2),
        name="pool",
    )(proj, proj, w_pool, scale)


def _merge_kernel(ya_ref, yr_ref, yp_ref, w_ref, ga_ref, gr_ref, gp_ref, o_ref):
    acc = ga_ref[...] * _dot(ya_ref[...], w_ref[0])
    acc = acc + gr_ref[...] * _dot(yr_ref[...], w_ref[1])
    acc = acc + gp_ref[...] * _dot(yp_ref[...], w_ref[2])
    o_ref[...] = acc.astype(o_ref.dtype)


def _merge(ya, yr, yp, w_branch, layer, gates, tm=1024, tn=1024):
    t = ya.shape[0]
    d = w_branch.shape[3]
    br = pl.BlockSpec((tm, BRANCH_WIDTH), lambda i, j: (i, 0))

    def gate(nb):
        return pl.BlockSpec((tm, tn), lambda i, j: (i, (nb * d) // tn + j))

    return pl.pallas_call(
        _merge_kernel,
        grid=(t // tm, d // tn),
        in_specs=[br, br, br,
                  pl.BlockSpec((None, N_BRANCH, BRANCH_WIDTH, tn), lambda i, j: (layer, 0, 0, j)),
                  gate(0), gate(1), gate(2)],
        out_specs=pl.BlockSpec((tm, tn), lambda i, j: (i, j)),
        out_shape=jax.ShapeDtypeStruct((t, d), BF16),
        compiler_params=_params(2),
        name="merge",
    )(ya, yr, yp, w_branch, gates, gates, gates)


def _outproj_ln_kernel(m_ref, w_ref, x_ref, g_ref, b_ref, o_ref, *, alpha):
    half = m_ref.shape[0] // 2
    for r in range(2):
        sl = slice(r * half, (r + 1) * half)
        y = _dot(m_ref[sl, :], w_ref[...])
        o_ref[sl, :] = _layer_norm(alpha * x_ref[sl, :] + y, g_ref[...], b_ref[...])


def _outproj_ln(merged, w_out, layer, x, g, b, alpha, tm=512):
    t, d = x.shape
    row = pl.BlockSpec((tm, d), lambda i: (i, 0))
    vec = pl.BlockSpec((1, d), lambda i: (0, 0))
    return pl.pallas_call(
        functools.partial(_outproj_ln_kernel, alpha=alpha),
        grid=(t // tm,),
        in_specs=[row, pl.BlockSpec((None, d, d), lambda i: (layer, 0, 0)), row, vec, vec],
        out_specs=row,
        out_shape=jax.ShapeDtypeStruct((t, d), F32),
        compiler_params=_params(1),
        name="outproj_ln",
    )(merged, w_out, x, g, b)


def _router_kernel(x_ref, wh_ref, wl_ref, b_ref, e_ref, p_ref):
    x = x_ref[...]
    xh, xl = _split_bf16(x)
    z = _dot(xh, wh_ref[...]) + (_dot(xh, wl_ref[...]) + _dot(xl, wh_ref[...])) + b_ref[...]
    lane = lax.broadcasted_iota(jnp.int32, z.shape, 1)
    big = jnp.int32(ROUTER_LANES)

    def masked_softmax(mask):
        zm = jnp.where(mask, z, NEG)
        e = jnp.where(mask, jnp.exp(zm - jnp.max(zm, axis=-1, keepdims=True)), 0.0)
        return e / jnp.sum(e, axis=-1, keepdims=True)

    def top1(p, mask):
        pm = jnp.where(mask, p, -1.0)
        top = jnp.max(pm, axis=-1, keepdims=True)
        idx = jnp.min(jnp.where(pm == top, lane, big), axis=-1, keepdims=True)
        return top, idx

    in_groups = lane < N_GROUPS
    p1 = masked_softmax(in_groups)
    g_top, g_idx = top1(p1, in_groups)
    lo = N_GROUPS + g_idx * EXPERTS_PER_GROUP
    in_group = (lane >= lo) & (lane < lo + EXPERTS_PER_GROUP)
    p2 = masked_softmax(in_group)
    e1, i1 = top1(p2, in_group)
    e2, i2 = top1(p2, in_group & (lane != i1))
    denom = e1 + e2
    w1 = g_top * e1 / denom
    w2 = g_top * e2 / denom
    e_ref[...] = jnp.where(lane == 0, i1 - N_GROUPS, jnp.where(lane == 1, i2 - N_GROUPS, 0))
    p_ref[...] = jnp.where(lane == 0, w1, jnp.where(lane == 1, w2, 0.0))


def _router(x, w_hi, w_lo, bias, tm=1024):
    t, d = x.shape
    row = pl.BlockSpec((tm, d), lambda i: (i, 0))
    wsp = pl.BlockSpec((d, ROUTER_LANES), lambda i: (0, 0))
    out = pl.BlockSpec((tm, ROUTER_LANES), lambda i: (i, 0))
    return pl.pallas_call(
        _router_kernel,
        grid=(t // tm,),
        in_specs=[row, wsp, wsp, pl.BlockSpec((1, ROUTER_LANES), lambda i: (0, 0))],
        out_specs=[out, out],
        out_shape=[jax.ShapeDtypeStruct((t, ROUTER_LANES), jnp.int32),
                   jax.ShapeDtypeStruct((t, ROUTER_LANES), F32)],
        compiler_params=_params(1),
        name="router",
    )(x, w_hi, w_lo, bias)


WAIT_CHUNKS = (64, 8, 1)


def _start_rows(n, copy):
    for r in range(MOE_BLOCK):
        @pl.when(r < n)
        def _(r=r):
            copy(r).start()


def _wait_rows(n, span):
    rem = n
    for c in WAIT_CHUNKS:
        cnt = lax.shift_right_logical(rem, c.bit_length() - 1)

        def one(_, carry, c=c):
            span(c).wait()
            return carry

        lax.fori_loop(0, cnt, one, 0)
        rem = rem & (c - 1)


def _experts_kernel(be_ref, base_ref, nvalid_ref, src_ref, dst_ref, nused_ref,
                    x_hbm, wg_ref, wu_ref, wd_ref, y_hbm,
                    xbuf, ybuf, wgb, wub, wdb, sem_in, sem_out):
    b = pl.program_id(0)
    n_used = nused_ref[0]
    slot = b & 1

    def start_gather(bb, sl):
        base = base_ref[bb]
        _start_rows(nvalid_ref[bb], lambda r: pltpu.make_async_copy(
            x_hbm.at[pl.ds(src_ref[base + r], 1), :], xbuf.at[sl, pl.ds(r, 1), :], sem_in.at[sl]))

    def wait_gather(bb, sl):
        _wait_rows(nvalid_ref[bb], lambda c: pltpu.make_async_copy(
            x_hbm.at[pl.ds(0, c), :], xbuf.at[sl, pl.ds(0, c), :], sem_in.at[sl]))

    def start_scatter(bb, sl):
        base = base_ref[bb]
        _start_rows(nvalid_ref[bb], lambda r: pltpu.make_async_copy(
            ybuf.at[sl, pl.ds(r, 1), :], y_hbm.at[pl.ds(dst_ref[base + r], 1), :], sem_out.at[sl]))

    def wait_scatter(bb, sl):
        _wait_rows(nvalid_ref[bb], lambda c: pltpu.make_async_copy(
            ybuf.at[sl, pl.ds(0, c), :], y_hbm.at[pl.ds(0, c), :], sem_out.at[sl]))

    @pl.when(b == 0)
    def _():
        xbuf[...] = jnp.zeros_like(xbuf)
        start_gather(0, 0)

    @pl.when(b < n_used)
    def _():
        @pl.when(b + 1 < n_used)
        def _():
            start_gather(b + 1, 1 - slot)

        new_expert = (b == 0) | (be_ref[b] != be_ref[jnp.maximum(b - 1, 0)])

        @pl.when(new_expert)
        def _():
            wgb[...] = wg_ref[...].astype(BF16)
            wub[...] = wu_ref[...].astype(BF16)
            wdb[...] = wd_ref[...].astype(BF16)

        wait_gather(b, slot)

        @pl.when(b >= 2)
        def _():
            wait_scatter(b - 2, slot)

        x = xbuf[slot].astype(BF16)
        gate = _dot(x, wgb[...])
        up = _dot(x, wub[...])
        hid = (gate * _sigmoid(gate) * up).astype(BF16)
        ybuf[slot] = _dot(hid, wdb[...])
        start_scatter(b, slot)

        @pl.when(b == n_used - 1)
        def _():
            @pl.when(b >= 1)
            def _():
                wait_scatter(b - 1, 1 - slot)

            wait_scatter(b, slot)


def _experts(x, wg, wu, wd, layer, plan):
    t, d = x.shape
    block_expert, base, nvalid, src, dst, n_used = plan
    n_blocks = block_expert.shape[0]

    def wspec(shape):
        return pl.BlockSpec((None, None) + shape, lambda b, be, *_: (layer, be[b], 0, 0))

    grid_spec = pltpu.PrefetchScalarGridSpec(
        num_scalar_prefetch=6,
        grid=(n_blocks,),
        in_specs=[pl.BlockSpec(memory_space=pl.ANY),
                  wspec((d, D_EXPERT)), wspec((d, D_EXPERT)), wspec((D_EXPERT, d))],
        out_specs=pl.BlockSpec(memory_space=pl.ANY),
        scratch_shapes=[pltpu.VMEM((2, MOE_BLOCK, d), F32),
                        pltpu.VMEM((2, MOE_BLOCK, d), F32),
                        pltpu.VMEM((d, D_EXPERT), BF16),
                        pltpu.VMEM((d, D_EXPERT), BF16),
                        pltpu.VMEM((D_EXPERT, d), BF16),
                        pltpu.SemaphoreType.DMA((2,)),
                        pltpu.SemaphoreType.DMA((2,))],
    )
    y = pl.pallas_call(
        _experts_kernel,
        grid_spec=grid_spec,
        out_shape=jax.ShapeDtypeStruct((TOP_K * t, d), F32),
        compiler_params=_params(1),
        name="experts",
    )(block_expert, base, nvalid, src, dst, n_used, x, wg, wu, wd)
    return y.reshape(TOP_K, t, d)


def _combine_ln_kernel(x_ref, y0_ref, y1_ref, p_ref, g_ref, b_ref, o_ref, ob_ref, *, alpha):
    p = p_ref[...]
    y = p[:, 0:1] * y0_ref[...] + p[:, 1:2] * y1_ref[...]
    out = _layer_norm(alpha * x_ref[...] + y, g_ref[...], b_ref[...])
    o_ref[...] = out
    ob_ref[...] = out.astype(BF16)


def _combine_ln(x, ycomb, probs, g, b, alpha, tm=512):
    t, d = x.shape
    row = pl.BlockSpec((tm, d), lambda i: (i, 0))
    vec = pl.BlockSpec((1, d), lambda i: (0, 0))
    return pl.pallas_call(
        functools.partial(_combine_ln_kernel, alpha=alpha),
        grid=(t // tm,),
        in_specs=[row,
                  pl.BlockSpec((None, tm, d), lambda i: (0, i, 0)),
                  pl.BlockSpec((None, tm, d), lambda i: (1, i, 0)),
                  pl.BlockSpec((tm, ROUTER_LANES), lambda i: (i, 0)), vec, vec],
        out_specs=[row, row],
        out_shape=[jax.ShapeDtypeStruct((t, d), F32), jax.ShapeDtypeStruct((t, d), BF16)],
        compiler_params=_params(1),
        name="combine_ln",
    )(x, ycomb, ycomb, probs, g, b)


def _rope_tables(t, rot_dim, theta, head_dim):
    half = rot_dim // 2
    inv_freq = 1.0 / (theta ** (jnp.arange(0, rot_dim, 2, dtype=F32) / rot_dim))
    ang = jnp.arange(t, dtype=F32)[:, None] * inv_freq[None, :]
    cos, sin = jnp.cos(ang), jnp.sin(ang)
    rest = head_dim - rot_dim
    ones, zeros = jnp.ones((t, rest), F32), jnp.zeros((t, rest), F32)
    zh = jnp.zeros((t, half), F32)
    c = jnp.concatenate([cos, cos, ones], axis=1)
    s1 = jnp.concatenate([zh, sin, zeros], axis=1)
    s2 = jnp.concatenate([-sin, zh, zeros], axis=1)
    rep = LANES // head_dim
    return tuple(jnp.tile(a, (1, rep)) for a in (c, s1, s2))


def _retention_tables():
    h, c = RET_HEADS, RET_CHUNK
    log_gamma = jnp.log1p(-jnp.exp2(-5.0 - jnp.arange(h, dtype=F32)))
    pos = jnp.arange(c, dtype=F32)
    diff = pos[:, None] - pos[None, :]
    decay = jnp.where(diff >= 0, jnp.exp(log_gamma[:, None, None] * jnp.maximum(diff, 0.0)), 0.0)
    k_decay = jnp.exp(log_gamma[:, None] * (c - 1.0 - pos)[None, :])
    q_decay = jnp.exp(log_gamma[:, None] * (pos + 1.0)[None, :])
    chunk_decay = jnp.exp(log_gamma * c)

    def pair_lanes(a):
        a = a.reshape(h // 2, 2, c).transpose(0, 2, 1)
        return jnp.repeat(a, RET_QK_DIM, axis=2)

    cdec = jnp.broadcast_to(chunk_decay[:, None, None], (h, 1, RET_V_DIM))
    return decay, pair_lanes(k_decay), pair_lanes(q_decay), cdec


def _dispatch_plan(expert, t):
    a = t * TOP_K
    flat_e = expert.reshape(a)
    order = jnp.argsort(flat_e).astype(jnp.int32)
    counts = jnp.sum(flat_e[None, :] == jnp.arange(N_EXPERTS, dtype=flat_e.dtype)[:, None],
                     axis=1, dtype=jnp.int32)
    start = jnp.cumsum(counts) - counts
    blocks = (counts + MOE_BLOCK - 1) // MOE_BLOCK
    blk_end = jnp.cumsum(blocks)
    n_blocks = -(-a // MOE_BLOCK) + N_EXPERTS
    b = jnp.arange(n_blocks, dtype=jnp.int32)
    block_expert = jnp.minimum(jnp.sum(blk_end[None, :] <= b[:, None], axis=1, dtype=jnp.int32),
                               N_EXPERTS - 1)
    first = (b - (blk_end - blocks)[block_expert]) * MOE_BLOCK
    base = start[block_expert] + first
    nvalid = jnp.clip(counts[block_expert] - first, 0, MOE_BLOCK)
    nvalid = jnp.where(b < blk_end[-1], nvalid, 0)
    n_used = blk_end[-1:].astype(jnp.int32)
    order = jnp.pad(order, (0, MOE_BLOCK))
    src = order // TOP_K
    dst = (order % TOP_K) * t + src
    return block_expert, base.astype(jnp.int32), nvalid.astype(jnp.int32), src, dst, n_used


def _token_mixer_ln(h, hb, w_in, ret_gain, w_pool, pool_scale, w_branch, w_out, layer, ln_g, ln_b,
                    rope_a, rope_r, ret_tables, alpha):
    proj = _inproj(hb, w_in, layer, 0, OFF_GATE)
    gates = _inproj(hb, w_in, layer, OFF_GATE, N_IN - OFF_GATE, gate=True)
    qt, kk, vt = _moba_prep(proj, rope_a)
    y_a = _moba_attn(qt, kk, vt)
    y_r = _retention(proj, rope_r, ret_tables, ret_gain[None, :])
    y_p = _pool(proj, w_pool, layer, pool_scale[None, :])
    merged = _merge(y_a, y_r, y_p, w_branch, layer, gates)
    return _outproj_ln(merged, w_out, layer, h, ln_g[None, :], ln_b[None, :], alpha)


def _moe_ln(h, wr_hi, wr_lo, b_router, w_gate, w_up, w_down, layer, ln_g, ln_b, alpha):
    e_pad, p_pad = _router(h, wr_hi, wr_lo, b_router)
    plan = _dispatch_plan(e_pad[:, :TOP_K], h.shape[0])
    ycomb = _experts(h, w_gate, w_up, w_down, layer, plan)
    return _combine_ln(h, ycomb, p_pad, ln_g[None, :], ln_b[None, :], alpha)


def kernel(x, w_in, ret_gain, w_pool, pool_scale, w_branch, w_out, ln1_g, ln1_b, w_r1, b_r1, w_r2, b_r2, w_e_gate, w_e_up, w_e_down, ln2_g, ln2_b):
    bsz, seq, d = x.shape
    depth = w_in.shape[0]
    assert bsz == 1 and d == D_MODEL and seq % 1024 == 0
    t = seq
    alpha = float((2 * depth) ** 0.25)

    rope_a = _rope_tables(t, ROPE_DIM, ROPE_THETA, ATT_HEAD_DIM)
    rope_r = _rope_tables(t, RET_QK_DIM, RET_ROPE_THETA, RET_QK_DIM)
    ret_tables = _retention_tables()

    w_router = jnp.concatenate([w_r1, w_r2], axis=2)
    w_router = jnp.pad(w_router, ((0, 0), (0, 0), (0, ROUTER_LANES - w_router.shape[2])))
    wr_hi = w_router.astype(BF16)
    wr_lo = (w_router - wr_hi.astype(F32)).astype(BF16)
    b_router = jnp.pad(jnp.concatenate([b_r1, b_r2], axis=1),
                       ((0, 0), (0, ROUTER_LANES - N_GROUPS - N_EXPERTS)))[:, None, :]

    w_pool_b, w_branch_b, w_out_b = w_pool.astype(BF16), w_branch.astype(BF16), w_out.astype(BF16)
    h = x.reshape(t, d)
    hb = h.astype(BF16)
    for l in range(depth):
        h = _token_mixer_ln(h, hb, w_in, ret_gain[l], w_pool_b, pool_scale[l], w_branch_b, w_out_b, l,
                            ln1_g[l], ln1_b[l], rope_a, rope_r, ret_tables, alpha)
        h, hb = _moe_ln(h, wr_hi[l], wr_lo[l], b_router[l], w_e_gate, w_e_up, w_e_down, l,
                        ln2_g[l], ln2_b[l], alpha)
    return h.reshape(bsz, seq, d)
```
